```python
import jax
import jax.numpy as jnp
from jax import lax
import numpy as np

D_MODEL = 2048
BATCH = 4
SEQ = 2048
DEPTH = 4
DEC_BATCH = 128
DEC_SEQ = 1
PAST_LEN = 16384
PAGE_SIZE = 128

N_EVEN = (DEPTH + 1) // 2
N_ODD = DEPTH // 2
D_FF = 5632
N_MOD = 9
EPS = 1e-6
CHUNK = 64
NEG_BIG = -1e30

W_A = D_MODEL // 2
NB_A = 8
BS_A = W_A // NB_A
CONV_W = 4
C_RGLRU = 8.0
H_B = 8
DK_B = 128
DV_B = (D_MODEL // 2) // H_B
H_C = 4
DK_C = (D_MODEL // 4) // H_C
DV_C = (D_MODEL // 2) // H_C
N_D = 64
H_D = (D_MODEL // 2) // N_D
W_D = H_D * N_D
R_W = 64
R_A = 64
R_G = 128
GN_EPS_D = 64e-5

SIZES_EVEN = (W_A, W_A, H_B * DK_B, H_B * DK_B, H_B * DV_B, H_B * DV_B)
P_EVEN = sum(SIZES_EVEN)
SIZES_D = (W_D, W_D, W_D, R_W, R_A, R_G)
P_D = sum(SIZES_D)
SIZES_ODD = (H_C * DK_C, H_C * DK_C, H_C * DV_C, H_C * DV_C, H_C, H_C, P_D)
P_ODD = sum(SIZES_ODD)
MIX_EVEN = W_A + H_B * DV_B
MIX_ODD = H_C * DV_C + W_D

kernel_name = 'hybrid_rglru_hgrn2_mlstm_rwkv7_decode_step'


def _split(z, sizes):
    idx = np.cumsum(sizes)[:-1].tolist()
    return jnp.split(z, idx, axis=-1)


def _rms(x):
    xf = x.astype(jnp.float32)
    return xf * lax.rsqrt(jnp.mean(xf * xf, axis=-1, keepdims=True) + EPS)


def _adaln(t, shift, scale):
    return (_rms(t) * (1.0 + scale) + shift).astype(t.dtype)


def _swiglu(h, wg, wu, wd):
    return (jax.nn.silu(h @ wg) * (h @ wu)) @ wd


def _chunk_len(L):
    return CHUNK if L % CHUNK == 0 else L


def _to_chunks(x, cl):
    B, H, L = x.shape[:3]
    return jnp.moveaxis(x.reshape((B, H, L // cl, cl) + x.shape[3:]), 2, 0)


def _from_chunks(x):
    NC, B, H, cl = x.shape[:4]
    return jnp.moveaxis(x, 0, 2).reshape((B, H, NC * cl) + x.shape[4:])


def _causal_conv(u, buf, w, b):
    L = u.shape[1]
    full = jnp.concatenate([buf, u], axis=1)
    out = b + sum(full[:, k:k + L] * w[k] for k in range(CONV_W))
    return out, full[:, L:]


def _rglru(u, h0, wr, br, wi, bi, lam):
    B, L, _ = u.shape
    ub = u.reshape(B, L, NB_A, BS_A)
    r = jax.nn.sigmoid(jnp.einsum('blnc,ncd->blnd', ub, wr).reshape(B, L, W_A) + br)
    i = jax.nn.sigmoid(jnp.einsum('blnc,ncd->blnd', ub, wi).reshape(B, L, W_A) + bi)
    log_a = -C_RGLRU * r * jax.nn.softplus(-lam)
    a = jnp.exp(log_a)
    b = jnp.sqrt(-jnp.expm1(2.0 * log_a)) * (i * u)
    b = b.at[:, 0].add(a[:, 0] * h0)

    def combine(left, right):
        a1, b1 = left
        a2, b2 = right
        return a1 * a2, a2 * b1 + b2

    _, h = lax.associative_scan(combine, (a, b), axis=1)
    return h, h[:, -1]


def _gla_chunked(q, k, v, log_f, s0):
    cl = _chunk_len(q.shape[2])
    tril = jnp.tril(jnp.ones((cl, cl), dtype=bool))
    tril3 = tril[:, :, None]

    def body(S, inp):
        qc, kc, vc, fc = inp
        b = jnp.cumsum(fc, axis=2)
        g = b[:, :, -1]
        o = jnp.einsum('bhtk,bhkv->bhtv', qc * jnp.exp(b), S)
        diff = b[:, :, :, None, :] - b[:, :, None, :, :]
        dec = jnp.where(tril3, jnp.exp(jnp.where(tril3, diff, 0.0)), 0.0)
        att = jnp.einsum('bhtk,bhsk,bhtsk->bhts', qc, kc, dec)
        o = o + jnp.einsum('bhts,bhsv->bhtv', att, vc)
        S = jnp.exp(g)[..., None] * S + jnp.einsum('bhsk,bhsv->bhkv', kc * jnp.exp(g[:, :, None] - b), vc)
        return S, o

    S, o = lax.scan(body, s0, tuple(_to_chunks(t, cl) for t in (q, k, v, log_f)))
    return _from_chunks(o), S


def _hgrn2(q, fpre, iv, g, s0, lb, norm_w):
    B, L, _ = q.shape
    k = (1.0 - lb) * jax.nn.sigmoid(-fpre)
    log_f = jnp.log1p(-k)

    def heads(t, d):
        return t.reshape(B, L, H_B, d).transpose(0, 2, 1, 3)

    o, s1 = _gla_chunked(heads(q, DK_B), heads(k, DK_B), heads(iv, DV_B), heads(log_f, DK_B), s0)
    o = (_rms(o) * norm_w).transpose(0, 2, 1, 3).reshape(B, L, H_B * DV_B)
    return o * jax.nn.silu(g), s1


def _mlstm(q, k, v, o_pre, i_pre, f_pre, c0, n0, m0, norm_w):
    B, L, _ = q.shape

    def heads(t, d):
        return t.reshape(B, L, H_C, d).transpose(0, 2, 1, 3)

    qh = heads(q, DK_C)
    kh = heads(k, DK_C) * (DK_C ** -0.5)
    vh = heads(v, DV_C)
    ig = i_pre.transpose(0, 2, 1)
    lf = jax.nn.log_sigmoid(f_pre).transpose(0, 2, 1)
    cl = _chunk_len(L)
    tril = jnp.tril(jnp.ones((cl, cl), dtype=bool))

    def body(carry, inp):
        C, n, m = carry
        qc, kc, vc, ic, fc = inp
        b = jnp.cumsum(fc, axis=-1)
        dm = b[..., :, None] - b[..., None, :] + ic[..., None, :]
        dm = jnp.where(tril, dm, NEG_BIG)
        inter = b + m[..., None]
        m_t = jnp.maximum(inter, jnp.max(dm, axis=-1))
        w_inter = jnp.exp(inter - m_t)
        s = jnp.einsum('bhtk,bhsk->bhts', qc, kc) * jnp.exp(dm - m_t[..., None])
        num = w_inter[..., None] * jnp.einsum('bhtk,bhkv->bhtv', qc, C) + jnp.einsum('bhts,bhsv->bhtv', s, vc)
        den = w_inter * jnp.einsum('bhtk,bhk->bht', qc, n) + jnp.sum(s, axis=-1)
        h = num / jnp.maximum(jnp.abs(den), jnp.exp(-m_t))[..., None]
        g = b[..., -1]
        m_new = m_t[..., -1]
        wk = jnp.exp(g[..., None] - b + ic - m_new[..., None])
        f_state = jnp.exp(g + m - m_new)
        C = f_state[..., None, None] * C + jnp.einsum('bhs,bhsk,bhsv->bhkv', wk, kc, vc)
        n = f_state[..., None] * n + jnp.einsum('bhs,bhsk->bhk', wk, kc)
        return (C, n, m_new), h

    (c1, n1, m1), h = lax.scan(body, (c0, n0, m0), tuple(_to_chunks(t, cl) for t in (qh, kh, vh, ig, lf)))
    h = (_rms(_from_chunks(h)) * norm_w).transpose(0, 2, 1, 3).reshape(B, L, H_C * DV_C)
    return h * jax.nn.sigmoid(o_pre), c1, n1, m1


def _rwkv7(zd, prev, s0, mu, w0, w2, a0, a2, g2, k_k, k_a, r_k, ln_w, ln_b):
    B, L, _ = zd.shape
    shifted = jnp.concatenate([prev[:, None], zd[:, :-1]], axis=1)
    zs = zd + (shifted - zd) * mu
    r, k, v, wl, al, gl = _split(zs, SIZES_D)
    w = -jax.nn.softplus(-(w0 + jnp.tanh(wl) @ w2)) - 0.5
    decay = jnp.exp(-jnp.exp(w))
    a = jax.nn.sigmoid(a0 + al @ a2)
    gate = jax.nn.sigmoid(gl) @ g2

    def hd(t):
        return t.reshape(B, L, H_D, N_D)

    kk = hd(k * k_k)
    kk = kk / jnp.maximum(jnp.sqrt(jnp.sum(kk * kk, axis=-1, keepdims=True)), 1e-12)
    k = hd(k * (1.0 + (a - 1.0) * k_a))
    r, v, a, decay = hd(r), hd(v), hd(a), hd(decay)

    def step(S, inp):
        r_t, d_t, k_t, v_t, kk_t, a_t = inp
        sa = jnp.einsum('bhvk,bhk->bhv', S, -kk_t)
        S = S * d_t[:, :, None, :] + sa[..., None] * (kk_t * a_t)[:, :, None, :] + v_t[..., None] * k_t[:, :, None, :]
        return S, jnp.einsum('bhvk,bhk->bhv', S, r_t)

    s1, y = lax.scan(step, s0, tuple(jnp.moveaxis(t, 1, 0) for t in (r, decay, k, v, kk, a)))
    y = jnp.moveaxis(y, 0, 1)
    mean = jnp.mean(y, axis=-1, keepdims=True)
    var = jnp.mean(jnp.square(y - mean), axis=-1, keepdims=True)
    y = ((y - mean) * lax.rsqrt(var + GN_EPS_D)).reshape(B, L, W_D) * ln_w + ln_b
    bonus = jnp.sum(r * k * r_k, axis=-1, keepdims=True) * v
    y = (y + bonus.reshape(B, L, W_D)) * gate
    return y, zd[:, -1], s1


def _even_mixer(h, conv0, h0, s0, lb, p, j):
    z = (h @ p['w_in_even'][j]).astype(jnp.float32)
    ax, ag, bq, bf, bi, bg = _split(z, SIZES_EVEN)
    u, conv1 = _causal_conv(ax, conv0, p['a_conv_w'][j], p['a_conv_b'][j])
    ha, ha_last = _rglru(u, h0, p['a_gate_r_w'][j], p['a_gate_r_b'][j], p['a_gate_i_w'][j], p['a_gate_i_b'][j], p['a_lambda'][j])
    ya = ha * jax.nn.gelu(ag)
    yb, s1 = _hgrn2(bq, bf, bi, bg, s0, lb, p['b_norm_w'][j])
    y = jnp.concatenate([ya, yb], axis=-1).astype(h.dtype) @ p['w_out_even'][j]
    return y, conv1, ha_last, s1


def _odd_mixer(h, c0, n0, m0, prev0, sd0, p, j):
    z = (h @ p['w_in_odd'][j]).astype(jnp.float32)
    cq, ck, cv, co, ci, cf, zd = _split(z, SIZES_ODD)
    yc, c1, n1, m1 = _mlstm(cq, ck, cv, co, ci + p['c_igate_b'][j], cf + p['c_fgate_b'][j], c0, n0, m0, p['c_norm_w'][j])
    yd, prev1, sd1 = _rwkv7(zd, prev0, sd0, p['d_mu'][j], p['d_w0'][j], p['d_w2'][j], p['d_a0'][j], p['d_a2'][j], p['d_g2'][j], p['d_k_k'][j], p['d_k_a'][j], p['d_r_k'][j], p['d_ln_w'][j], p['d_ln_b'][j])
    y = jnp.concatenate([yc, yd], axis=-1).astype(h.dtype) @ p['w_out_odd'][j]
    return y, c1, n1, m1, prev1, sd1


def _trunk(x, c, a_conv, a_h, b_s, c_c, c_n, c_m, d_shift, d_s, p):
    f32 = jnp.float32
    sm = jax.nn.softmax(p['b_lb_gamma'].astype(f32), axis=0)
    lower_bounds = jnp.cumsum(sm, axis=0) - sm[0]
    cs = jax.nn.silu(c)
    n_a_conv, n_a_h, n_b_s = [], [], []
    n_c_c, n_c_n, n_c_m, n_d_shift, n_d_s = [], [], [], [], []
    for l in range(DEPTH):
        j = l // 2
        mod = (cs @ p['w_mod'][l] + p['b_mod'][l]).reshape(c.shape[0], N_MOD, 1, D_MODEL)
        hf = _adaln(x, mod[:, 0], mod[:, 1])
        x = x + 0.5 * (1.0 + mod[:, 2]) * _swiglu(hf, p['ffn_w_gate'][l, 0], p['ffn_w_up'][l, 0], p['ffn_w_down'][l, 0])
        h = _adaln(x, mod[:, 3], mod[:, 4])
        if l % 2 == 0:
            y, cv1, ha1, bs1 = _even_mixer(h, a_conv[j].astype(f32), a_h[j].astype(f32), b_s[j].astype(f32), lower_bounds[j], p, j)
            n_a_conv.append(cv1)
            n_a_h.append(ha1)
            n_b_s.append(bs1)
        else:
            y, cc1, cn1, cm1, ds1, dss1 = _odd_mixer(h, c_c[j].astype(f32), c_n[j].astype(f32), c_m[j].astype(f32), d_shift[j].astype(f32), d_s[j].astype(f32), p, j)
            n_c_c.append(cc1)
            n_c_n.append(cn1)
            n_c_m.append(cm1)
            n_d_shift.append(ds1)
            n_d_s.append(dss1)
        x = x + (1.0 + mod[:, 5]) * y
        hf = _adaln(x, mod[:, 6], mod[:, 7])
        x = x + 0.5 * (1.0 + mod[:, 8]) * _swiglu(hf, p['ffn_w_gate'][l, 1], p['ffn_w_up'][l, 1], p['ffn_w_down'][l, 1])
    y = (_rms(x) * p['final_norm_w']).astype(x.dtype)
    return (y, jnp.stack(n_a_conv), jnp.stack(n_a_h), jnp.stack(n_b_s), jnp.stack(n_c_c), jnp.stack(n_c_n), jnp.stack(n_c_m), jnp.stack(n_d_shift), jnp.stack(n_d_s))


def setup_inputs(seed: int = 0) -> dict:
    key = jax.random.key(seed)
    ks = iter(jax.random.split(key, 64))
    f32 = jnp.float32

    def nrm(shape, scale):
        return jax.random.normal(next(ks), shape, f32) * scale

    def unif(shape, lo, hi):
        return jax.random.uniform(next(ks), shape, f32, lo, hi)

    def gain(shape):
        return 1.0 + nrm(shape, 0.02)

    a_target = unif((N_EVEN, W_A), 0.9, 0.999) ** (1.0 / C_RGLRU)
    return {
        'x_prompt': nrm((BATCH, SEQ, D_MODEL), 1.0),
        'x_sample': nrm((DEC_BATCH, DEC_SEQ, D_MODEL), 1.0),
        'c_prompt': nrm((BATCH, D_MODEL), 1.0),
        'c_sample': nrm((DEC_BATCH, D_MODEL), 1.0),
        'state_a_conv': nrm((N_EVEN, DEC_BATCH, CONV_W - 1, W_A), 1.0),
        'state_a_h': nrm((N_EVEN, DEC_BATCH, W_A), 0.5),
        'state_b_s': nrm((N_EVEN, DEC_BATCH, H_B, DK_B, DV_B), 0.5),
        'state_c_c': nrm((N_ODD, DEC_BATCH, H_C, DK_C, DV_C), 0.5),
        'state_c_n': nrm((N_ODD, DEC_BATCH, H_C, DK_C), 0.5),
        'state_c_m': unif((N_ODD, DEC_BATCH, H_C), 0.0, 3.0),
        'state_d_shift': nrm((N_ODD, DEC_BATCH, P_D), 1.0),
        'state_d_s': nrm((N_ODD, DEC_BATCH, H_D, N_D, N_D), 0.3),
        'w_mod': nrm((DEPTH, D_MODEL, N_MOD * D_MODEL), 0.3 * D_MODEL ** -0.5),
        'b_mod': nrm((DEPTH, N_MOD * D_MODEL), 0.02),
        'ffn_w_gate': nrm((DEPTH, 2, D_MODEL, D_FF), D_MODEL ** -0.5),
        'ffn_w_up': nrm((DEPTH, 2, D_MODEL, D_FF), D_MODEL ** -0.5),
        'ffn_w_down': nrm((DEPTH, 2, D_FF, D_MODEL), D_FF ** -0.5),
        'w_in_even': nrm((N_EVEN, D_MODEL, P_EVEN), D_MODEL ** -0.5),
        'w_out_even': nrm((N_EVEN, MIX_EVEN, D_MODEL), MIX_EVEN ** -0.5),
        'a_conv_w': nrm((N_EVEN, CONV_W, W_A), CONV_W ** -0.5),
        'a_conv_b': nrm((N_EVEN, W_A), 0.02),
        'a_gate_r_w': nrm((N_EVEN, NB_A, BS_A, BS_A), BS_A ** -0.5),
        'a_gate_r_b': nrm((N_EVEN, W_A), 0.02),
        'a_gate_i_w': nrm((N_EVEN, NB_A, BS_A, BS_A), BS_A ** -0.5),
        'a_gate_i_b': nrm((N_EVEN, W_A), 0.02),
        'a_lambda': jnp.log(a_target) - jnp.log1p(-a_target),
        'b_lb_gamma': nrm((N_EVEN, H_B * DK_B), 1.0),
        'b_norm_w': gain((N_EVEN, DV_B)),
        'w_in_odd': nrm((N_ODD, D_MODEL, P_ODD), D_MODEL ** -0.5),
        'w_out_odd': nrm((N_ODD, MIX_ODD, D_MODEL), MIX_ODD ** -0.5),
        'c_igate_b': nrm((N_ODD, H_C), 0.1),
        'c_fgate_b': jnp.linspace(3.0, 6.0, H_C, dtype=f32)[None] + nrm((N_ODD, H_C), 0.1),
        'c_norm_w': gain((N_ODD, DV_C)),
        'd_mu': unif((N_ODD, P_D), 0.0, 1.0),
        'd_w0': unif((N_ODD, W_D), -6.0, 0.0),
        'd_w2': nrm((N_ODD, R_W, W_D), 0.5 * R_W ** -0.5),
        'd_a0': nrm((N_ODD, W_D), 0.1),
        'd_a2': nrm((N_ODD, R_A, W_D), R_A ** -0.5),
        'd_g2': nrm((N_ODD, R_G, W_D), R_G ** -0.5),
        'd_k_k': 0.85 + nrm((N_ODD, W_D), 0.02),
        'd_k_a': gain((N_ODD, W_D)),
        'd_r_k': nrm((N_ODD, H_D, N_D), 0.1),
        'd_ln_w': gain((N_ODD, W_D)),
        'd_ln_b': nrm((N_ODD, W_D), 0.02),
        'final_norm_w': gain((D_MODEL,)),
    }


def reference(x_prompt, x_sample, c_prompt, c_sample, state_a_conv, state_a_h, state_b_s, state_c_c, state_c_n, state_c_m, state_d_shift, state_d_s, w_mod, b_mod, ffn_w_gate, ffn_w_up, ffn_w_down, w_in_even, w_out_even, a_conv_w, a_conv_b, a_gate_r_w, a_gate_r_b, a_gate_i_w, a_gate_i_b, a_lambda, b_lb_gamma, b_norm_w, w_in_odd, w_out_odd, c_igate_b, c_fgate_b, c_norm_w, d_mu, d_w0, d_w2, d_a0, d_a2, d_g2, d_k_k, d_k_a, d_r_k, d_ln_w, d_ln_b, final_norm_w):
    p = dict(w_mod=w_mod, b_mod=b_mod, ffn_w_gate=ffn_w_gate, ffn_w_up=ffn_w_up, ffn_w_down=ffn_w_down,
             w_in_even=w_in_even, w_out_even=w_out_even, a_conv_w=a_conv_w, a_conv_b=a_conv_b,
             a_gate_r_w=a_gate_r_w, a_gate_r_b=a_gate_r_b, a_gate_i_w=a_gate_i_w, a_gate_i_b=a_gate_i_b,
             a_lambda=a_lambda, b_lb_gamma=b_lb_gamma, b_norm_w=b_norm_w, w_in_odd=w_in_odd,
             w_out_odd=w_out_odd, c_igate_b=c_igate_b, c_fgate_b=c_fgate_b, c_norm_w=c_norm_w,
             d_mu=d_mu, d_w0=d_w0, d_w2=d_w2, d_a0=d_a0, d_a2=d_a2, d_g2=d_g2, d_k_k=d_k_k,
             d_k_a=d_k_a, d_r_k=d_r_k, d_ln_w=d_ln_w, d_ln_b=d_ln_b, final_norm_w=final_norm_w)
    f32 = jnp.float32
    bp = x_prompt.shape[0]
    (y_prompt, pa_conv, pa_h, pb_s, pc_c, pc_n, pc_m, pd_shift, pd_s) = _trunk(
        x_prompt, c_prompt,
        jnp.zeros((N_EVEN, bp, CONV_W - 1, W_A), f32), jnp.zeros((N_EVEN, bp, W_A), f32),
        jnp.zeros((N_EVEN, bp, H_B, DK_B, DV_B), f32), jnp.zeros((N_ODD, bp, H_C, DK_C, DV_C), f32),
        jnp.zeros((N_ODD, bp, H_C, DK_C), f32), jnp.zeros((N_ODD, bp, H_C), f32),
        jnp.zeros((N_ODD, bp, P_D), f32), jnp.zeros((N_ODD, bp, H_D, N_D, N_D), f32), p)
    (y_sample, sa_conv, sa_h, sb_s, sc_c, sc_n, sc_m, sd_shift, sd_s) = _trunk(
        x_sample, c_sample, state_a_conv, state_a_h, state_b_s, state_c_c, state_c_n, state_c_m,
        state_d_shift, state_d_s, p)
    return (y_prompt, y_sample, pa_conv, pa_h, pb_s, pc_c, pc_n, pc_m, pd_shift, pd_s, sa_conv, sa_h, sb_s, sc_c, sc_n, sc_m, sd_shift, sd_s)
```

```python
import functools

import jax
import jax.numpy as jnp
from jax import lax
from jax.experimental import pallas as pl
from jax.experimental.pallas import tpu as pltpu

F32 = jnp.float32
BF16 = jnp.bfloat16

D = 2048
DEPTH = 4
N_EVEN = 2
N_ODD = 2
D_FF = 5632
N_MOD = 9
EPS = 1e-6
CHUNK = 64
NEG_BIG = -1e30

W_A = 1024
NB_A = 8
BS_A = 128
CONV_W = 4
C_RGLRU = 8.0
H_B = 8
DK_B = 128
DV_B = 128
H_C = 4
DK_C = 128
DV_C = 256
N_D = 64
H_D = 16
W_D = 1024
R_W = 64
R_A = 64
R_G = 128
GN_EPS_D = 64e-5
P_EVEN = 6144
P_D = 3328
P_ODD = 6408
P_ODD_PAD = 6656
PAD_L = 8
LANES = 128
VMEM_LIMIT = 48 * 2**20


def _cp(sem, vmem=VMEM_LIMIT):
    return pltpu.CompilerParams(dimension_semantics=sem, vmem_limit_bytes=vmem)


def _mm(a, b):
    return jnp.dot(a, b, preferred_element_type=F32)


def _dot(a, b):
    return _mm(a.astype(BF16), b.astype(BF16))


_NT = (((1,), (1,)), ((), ()))
_TN = (((0,), (0,)), ((), ()))
_NN = (((1,), (0,)), ((), ()))


def _dg(a, b, dn):
    return lax.dot_general(a, b, dn, preferred_element_type=F32)


def _dot_nt(a, b):
    return _dg(a.astype(BF16), b.astype(BF16), _NT)


def _dot_tn(a, b):
    return _dg(a.astype(BF16), b.astype(BF16), _TN)


def _split3(x):
    hi = x.astype(BF16)
    r1 = x - hi.astype(F32)
    mid = r1.astype(BF16)
    lo = (r1 - mid.astype(F32)).astype(BF16)
    return hi, mid, lo


def _split2(x):
    hi = x.astype(BF16)
    lo = (x - hi.astype(F32)).astype(BF16)
    return hi, lo


def _sel_dot(mask_bf, x):
    hi, mid, lo = _split3(x)
    return _mm(mask_bf, hi) + _mm(mask_bf, mid) + _mm(mask_bf, lo)


def _dot_sel(x, mask_bf):
    hi, mid, lo = _split3(x)
    return _mm(hi, mask_bf) + _mm(mid, mask_bf) + _mm(lo, mask_bf)


def _dot_sel_tn(x, mask_bf):
    hi, mid, lo = _split3(x)
    return _dg(hi, mask_bf, _TN) + _dg(mid, mask_bf, _TN) + _dg(lo, mask_bf, _TN)


def _mmx(a, b, dn=_NN):
    a_hi, a_lo = _split2(a)
    b_hi, b_lo = _split2(b)
    return _dg(a_hi, b_hi, dn) + _dg(a_hi, b_lo, dn) + _dg(a_lo, b_hi, dn)


def _iota(shape, dim):
    return lax.broadcasted_iota(jnp.int32, shape, dim)


def _sigmoid(x):
    return jax.nn.sigmoid(x)


def _softplus(x):
    return jnp.maximum(x, 0.0) + jnp.log1p(jnp.exp(-jnp.abs(x)))


def _log_sigmoid(x):
    return jnp.minimum(x, 0.0) - jnp.log1p(jnp.exp(-jnp.abs(x)))


def _gelu_tanh(x):
    return x * (0.5 * (1.0 + jnp.tanh(0.7978845608028654 * (x + 0.044715 * (x * x * x)))))


def _rms(x):
    return x * lax.rsqrt(jnp.mean(x * x, axis=-1, keepdims=True) + EPS)


def _adaln(x, shift, scale):
    return _rms(x) * (1.0 + scale) + shift


def _log2(n):
    k = n.bit_length() - 1
    assert (1 << k) == n
    return k


def _mod_spec(rows, width, layer, col):
    return pl.BlockSpec((None, None, rows, width), lambda g, i, n: (layer, g, 0, col))


def _mod_body(c_ref, w_ref, b_ref, o_ref):
    c = c_ref[...]
    cs = (c * _sigmoid(c)).astype(BF16)
    o_ref[...] = _mm(cs, w_ref[...].astype(BF16)) + b_ref[...]


def _modulation(c_all, w_mod, b_mod):
    rows = c_all.shape[0]
    tn = 1024
    return pl.pallas_call(
        _mod_body,
        grid=(DEPTH, N_MOD * D // tn),
        in_specs=[pl.BlockSpec((rows, D), lambda l, n: (0, 0)),
                  pl.BlockSpec((None, D, tn), lambda l, n: (l, 0, n)),
                  pl.BlockSpec((None, 1, tn), lambda l, n: (l, 0, n))],
        out_specs=pl.BlockSpec((None, rows, tn), lambda l, n: (l, 0, n)),
        out_shape=jax.ShapeDtypeStruct((DEPTH, rows, N_MOD * D), F32),
        compiler_params=_cp(("arbitrary", "arbitrary")),
        name="modulation",
    )(c_all, w_mod, b_mod.reshape(DEPTH, 1, N_MOD * D))


def _lower_bounds_body(g_ref, o_ref):
    g = g_ref[...]
    e = jnp.exp(g - jnp.max(g, axis=0, keepdims=True))
    sm = e / jnp.sum(e, axis=0, keepdims=True)
    acc = jnp.zeros_like(sm[0:1])
    for j in range(N_EVEN):
        acc = acc + sm[j:j + 1]
        o_ref[j:j + 1, :] = acc - sm[0:1]


def _lower_bounds(gamma):
    return pl.pallas_call(
        _lower_bounds_body,
        out_shape=jax.ShapeDtypeStruct(gamma.shape, F32),
        name="hgrn2_lower_bounds",
    )(gamma)


def _ffn_body(x_ref, sh_ref, sc_ref, gt_ref, wg_ref, wu_ref, wd_ref, o_ref, hf_ref, acc_ref):
    f = pl.program_id(2)

    @pl.when(f == 0)
    def _():
        hf_ref[...] = _adaln(x_ref[...], sh_ref[...], sc_ref[...]).astype(BF16)
        acc_ref[...] = jnp.zeros_like(acc_ref)

    hf = hf_ref[...]
    g = _mm(hf, wg_ref[...])
    u = _mm(hf, wu_ref[...])
    act = (g * _sigmoid(g) * u).astype(BF16)
    acc_ref[...] += _mm(act, wd_ref[...])

    @pl.when(f == pl.num_programs(2) - 1)
    def _():
        o_ref[...] = x_ref[...] + 0.5 * (1.0 + gt_ref[...]) * acc_ref[...]


def _ffn(x, mod, wg, wu, wd, layer, which, tm):
    G, Lg, _ = x.shape
    R = mod.shape[2]
    tf = 512
    kb = 6 * which
    return pl.pallas_call(
        _ffn_body,
        grid=(G, Lg // tm, D_FF // tf),
        in_specs=[pl.BlockSpec((None, tm, D), lambda g, i, f: (g, i, 0)),
                  _mod_spec(R, D, layer, kb), _mod_spec(R, D, layer, kb + 1), _mod_spec(R, D, layer, kb + 2),
                  pl.BlockSpec((None, None, D, tf), lambda g, i, f: (layer, which, 0, f)),
                  pl.BlockSpec((None, None, D, tf), lambda g, i, f: (layer, which, 0, f)),
                  pl.BlockSpec((None, None, tf, D), lambda g, i, f: (layer, which, f, 0))],
        out_specs=pl.BlockSpec((None, tm, D), lambda g, i, f: (g, i, 0)),
        out_shape=jax.ShapeDtypeStruct(x.shape, F32),
        scratch_shapes=[pltpu.VMEM((tm, D), BF16), pltpu.VMEM((tm, D), F32)],
        compiler_params=_cp(("arbitrary", "arbitrary", "arbitrary")),
        name="adaln_swiglu_ffn",
    )(x, mod, mod, mod, wg, wu, wd)


def _inproj_body(x_ref, sh_ref, sc_ref, w_ref, o_ref, hf_ref):
    @pl.when(pl.program_id(2) == 0)
    def _():
        hf_ref[...] = _adaln(x_ref[...], sh_ref[...], sc_ref[...]).astype(BF16)

    o_ref[...] = _mm(hf_ref[...], w_ref[...])


def _inproj(x, mod, w, layer, j, tm):
    G, Lg, _ = x.shape
    R = mod.shape[2]
    P = w.shape[-1]
    tn = 512
    return pl.pallas_call(
        _inproj_body,
        grid=(G, Lg // tm, P // tn),
        in_specs=[pl.BlockSpec((None, tm, D), lambda g, i, n: (g, i, 0)),
                  _mod_spec(R, D, layer, 3), _mod_spec(R, D, layer, 4),
                  pl.BlockSpec((None, D, tn), lambda g, i, n: (j, 0, n))],
        out_specs=pl.BlockSpec((None, tm, tn), lambda g, i, n: (g, i, n)),
        out_shape=jax.ShapeDtypeStruct((G, Lg, P), F32),
        scratch_shapes=[pltpu.VMEM((tm, D), BF16)],
        compiler_params=_cp(("arbitrary", "arbitrary", "arbitrary")),
        name="adaln_in_projection",
    )(x, mod, mod, w)


def _outproj_body(x_ref, gt_ref, y1_ref, y2_ref, w1_ref, w2_ref, o_ref):
    y = _dot(y1_ref[...], w1_ref[...]) + _dot(y2_ref[...], w2_ref[...])
    o_ref[...] = x_ref[...] + (1.0 + gt_ref[...]) * y


def _outproj(x, mod, y1, y2, w, layer, j, tm):
    G, Lg, _ = x.shape
    R = mod.shape[2]
    tn = 1024
    nb = D // tn
    half = y1.shape[-1]
    return pl.pallas_call(
        _outproj_body,
        grid=(G, Lg // tm, nb),
        in_specs=[pl.BlockSpec((None, tm, tn), lambda g, i, n: (g, i, n)),
                  pl.BlockSpec((None, None, R, tn), lambda g, i, n: (layer, g, 0, 5 * nb + n)),
                  pl.BlockSpec((None, tm, half), lambda g, i, n: (g, i, 0)),
                  pl.BlockSpec((None, tm, half), lambda g, i, n: (g, i, 0)),
                  pl.BlockSpec((None, None, half, tn), lambda g, i, n: (j, 0, 0, n)),
                  pl.BlockSpec((None, None, half, tn), lambda g, i, n: (j, 1, 0, n))],
        out_specs=pl.BlockSpec((None, tm, tn), lambda g, i, n: (g, i, n)),
        out_shape=jax.ShapeDtypeStruct(x.shape, F32),
        compiler_params=_cp(("arbitrary", "arbitrary", "arbitrary")),
        name="out_projection_residual",
    )(x, mod, y1, y2, w, w)


def _final_body(x_ref, w_ref, o_ref):
    o_ref[...] = _rms(x_ref[...]) * w_ref[...]


def _final_norm(x, w, tm):
    G, Lg, _ = x.shape
    return pl.pallas_call(
        _final_body,
        grid=(G, Lg // tm),
        in_specs=[pl.BlockSpec((None, tm, D), lambda g, i: (g, i, 0)),
                  pl.BlockSpec((1, D), lambda g, i: (0, 0))],
        out_specs=pl.BlockSpec((None, tm, D), lambda g, i: (g, i, 0)),
        out_shape=jax.ShapeDtypeStruct(x.shape, F32),
        compiler_params=_cp(("arbitrary", "arbitrary")),
        name="final_rmsnorm",
    )(x, w.reshape(1, D))


def _rglru_gates(x0, x1, x2, x3, cw, cb, wr, br, wi, bi, lam):
    u = cb + x0 * cw[0:1] + x1 * cw[1:2] + x2 * cw[2:3] + x3 * cw[3:4]
    r_parts, i_parts = [], []
    for n in range(NB_A):
        un = u[:, n * BS_A:(n + 1) * BS_A].astype(BF16)
        r_parts.append(_mm(un, wr[n]))
        i_parts.append(_mm(un, wi[n]))
    r = _sigmoid(jnp.concatenate(r_parts, axis=1) + br)
    ig = _sigmoid(jnp.concatenate(i_parts, axis=1) + bi)
    log_a = -C_RGLRU * r * _softplus(-lam)
    a = jnp.exp(log_a)
    b = jnp.sqrt(-jnp.tanh(log_a) * (a * a + 1.0)) * (ig * u)
    return a, b


def _rglru_seq_body(x_ref, halo_ref, buf_ref, cw_ref, cb_ref, wr_ref, br_ref, wi_ref, bi_ref, lam_ref,
                    a_ref, b_ref, *, tt):
    i = pl.program_id(1)
    halo = jnp.where(i == 0, buf_ref[...], halo_ref[...])
    full = jnp.concatenate([halo, x_ref[...]], axis=0)
    taps = [pltpu.roll(full, CONV_W - 1 - k, axis=0)[8:8 + tt] for k in range(CONV_W - 1)]
    a, b = _rglru_gates(taps[0], taps[1], taps[2], x_ref[...], cw_ref[...], cb_ref[...], wr_ref[...], br_ref[...],
                        wi_ref[...], bi_ref[...], lam_ref[...])
    a_ref[...] = a
    b_ref[...] = b


def _rglru_param_specs(nidx):
    z2 = lambda *_: (0, 0)
    z3 = lambda *_: (0, 0, 0)
    del nidx
    return [pl.BlockSpec((CONV_W, W_A), z2), pl.BlockSpec((1, W_A), z2),
            pl.BlockSpec((NB_A, BS_A, BS_A), z3), pl.BlockSpec((1, W_A), z2),
            pl.BlockSpec((NB_A, BS_A, BS_A), z3), pl.BlockSpec((1, W_A), z2),
            pl.BlockSpec((1, W_A), z2)]


def _rglru_seq(z, buf8, prm, tt):
    B, L, _ = z.shape
    hb = tt // 8
    out = jax.ShapeDtypeStruct((B, L, W_A), F32)
    return pl.pallas_call(
        functools.partial(_rglru_seq_body, tt=tt),
        grid=(B, L // tt),
        in_specs=[pl.BlockSpec((None, tt, W_A), lambda b, i: (b, i, 0)),
                  pl.BlockSpec((None, 8, W_A), lambda b, i: (b, jnp.maximum(i * hb - 1, 0), 0)),
                  pl.BlockSpec((None, 8, W_A), lambda b, i: (b, 0, 0))] + _rglru_param_specs(2),
        out_specs=[pl.BlockSpec((None, tt, W_A), lambda b, i: (b, i, 0))] * 2,
        out_shape=[out, out],
        compiler_params=_cp(("arbitrary", "arbitrary")),
        name="rglru_conv_gates",
    )(z, z, buf8, *prm)


def _rglru_scan_body(a_ref, b_ref, ag_ref, h0_ref, y_ref, hl_ref, h_scr, *, tt):
    @pl.when(pl.program_id(1) == 0)
    def _():
        h_scr[...] = h0_ref[...]

    def step(t, h):
        h = a_ref[t] * h + b_ref[t]
        y_ref[t] = h
        return h

    h = lax.fori_loop(0, tt, step, h_scr[...], unroll=8)
    h_scr[...] = h
    hl_ref[...] = h
    y_ref[...] = y_ref[...] * _gelu_tanh(ag_ref[...])


def _rglru_scan(a, b, z, h0, tt):
    B, L, _ = a.shape
    a4 = a.reshape(B, L, 8, LANES)
    b4 = b.reshape(B, L, 8, LANES)
    z4 = z.reshape(B, L, z.shape[-1] // LANES, LANES)
    spec = pl.BlockSpec((None, tt, 8, LANES), lambda bb, i: (bb, i, 0, 0))
    y, hl = pl.pallas_call(
        functools.partial(_rglru_scan_body, tt=tt),
        grid=(B, L // tt),
        in_specs=[spec, spec,
                  pl.BlockSpec((None, tt, 8, LANES), lambda bb, i: (bb, i, 1, 0)),
                  pl.BlockSpec((None, 8, LANES), lambda bb, i: (bb, 0, 0))],
        out_specs=[spec, pl.BlockSpec((None, 8, LANES), lambda bb, i: (bb, 0, 0))],
        out_shape=[jax.ShapeDtypeStruct((B, L, 8, LANES), F32), jax.ShapeDtypeStruct((B, 8, LANES), F32)],
        scratch_shapes=[pltpu.VMEM((8, LANES), F32)],
        compiler_params=_cp(("arbitrary", "arbitrary")),
        name="rglru_scan_gelu_gate",
    )(a4, b4, z4, h0.reshape(B, 8, LANES))
    return y.reshape(B, L, W_A), hl.reshape(B, W_A)


def _rglru_step_body(x3_ref, ag_ref, x0_ref, x1_ref, x2_ref, h0_ref, cw_ref, cb_ref, wr_ref, br_ref, wi_ref, bi_ref,
                     lam_ref, y_ref, h_ref):
    a, b = _rglru_gates(x0_ref[...], x1_ref[...], x2_ref[...], x3_ref[...], cw_ref[...], cb_ref[...], wr_ref[...],
                        br_ref[...], wi_ref[...], bi_ref[...], lam_ref[...])
    h = a * h0_ref[...] + b
    h_ref[...] = h
    y_ref[...] = h * _gelu_tanh(ag_ref[...])


def _rglru_step(z, buf, h0, prm):
    n = z.shape[0]
    row = lambda c: pl.BlockSpec((n, W_A), lambda i: (0, c))
    out = jax.ShapeDtypeStruct((n, W_A), F32)
    return pl.pallas_call(
        _rglru_step_body,
        grid=(1,),
        in_specs=[row(0), row(1), row(0), row(0), row(0), row(0)] + _rglru_param_specs(1),
        out_specs=[row(0), row(0)],
        out_shape=[out, out],
        compiler_params=_cp(("arbitrary",)),
        name="rglru_decode_step",
    )(z, z, buf[:, 0], buf[:, 1], buf[:, 2], h0, *prm)


def _hgrn2_chunk(q, fp, v, gate, lb, nw, S, *, C, valid):
    SB = min(16, C)
    k = (1.0 - lb) * _sigmoid(-fp)
    lf = jnp.log1p(-k)
    if valid < C:
        live = _iota((C, 1), 0) < valid
        k = jnp.where(live, k, 0.0)
        lf = jnp.where(live, lf, 0.0)
    tril = _iota((C, C), 0) >= _iota((C, C), 1)
    b = _sel_dot(jnp.where(tril, 1.0, 0.0).astype(BF16), lf)
    g_row = b[C - 1:C, :]
    o = _dot(q * jnp.exp(b), S)
    khat = k * jnp.exp(g_row - b)
    g_col = _dot_sel_tn(lf, jnp.ones((C, DV_B), BF16))
    S_new = jnp.exp(g_col) * S + _dot_tn(khat, v)

    tril_sb = _iota((SB, SB), 0) >= _iota((SB, SB), 1)
    parts = []
    for i in range(C // SB):
        lo = i * SB
        qi, bi, ki, vi = q[lo:lo + SB], b[lo:lo + SB], k[lo:lo + SB], v[lo:lo + SB]
        dec = jnp.exp(jnp.minimum(bi[:, None, :] - bi[None, :, :], 0.0))
        att = jnp.sum(qi[:, None, :] * ki[None, :, :] * dec, axis=-1)
        oi = _dot(jnp.where(tril_sb, att, 0.0), vi)
        if i > 0:
            ref = b[lo - 1:lo, :]
            qt = qi * jnp.exp(bi - ref)
            kt = k[0:lo] * jnp.exp(ref - b[0:lo])
            oi = oi + _dot(_dot_nt(qt, kt), v[0:lo])
        parts.append(oi)
    o = o + (parts[0] if len(parts) == 1 else jnp.concatenate(parts, axis=0))
    y = _rms(o) * nw * (gate * _sigmoid(gate))
    return y, S_new


def _hgrn2_body(q_ref, f_ref, v_ref, g_ref, lb_ref, nw_ref, s0_ref, y_ref, s1_ref, *, C, valid, nb, nchunk):
    @pl.when(pl.program_id(2) == 0)
    def _():
        s1_ref[...] = s0_ref[...]

    lb = lb_ref[...]
    nw = nw_ref[...]

    def seq(bi, carry):
        def chunk(c, carry2):
            rows = pl.ds(pl.multiple_of(c * C, C), C)
            y, s_new = _hgrn2_chunk(q_ref[bi, rows, :], f_ref[bi, rows, :], v_ref[bi, rows, :], g_ref[bi, rows, :],
                                    lb, nw, s1_ref[bi], C=C, valid=valid)
            y_ref[bi, rows, :] = y
            s1_ref[bi] = s_new
            return carry2

        return lax.fori_loop(0, nchunk, chunk, carry)

    lax.fori_loop(0, nb, seq, 0)


def _hgrn2(z, lb, nw, s0, *, C, valid, nb, tt):
    B, L, _ = z.shape
    col = lambda base: pl.BlockSpec((nb, tt, LANES), lambda b, h, i: (b, i, base + h))
    st = pl.BlockSpec((nb, None, DK_B, DV_B), lambda b, h, i: (b, h, 0, 0))
    return pl.pallas_call(
        functools.partial(_hgrn2_body, C=C, valid=valid, nb=nb, nchunk=tt // C),
        grid=(B // nb, H_B, L // tt),
        in_specs=[col(16), col(24), col(32), col(40),
                  pl.BlockSpec((1, LANES), lambda b, h, i: (0, h)),
                  pl.BlockSpec((1, LANES), lambda b, h, i: (0, 0)), st],
        out_specs=[col(0), st],
        out_shape=[jax.ShapeDtypeStruct((B, L, H_B * DV_B), F32), jax.ShapeDtypeStruct(s0.shape, F32)],
        compiler_params=_cp(("arbitrary", "arbitrary", "arbitrary")),
        name="hgrn2_chunked",
    )(z, z, z, z, lb, nw, s0)


def _mlstm_chunk(q, k, v, op, gt, gtT, h, nw, Cst, n, m, *, C, valid):
    lane = _iota((C, LANES), 1)
    sub = _iota((LANES, C), 0)
    i_col = jnp.sum(jnp.where(lane == h, gt, 0.0), axis=1, keepdims=True)
    f_col = jnp.sum(jnp.where(lane == h + H_C, gt, 0.0), axis=1, keepdims=True)
    i_row = jnp.sum(jnp.where(sub == h, gtT, 0.0), axis=0, keepdims=True)
    f_row = jnp.sum(jnp.where(sub == h + H_C, gtT, 0.0), axis=0, keepdims=True)
    lf_col = _log_sigmoid(f_col)
    lf_row = _log_sigmoid(f_row)
    if valid < C:
        live_c = _iota((C, 1), 0) < valid
        live_r = _iota((1, C), 1) < valid
        i_col = jnp.where(live_c, i_col, NEG_BIG)
        i_row = jnp.where(live_r, i_row, NEG_BIG)
        lf_col = jnp.where(live_c, lf_col, 0.0)
        lf_row = jnp.where(live_r, lf_row, 0.0)
    tril = _iota((C, C), 0) >= _iota((C, C), 1)
    b_col = jnp.sum(jnp.where(tril, lf_row, 0.0), axis=1, keepdims=True)
    b_row = jnp.sum(jnp.where(_iota((C, C), 0) <= _iota((C, C), 1), lf_col, 0.0), axis=0, keepdims=True)
    dm = jnp.where(tril, b_col - b_row + i_row, NEG_BIG)
    inter = b_col + m
    m_t = jnp.maximum(inter, jnp.max(dm, axis=1, keepdims=True))
    w_inter = jnp.exp(inter - m_t)
    kh = k * (DK_C ** -0.5)
    s = _dot_nt(q, kh) * jnp.exp(dm - m_t)
    num = w_inter * _dot(q, Cst) + _dot(s, v)
    den = w_inter * jnp.sum(q * n, axis=1, keepdims=True) + jnp.sum(s, axis=1, keepdims=True)
    hh = num / jnp.maximum(jnp.abs(den), jnp.exp(-m_t))
    g = b_col[C - 1:C, :]
    m_new = m_t[C - 1:C, :]
    wk = jnp.exp(g - b_col + i_col - m_new)
    f_state = jnp.exp(g + m - m_new)
    kw = kh * wk
    C_new = f_state * Cst + _dot_tn(kw, v)
    n_new = f_state * n + jnp.sum(kw, axis=0, keepdims=True)
    y = _rms(hh) * nw * _sigmoid(op)
    return y, C_new, n_new, m_new


def _mlstm_body(q_ref, k_ref, v_ref, o_ref, g_ref, gb_ref, nw_ref, c0_ref, n0_ref, m0_ref,
                y_ref, c1_ref, n1_ref, m1_ref, *, C, valid, nb, nchunk):
    @pl.when(pl.program_id(2) == 0)
    def _():
        c1_ref[...] = c0_ref[...]
        n1_ref[...] = n0_ref[...]
        m1_ref[...] = m0_ref[...]

    h = pl.program_id(1)
    gb = gb_ref[...]
    nw = nw_ref[...]

    def seq(bi, carry):
        def chunk(c, carry2):
            rows = pl.ds(pl.multiple_of(c * C, C), C)
            gt = g_ref[bi, rows, :] + gb
            y, c_new, n_new, m_new = _mlstm_chunk(
                q_ref[bi, rows, :], k_ref[bi, rows, :], v_ref[bi, rows, :], o_ref[bi, rows, :], gt, gt.T, h, nw,
                c1_ref[bi], n1_ref[bi], m1_ref[bi][:, 0:1], C=C, valid=valid)
            y_ref[bi, rows, :] = y
            c1_ref[bi] = c_new
            n1_ref[bi] = n_new
            m1_ref[bi] = jnp.broadcast_to(m_new, (1, LANES))
            return carry2

        return lax.fori_loop(0, nchunk, chunk, carry)

    lax.fori_loop(0, nb, seq, 0)


def _mlstm(z, gb, nw, c0, n0, m0, *, C, valid, nb, tt):
    B, L, _ = z.shape
    c128 = lambda base: pl.BlockSpec((nb, tt, LANES), lambda b, h, i: (b, i, base + h))
    c256 = lambda base: pl.BlockSpec((nb, tt, DV_C), lambda b, h, i: (b, i, base + h))
    cst = pl.BlockSpec((nb, None, DK_C, DV_C), lambda b, h, i: (b, h, 0, 0))
    vec = pl.BlockSpec((nb, None, 1, LANES), lambda b, h, i: (b, h, 0, 0))
    return pl.pallas_call(
        functools.partial(_mlstm_body, C=C, valid=valid, nb=nb, nchunk=tt // C),
        grid=(B // nb, H_C, L // tt),
        in_specs=[c128(26), c128(30), c256(17), c256(21),
                  pl.BlockSpec((nb, tt, LANES), lambda b, h, i: (b, i, 50)),
                  pl.BlockSpec((1, LANES), lambda b, h, i: (0, 0)),
                  pl.BlockSpec((1, DV_C), lambda b, h, i: (0, 0)), cst, vec, vec],
        out_specs=[c256(0), cst, vec, vec],
        out_shape=[jax.ShapeDtypeStruct((B, L, H_C * DV_C), F32), jax.ShapeDtypeStruct(c0.shape, F32),
                   jax.ShapeDtypeStruct(n0.shape, F32), jax.ShapeDtypeStruct(m0.shape, F32)],
        compiler_params=_cp(("arbitrary", "arbitrary", "arbitrary")),
        name="mlstm_chunked",
    )(z, z, z, z, z, gb, nw, c0, n0, m0)


def _head_sum(x, bd):
    return jnp.concatenate([_dot_sel(x[:, n * LANES:(n + 1) * LANES], bd) for n in range(W_D // LANES)], axis=1)


def _rwkv_pre(zd, sh, mu, w0, a0, kkw, ka, rk, w2p, a2p, g2, bd):
    zs = zd + (sh - zd) * mu
    r, k, v = zs[:, 0:W_D], zs[:, W_D:2 * W_D], zs[:, 2 * W_D:3 * W_D]
    wa = zs[:, 3 * W_D:3 * W_D + LANES]
    gl = zs[:, 3 * W_D + LANES:]
    w = -_softplus(-(w0 + _dot(jnp.tanh(wa), w2p))) - 0.5
    logd = -jnp.exp(w)
    a = _sigmoid(a0 + _dot(wa, a2p))
    gate = _dot(_sigmoid(gl), g2)
    kk = k * kkw
    kk = kk / jnp.maximum(jnp.sqrt(_head_sum(kk * kk, bd)), 1e-12)
    kmod = k * (1.0 + (a - 1.0) * ka)
    bonus = _head_sum(r * kmod * rk, bd) * v
    return r, logd, kmod, v, kk, kk * a, gate, bonus


def _head_block_ones():
    return jnp.where((_iota((LANES, LANES), 0) >> 6) == (_iota((LANES, LANES), 1) >> 6), 1.0, 0.0).astype(BF16)


def _rwkv_pre_seq_body(z_ref, halo_ref, prev_ref, mu_ref, w0_ref, a0_ref, kk_ref, ka_ref, rk_ref, w2_ref, a2_ref,
                       g2_ref, *out_refs, tt):
    i = pl.program_id(1)
    halo = jnp.where(i == 0, prev_ref[...], halo_ref[...])
    zd = z_ref[...]
    sh = pltpu.roll(jnp.concatenate([halo, zd], axis=0), 1, axis=0)[8:8 + tt]
    outs = _rwkv_pre(zd, sh, mu_ref[...], w0_ref[...], a0_ref[...], kk_ref[...], ka_ref[...], rk_ref[...],
                     w2_ref[...], a2_ref[...], g2_ref[...], _head_block_ones())
    for ref, val in zip(out_refs, outs):
        ref[...] = val


def _rwkv_pre_step_body(z_ref, sh_ref, mu_ref, w0_ref, a0_ref, kk_ref, ka_ref, rk_ref, w2_ref, a2_ref, g2_ref,
                        *out_refs):
    outs = _rwkv_pre(z_ref[...], sh_ref[...], mu_ref[...], w0_ref[...], a0_ref[...], kk_ref[...], ka_ref[...],
                     rk_ref[...], w2_ref[...], a2_ref[...], g2_ref[...], _head_block_ones())
    for ref, val in zip(out_refs, outs):
        ref[...] = val


def _rwkv_pre_param_specs():
    z2 = lambda *_: (0, 0)
    return ([pl.BlockSpec((1, P_D), z2)] + [pl.BlockSpec((1, W_D), z2)] * 5
            + [pl.BlockSpec((LANES, W_D), z2)] * 3)


def _rwkv_pre_seq(z, prev8, prm, tt):
    B, L, _ = z.shape
    hb = tt // 8
    out = jax.ShapeDtypeStruct((B, L, W_D), F32)
    return pl.pallas_call(
        functools.partial(_rwkv_pre_seq_body, tt=tt),
        grid=(B, L // tt),
        in_specs=[pl.BlockSpec((None, tt, P_D), lambda b, i: (b, i, 0)),
                  pl.BlockSpec((None, 8, P_D), lambda b, i: (b, jnp.maximum(i * hb - 1, 0), 0)),
                  pl.BlockSpec((None, 8, P_D), lambda b, i: (b, 0, 0))] + _rwkv_pre_param_specs(),
        out_specs=[pl.BlockSpec((None, tt, W_D), lambda b, i: (b, i, 0))] * 8,
        out_shape=[out] * 8,
        compiler_params=_cp(("arbitrary", "arbitrary")),
        name="rwkv7_token_shift_features",
    )(z, z, prev8, *prm)


def _rwkv_pre_step(z, shifted, prm):
    n = z.shape[0]
    out = jax.ShapeDtypeStruct((n, W_D), F32)
    return pl.pallas_call(
        _rwkv_pre_step_body,
        grid=(1,),
        in_specs=[pl.BlockSpec((n, P_D), lambda i: (0, 0)), pl.BlockSpec((n, P_D), lambda i: (0, 0))]
        + _rwkv_pre_param_specs(),
        out_specs=[pl.BlockSpec((n, W_D), lambda i: (0, 0))] * 8,
        out_shape=[out] * 8,
        compiler_params=_cp(("arbitrary",)),
        name="rwkv7_decode_features",
    )(z, shifted, *prm)


def _unit_lower_inverse(N, C):
    n = N.shape[0]
    ri, ci = _iota((n, n), 0), _iota((n, n), 1)
    base = min(8, C)
    kb = _log2(base)
    X = jnp.where((ri >> kb) == (ci >> kb), N, 0.0)
    T = jnp.where(ri == ci, 1.0, 0.0) + X
    for _ in range(kb - 1):
        X = _mmx(X, X)
        T = T + _mmx(T, X)
    size = base
    while size < C:
        ks = _log2(size)
        sel = ((ri >> (ks + 1)) == (ci >> (ks + 1))) & (((ri >> ks) & 1) == 1) & (((ci >> ks) & 1) == 0)
        T = T + _mmx(_mmx(T, jnp.where(sel, N, 0.0)), T)
        size *= 2
    return T


def _rwkv_chunk(r, ld, k, v, kk, be, gate, bonus, lnw, lnb, P, *, C, valid):
    if valid < C:
        live = _iota((C, 1), 0) < valid
        ld = jnp.where(live, ld, 0.0)
        be = jnp.where(live, be, 0.0)
        k = jnp.where(live, k, 0.0)
        v = jnp.where(live, v, 0.0)
    tril = _iota((C, C), 0) >= _iota((C, C), 1)
    c = _sel_dot(jnp.where(tril, 1.0, 0.0).astype(BF16), ld)
    ec, enc = jnp.exp(c), jnp.exp(-c)
    At = -kk * jnp.exp(c - ld)
    Bt = be * enc
    Kt = k * enc
    Rt = r * ec
    lane0 = _iota((C, LANES), 1) < N_D

    def stack(x):
        return jnp.concatenate([jnp.where(lane0, x, 0.0), jnp.where(lane0, 0.0, x)], axis=0)

    A2, R2, B2, K2, V2 = stack(At), stack(Rt), stack(Bt), stack(Kt), stack(v)
    n2 = 2 * C
    kc = _log2(C)
    ri, ci = _iota((n2, n2), 0), _iota((n2, n2), 1)
    same = (ri >> kc) == (ci >> kc)
    tpos, spos = ri & (C - 1), ci & (C - 1)
    strict = same & (spos < tpos)
    incl = same & (spos <= tpos)
    N = jnp.where(strict, _mmx(A2, B2, _NT), 0.0)
    AK = jnp.where(strict, _mmx(A2, K2, _NT), 0.0)
    RB = jnp.where(incl, _mmx(R2, B2, _NT), 0.0)
    RK = jnp.where(incl, _mmx(R2, K2, _NT), 0.0)
    T = _unit_lower_inverse(N, C)
    W = _mmx(A2, P) + _mmx(AK, V2)
    U = _mmx(T, W)
    Y2 = _mmx(R2, P) + _mmx(RB, U) + _mmx(RK, V2)
    y = Y2[0:C] + Y2[C:n2]
    g_col = _dot_sel_tn(ld, jnp.ones((C, LANES), BF16))
    P_new = jnp.exp(g_col) * (P + _mmx(B2, U, _TN) + _mmx(K2, V2, _TN))

    bd = _head_block_ones()
    mean = _dot_sel(y, bd) * (1.0 / N_D)
    dy = y - mean
    var = _dot_sel(dy * dy, bd) * (1.0 / N_D)
    out = (dy * lax.rsqrt(var + GN_EPS_D) * lnw + lnb + bonus) * gate
    return out, P_new


def _rwkv_body(r_ref, ld_ref, k_ref, v_ref, kk_ref, be_ref, gt_ref, bo_ref, lnw_ref, lnb_ref, p0_ref,
               y_ref, p1_ref, *, C, valid, nb, nchunk):
    @pl.when(pl.program_id(2) == 0)
    def _():
        p1_ref[...] = p0_ref[...]

    lnw = lnw_ref[...]
    lnb = lnb_ref[...]

    def seq(bi, carry):
        def chunk(c, carry2):
            rows = pl.ds(pl.multiple_of(c * C, C), C)
            y, p_new = _rwkv_chunk(r_ref[bi, rows, :], ld_ref[bi, rows, :], k_ref[bi, rows, :], v_ref[bi, rows, :],
                                   kk_ref[bi, rows, :], be_ref[bi, rows, :], gt_ref[bi, rows, :], bo_ref[bi, rows, :],
                                   lnw, lnb, p1_ref[bi], C=C, valid=valid)
            y_ref[bi, rows, :] = y
            p1_ref[bi] = p_new
            return carry2

        return lax.fori_loop(0, nchunk, chunk, carry)

    lax.fori_loop(0, nb, seq, 0)


def _rwkv(feats, lnw, lnb, p0, *, C, valid, nb, tt):
    B, L, _ = feats[0].shape
    col = pl.BlockSpec((nb, tt, LANES), lambda b, p, i: (b, i, p))
    vec = pl.BlockSpec((1, LANES), lambda b, p, i: (0, p))
    st = pl.BlockSpec((nb, None, LANES, LANES), lambda b, p, i: (b, p, 0, 0))
    return pl.pallas_call(
        functools.partial(_rwkv_body, C=C, valid=valid, nb=nb, nchunk=tt // C),
        grid=(B // nb, H_D // 2, L // tt),
        in_specs=[col] * 8 + [vec, vec, st],
        out_specs=[col, st],
        out_shape=[jax.ShapeDtypeStruct((B, L, W_D), F32), jax.ShapeDtypeStruct(p0.shape, F32)],
        compiler_params=_cp(("arbitrary", "arbitrary", "arbitrary")),
        name="rwkv7_chunked",
    )(*feats, lnw, lnb, p0)


def _pair_blockdiag(s):
    B = s.shape[0]
    st = jnp.swapaxes(s, -1, -2).reshape(B, H_D // 2, 2, N_D, N_D)
    zero = jnp.zeros_like(st[:, :, 0])
    top = jnp.concatenate([st[:, :, 0], zero], axis=-1)
    bot = jnp.concatenate([zero, st[:, :, 1]], axis=-1)
    return jnp.concatenate([top, bot], axis=-2)


def _pair_unblockdiag(p):
    B = p.shape[0]
    st = jnp.stack([p[:, :, :N_D, :N_D], p[:, :, N_D:, N_D:]], axis=2)
    return jnp.swapaxes(st.reshape(B, H_D, N_D, N_D), -1, -2)


def _pad_rows(t):
    return jnp.pad(t[:, None, :], ((0, 0), (0, PAD_L - 1), (0, 0)))


def _trunk(x, mod, st, prm, *, decode):
    G, Lg, _ = x.shape
    tm = min(128 if decode else 512, Lg)
    assert decode or Lg % 512 == 0
    nseq = Lg if decode else G
    new = {k: [] for k in ("a_conv", "a_h", "b_s", "c_c", "c_n", "c_m", "d_shift", "d_s")}
    for l in range(DEPTH):
        j = l // 2
        x = _ffn(x, mod, prm["wg"], prm["wu"], prm["wd"], l, 0, tm)
        if l % 2 == 0:
            z = _inproj(x, mod, prm["w_in_even"], l, j, tm)
            a_prm = prm["rglru"][j]
            conv0, h0, s0 = st["a_conv"][j], st["a_h"][j], st["b_s"][j]
            if decode:
                z2 = z[0]
                y1, h1 = _rglru_step(z2, conv0, h0, a_prm)
                conv1 = jnp.concatenate([conv0[:, 1:], z2[:, None, :W_A]], axis=1)
                yb, s1 = _hgrn2(_pad_rows(z2), prm["lb"][j], prm["b_norm_w"][j], s0, C=PAD_L, valid=1, nb=8, tt=PAD_L)
                y1, y2 = y1[None], yb[None, :, 0]
            else:
                buf8 = jnp.concatenate([jnp.zeros((nseq, 8 - (CONV_W - 1), W_A), F32), conv0], axis=1)
                a, b = _rglru_seq(z, buf8, a_prm, 512)
                y1, h1 = _rglru_scan(a, b, z, h0, 512)
                conv1 = z[:, Lg - (CONV_W - 1):, :W_A]
                y2, s1 = _hgrn2(z, prm["lb"][j], prm["b_norm_w"][j], s0, C=CHUNK, valid=CHUNK, nb=1, tt=512)
            new["a_conv"].append(conv1)
            new["a_h"].append(h1)
            new["b_s"].append(s1)
            x = _outproj(x, mod, y1, y2, prm["w_out_even"], l, j, tm)
        else:
            z = _inproj(x, mod, prm["w_in_odd"], l, j, tm)
            c0, n0, m0 = st["c_c"][j], st["c_n"][j], st["c_m"][j]
            n0 = n0[:, :, None, :]
            m0 = jnp.broadcast_to(m0[:, :, None, None], m0.shape + (1, LANES))
            d_prm = prm["rwkv"][j]
            p0 = _pair_blockdiag(st["d_s"][j])
            if decode:
                z2 = z[0]
                zp = _pad_rows(z2)
                yc, c1, n1, m1 = _mlstm(zp, prm["c_gate_b"][j], prm["c_norm_w"][j], c0, n0, m0,
                                        C=PAD_L, valid=1, nb=8, tt=PAD_L)
                feats = _rwkv_pre_step(z2, st["d_shift"][j], d_prm)
                feats = [_pad_rows(t) for t in feats]
                yd, p1 = _rwkv(feats, prm["d_ln_w"][j], prm["d_ln_b"][j], p0, C=PAD_L, valid=1, nb=8, tt=PAD_L)
                y1, y2 = yc[None, :, 0], yd[None, :, 0]
                shift1 = z2[:, :P_D]
            else:
                y1, c1, n1, m1 = _mlstm(z, prm["c_gate_b"][j], prm["c_norm_w"][j], c0, n0, m0,
                                        C=CHUNK, valid=CHUNK, nb=1, tt=256)
                prev8 = jnp.concatenate([jnp.zeros((nseq, 7, P_D), F32), st["d_shift"][j][:, None]], axis=1)
                feats = _rwkv_pre_seq(z, prev8, d_prm, 128)
                y2, p1 = _rwkv(feats, prm["d_ln_w"][j], prm["d_ln_b"][j], p0, C=CHUNK, valid=CHUNK, nb=1, tt=256)
                shift1 = z[:, -1, :P_D]
            new["c_c"].append(c1)
            new["c_n"].append(n1[:, :, 0, :])
            new["c_m"].append(m1[:, :, 0, 0])
            new["d_shift"].append(shift1)
            new["d_s"].append(_pair_unblockdiag(p1))
            x = _outproj(x, mod, y1, y2, prm["w_out_odd"], l, j, tm)
        x = _ffn(x, mod, prm["wg"], prm["wu"], prm["wd"], l, 1, tm)
    y = _final_norm(x, prm["final_norm_w"], tm)
    return (y,) + tuple(jnp.stack(new[k]) for k in ("a_conv", "a_h", "b_s", "c_c", "c_n", "c_m", "d_shift", "d_s"))


def _prepare(w_mod, b_mod, ffn_w_gate, ffn_w_up, ffn_w_down, w_in_even, w_out_even, a_conv_w, a_conv_b, a_gate_r_w,
             a_gate_r_b, a_gate_i_w, a_gate_i_b, a_lambda, b_lb_gamma, b_norm_w, w_in_odd, w_out_odd, c_igate_b,
             c_fgate_b, c_norm_w, d_mu, d_w0, d_w2, d_a0, d_a2, d_g2, d_k_k, d_k_a, d_r_k, d_ln_w, d_ln_b,
             final_norm_w):
    n_gate = 2 * H_C
    odd = jnp.concatenate(
        [w_in_odd[:, :, P_ODD - P_D:], w_in_odd[:, :, :P_ODD - P_D - n_gate],
         w_in_odd[:, :, P_ODD - P_D - n_gate:P_ODD - P_D],
         jnp.zeros((N_ODD, D, P_ODD_PAD - P_ODD), F32)], axis=-1)
    row = lambda t: t[:, None, :]
    zpad = jnp.zeros((N_ODD, LANES - n_gate), F32)
    half = jnp.zeros((N_ODD, R_W, W_D), F32)
    rglru = [(a_conv_w[j], a_conv_b[j][None], a_gate_r_w[j].astype(BF16), a_gate_r_b[j][None],
              a_gate_i_w[j].astype(BF16), a_gate_i_b[j][None], a_lambda[j][None]) for j in range(N_EVEN)]
    w2p = jnp.concatenate([d_w2, half], axis=1).astype(BF16)
    a2p = jnp.concatenate([half, d_a2], axis=1).astype(BF16)
    g2 = d_g2.astype(BF16)
    rwkv = [(d_mu[j][None], d_w0[j][None], d_a0[j][None], d_k_k[j][None], d_k_a[j][None],
             d_r_k[j].reshape(1, W_D), w2p[j], a2p[j], g2[j]) for j in range(N_ODD)]
    return dict(
        wg=ffn_w_gate.astype(BF16), wu=ffn_w_up.astype(BF16), wd=ffn_w_down.astype(BF16),
        w_in_even=w_in_even.astype(BF16), w_in_odd=odd.astype(BF16),
        w_out_even=w_out_even.astype(BF16).reshape(N_EVEN, 2, D // 2, D),
        w_out_odd=w_out_odd.astype(BF16).reshape(N_ODD, 2, D // 2, D),
        rglru=rglru, rwkv=rwkv,
        lb=row(_lower_bounds(b_lb_gamma)), b_norm_w=row(b_norm_w),
        c_gate_b=row(jnp.concatenate([c_igate_b, c_fgate_b, zpad], axis=-1)), c_norm_w=row(c_norm_w),
        d_ln_w=row(d_ln_w), d_ln_b=row(d_ln_b), final_norm_w=final_norm_w)


def kernel(x_prompt, x_sample, c_prompt, c_sample, state_a_conv, state_a_h, state_b_s, state_c_c, state_c_n, state_c_m, state_d_shift, state_d_s, w_mod, b_mod, ffn_w_gate, ffn_w_up, ffn_w_down, w_in_even, w_out_even, a_conv_w, a_conv_b, a_gate_r_w, a_gate_r_b, a_gate_i_w, a_gate_i_b, a_lambda, b_lb_gamma, b_norm_w, w_in_odd, w_out_odd, c_igate_b, c_fgate_b, c_norm_w, d_mu, d_w0, d_w2, d_a0, d_a2, d_g2, d_k_k, d_k_a, d_r_k, d_ln_w, d_ln_b, final_norm_w):
    prm = _prepare(w_mod, b_mod, ffn_w_gate, ffn_w_up, ffn_w_down, w_in_even, w_out_even, a_conv_w, a_conv_b,
                   a_gate_r_w, a_gate_r_b, a_gate_i_w, a_gate_i_b, a_lambda, b_lb_gamma, b_norm_w, w_in_odd,
                   w_out_odd, c_igate_b, c_fgate_b, c_norm_w, d_mu, d_w0, d_w2, d_a0, d_a2, d_g2, d_k_k, d_k_a,
                   d_r_k, d_ln_w, d_ln_b, final_norm_w)
    bp, lp, _ = x_prompt.shape
    bs = x_sample.shape[0]
    n_rows = -(-(bs + bp) // 8) * 8
    c_all = jnp.concatenate([c_sample, c_prompt, jnp.zeros((n_rows - bs - bp, D), F32)], axis=0)
    mod_all = _modulation(c_all, w_mod, b_mod)
    mod_s = mod_all[:, None, :bs]
    mod_p = mod_all[:, bs:bs + bp, None]

    zeros = lambda *s: jnp.zeros(s, F32)
    st_p = dict(a_conv=zeros(N_EVEN, bp, CONV_W - 1, W_A), a_h=zeros(N_EVEN, bp, W_A),
                b_s=zeros(N_EVEN, bp, H_B, DK_B, DV_B), c_c=zeros(N_ODD, bp, H_C, DK_C, DV_C),
                c_n=zeros(N_ODD, bp, H_C, DK_C), c_m=zeros(N_ODD, bp, H_C), d_shift=zeros(N_ODD, bp, P_D),
                d_s=zeros(N_ODD, bp, H_D, N_D, N_D))
    st_s = dict(a_conv=state_a_conv, a_h=state_a_h, b_s=state_b_s, c_c=state_c_c, c_n=state_c_n, c_m=state_c_m,
                d_shift=state_d_shift, d_s=state_d_s)
    out_p = _trunk(x_prompt, mod_p, st_p, prm, decode=False)
    out_s = _trunk(x_sample.reshape(1, bs, D), mod_s, st_s, prm, decode=True)
    y_s = out_s[0].reshape(bs, 1, D)
    return (out_p[0], y_s) + out_p[1:] + out_s[1:]
```

```python
import functools

import jax
import jax.numpy as jnp
from jax import lax
from jax.experimental import pallas as pl
from jax.experimental.pallas import tpu as pltpu

F32 = jnp.float32
BF16 = jnp.bfloat16

D = 2048
DEPTH = 4
N_EVEN = 2
N_ODD = 2
D_FF = 5632
N_MOD = 9
EPS = 1e-6
CHUNK = 64
NEG_BIG = -1e30

W_A = 1024
NB_A = 8
BS_A = 128
CONV_W = 4
C_RGLRU = 8.0
H_B = 8
DK_B = 128
DV_B = 128
H_C = 4
DK_C = 128
DV_C = 256
N_D = 64
H_D = 16
W_D = 1024
R_W = 64
R_A = 64
R_G = 128
GN_EPS_D = 64e-5
P_EVEN = 6144
P_D = 3328
P_ODD = 6408
P_ODD_PAD = 6656
PAD_L = 8
LANES = 128
VMEM_LIMIT = 48 * 2**20


def _cp(sem, vmem=VMEM_LIMIT):
    return pltpu.CompilerParams(dimension_semantics=sem, vmem_limit_bytes=vmem)


def _mm(a, b):
    return jnp.dot(a, b, preferred_element_type=F32)


def _dot(a, b):
    return _mm(a.astype(BF16), b.astype(BF16))


_NT = (((1,), (1,)), ((), ()))
_TN = (((0,), (0,)), ((), ()))
_NN = (((1,), (0,)), ((), ()))


def _dg(a, b, dn):
    return lax.dot_general(a, b, dn, preferred_element_type=F32)


def _dot_nt(a, b):
    return _dg(a.astype(BF16), b.astype(BF16), _NT)


def _dot_tn(a, b):
    return _dg(a.astype(BF16), b.astype(BF16), _TN)


def _split3(x):
    hi = x.astype(BF16)
    r1 = x - hi.astype(F32)
    mid = r1.astype(BF16)
    lo = (r1 - mid.astype(F32)).astype(BF16)
    return hi, mid, lo


def _split2(x):
    hi = x.astype(BF16)
    lo = (x - hi.astype(F32)).astype(BF16)
    return hi, lo


def _sel_dot(mask_bf, x):
    hi, mid, lo = _split3(x)
    return _mm(mask_bf, hi) + _mm(mask_bf, mid) + _mm(mask_bf, lo)


def _dot_sel(x, mask_bf):
    hi, mid, lo = _split3(x)
    return _mm(hi, mask_bf) + _mm(mid, mask_bf) + _mm(lo, mask_bf)


def _dot_sel_tn(x, mask_bf):
    hi, mid, lo = _split3(x)
    return _dg(hi, mask_bf, _TN) + _dg(mid, mask_bf, _TN) + _dg(lo, mask_bf, _TN)


def _mmx(a, b, dn=_NN):
    a_hi, a_lo = _split2(a)
    b_hi, b_lo = _split2(b)
    return _dg(a_hi, b_hi, dn) + _dg(a_hi, b_lo, dn) + _dg(a_lo, b_hi, dn)


def _iota(shape, dim):
    return lax.broadcasted_iota(jnp.int32, shape, dim)


def _sigmoid(x):
    return jax.nn.sigmoid(x)


def _softplus(x):
    return jnp.maximum(x, 0.0) + jnp.log1p(jnp.exp(-jnp.abs(x)))


def _log_sigmoid(x):
    return jnp.minimum(x, 0.0) - jnp.log1p(jnp.exp(-jnp.abs(x)))


def _gelu_tanh(x):
    return x * (0.5 * (1.0 + jnp.tanh(0.7978845608028654 * (x + 0.044715 * (x * x * x)))))


def _rms(x):
    return x * lax.rsqrt(jnp.mean(x * x, axis=-1, keepdims=True) + EPS)


def _adaln(x, shift, scale):
    return _rms(x) * (1.0 + scale) + shift


def _log2(n):
    k = n.bit_length() - 1
    assert (1 << k) == n
    return k


def _mod_spec(rows, width, layer, col):
    return pl.BlockSpec((None, None, rows, width), lambda g, i, n: (layer, g, 0, col))


def _mod_body(c_ref, w_ref, b_ref, o_ref):
    c = c_ref[...]
    cs = (c * _sigmoid(c)).astype(BF16)
    o_ref[...] = _mm(cs, w_ref[...].astype(BF16)) + b_ref[...]


def _modulation(c_all, w_mod, b_mod):
    rows = c_all.shape[0]
    tn = 1024
    return pl.pallas_call(
        _mod_body,
        grid=(DEPTH, N_MOD * D // tn),
        in_specs=[pl.BlockSpec((rows, D), lambda l, n: (0, 0)),
                  pl.BlockSpec((None, D, tn), lambda l, n: (l, 0, n)),
                  pl.BlockSpec((None, 1, tn), lambda l, n: (l, 0, n))],
        out_specs=pl.BlockSpec((None, rows, tn), lambda l, n: (l, 0, n)),
        out_shape=jax.ShapeDtypeStruct((DEPTH, rows, N_MOD * D), F32),
        compiler_params=_cp(("arbitrary", "arbitrary")),
        name="modulation",
    )(c_all, w_mod, b_mod.reshape(DEPTH, 1, N_MOD * D))


def _lower_bounds_body(g_ref, o_ref):
    g = g_ref[...]
    e = jnp.exp(g - jnp.max(g, axis=0, keepdims=True))
    sm = e / jnp.sum(e, axis=0, keepdims=True)
    acc = jnp.zeros_like(sm[0:1])
    for j in range(N_EVEN):
        acc = acc + sm[j:j + 1]
        o_ref[j:j + 1, :] = acc - sm[0:1]


def _lower_bounds(gamma):
    return pl.pallas_call(
        _lower_bounds_body,
        out_shape=jax.ShapeDtypeStruct(gamma.shape, F32),
        name="hgrn2_lower_bounds",
    )(gamma)


def _ffn_body(x_ref, sh_ref, sc_ref, gt_ref, wg_ref, wu_ref, wd_ref, o_ref, hf_ref, acc_ref):
    f = pl.program_id(2)

    @pl.when(f == 0)
    def _():
        hf_ref[...] = _adaln(x_ref[...], sh_ref[...], sc_ref[...]).astype(BF16)
        acc_ref[...] = jnp.zeros_like(acc_ref)

    hf = hf_ref[...]
    g = _mm(hf, wg_ref[...])
    u = _mm(hf, wu_ref[...])
    act = (g * _sigmoid(g) * u).astype(BF16)
    acc_ref[...] += _mm(act, wd_ref[...])

    @pl.when(f == pl.num_programs(2) - 1)
    def _():
        o_ref[...] = x_ref[...] + 0.5 * (1.0 + gt_ref[...]) * acc_ref[...]


def _ffn(x, mod, wg, wu, wd, layer, which, tm):
    G, Lg, _ = x.shape
    R = mod.shape[2]
    tf = 512
    kb = 6 * which
    return pl.pallas_call(
        _ffn_body,
        grid=(G, Lg // tm, D_FF // tf),
        in_specs=[pl.BlockSpec((None, tm, D), lambda g, i, f: (g, i, 0)),
                  _mod_spec(R, D, layer, kb), _mod_spec(R, D, layer, kb + 1), _mod_spec(R, D, layer, kb + 2),
                  pl.BlockSpec((None, None, D, tf), lambda g, i, f: (layer, which, 0, f)),
                  pl.BlockSpec((None, None, D, tf), lambda g, i, f: (layer, which, 0, f)),
                  pl.BlockSpec((None, None, tf, D), lambda g, i, f: (layer, which, f, 0))],
        out_specs=pl.BlockSpec((None, tm, D), lambda g, i, f: (g, i, 0)),
        out_shape=jax.ShapeDtypeStruct(x.shape, F32),
        scratch_shapes=[pltpu.VMEM((tm, D), BF16), pltpu.VMEM((tm, D), F32)],
        compiler_params=_cp(("arbitrary", "arbitrary", "arbitrary")),
        name="adaln_swiglu_ffn",
    )(x, mod, mod, mod, wg, wu, wd)


def _inproj_body(x_ref, sh_ref, sc_ref, w_ref, o_ref, hf_ref):
    @pl.when(pl.program_id(2) == 0)
    def _():
        hf_ref[...] = _adaln(x_ref[...], sh_ref[...], sc_ref[...]).astype(BF16)

    o_ref[...] = _mm(hf_ref[...], w_ref[...])


def _inproj(x, mod, w, layer, j, tm):
    G, Lg, _ = x.shape
    R = mod.shape[2]
    P = w.shape[-1]
    tn = 512
    return pl.pallas_call(
        _inproj_body,
        grid=(G, Lg // tm, P // tn),
        in_specs=[pl.BlockSpec((None, tm, D), lambda g, i, n: (g, i, 0)),
                  _mod_spec(R, D, layer, 3), _mod_spec(R, D, layer, 4),
                  pl.BlockSpec((None, D, tn), lambda g, i, n: (j, 0, n))],
        out_specs=pl.BlockSpec((None, tm, tn), lambda g, i, n: (g, i, n)),
        out_shape=jax.ShapeDtypeStruct((G, Lg, P), F32),
        scratch_shapes=[pltpu.VMEM((tm, D), BF16)],
        compiler_params=_cp(("arbitrary", "arbitrary", "arbitrary")),
        name="adaln_in_projection",
    )(x, mod, mod, w)


def _outproj_body(x_ref, gt_ref, y1_ref, y2_ref, w1_ref, w2_ref, o_ref):
    y = _dot(y1_ref[...], w1_ref[...]) + _dot(y2_ref[...], w2_ref[...])
    o_ref[...] = x_ref[...] + (1.0 + gt_ref[...]) * y


def _outproj(x, mod, y1, y2, w, layer, j, tm):
    G, Lg, _ = x.shape
    R = mod.shape[2]
    tn = 1024
    nb = D // tn
    half = y1.shape[-1]
    return pl.pallas_call(
        _outproj_body,
        grid=(G, Lg // tm, nb),
        in_specs=[pl.BlockSpec((None, tm, tn), lambda g, i, n: (g, i, n)),
                  pl.BlockSpec((None, None, R, tn), lambda g, i, n: (layer, g, 0, 5 * nb + n)),
                  pl.BlockSpec((None, tm, half), lambda g, i, n: (g, i, 0)),
                  pl.BlockSpec((None, tm, half), lambda g, i, n: (g, i, 0)),
                  pl.BlockSpec((None, None, half, tn), lambda g, i, n: (j, 0, 0, n)),
                  pl.BlockSpec((None, None, half, tn), lambda g, i, n: (j, 1, 0, n))],
        out_specs=pl.BlockSpec((None, tm, tn), lambda g, i, n: (g, i, n)),
        out_shape=jax.ShapeDtypeStruct(x.shape, F32),
        compiler_params=_cp(("arbitrary", "arbitrary", "arbitrary")),
        name="out_projection_residual",
    )(x, mod, y1, y2, w, w)


def _final_body(x_ref, w_ref, o_ref):
    o_ref[...] = _rms(x_ref[...]) * w_ref[...]


def _final_norm(x, w, tm):
    G, Lg, _ = x.shape
    return pl.pallas_call(
        _final_body,
        grid=(G, Lg // tm),
        in_specs=[pl.BlockSpec((None, tm, D), lambda g, i: (g, i, 0)),
                  pl.BlockSpec((1, D), lambda g, i: (0, 0))],
        out_specs=pl.BlockSpec((None, tm, D), lambda g, i: (g, i, 0)),
        out_shape=jax.ShapeDtypeStruct(x.shape, F32),
        compiler_params=_cp(("arbitrary", "arbitrary")),
        name="final_rmsnorm",
    )(x, w.reshape(1, D))


def _rglru_gates(x0, x1, x2, x3, cw, cb, wr, br, wi, bi, lam):
    u = cb + x0 * cw[0:1] + x1 * cw[1:2] + x2 * cw[2:3] + x3 * cw[3:4]
    r_parts, i_parts = [], []
    for n in range(NB_A):
        un = u[:, n * BS_A:(n + 1) * BS_A].astype(BF16)
        r_parts.append(_mm(un, wr[n]))
        i_parts.append(_mm(un, wi[n]))
    r = _sigmoid(jnp.concatenate(r_parts, axis=1) + br)
    ig = _sigmoid(jnp.concatenate(i_parts, axis=1) + bi)
    log_a = -C_RGLRU * r * _softplus(-lam)
    a = jnp.exp(log_a)
    b = jnp.sqrt(-jnp.tanh(log_a) * (a * a + 1.0)) * (ig * u)
    return a, b


def _rglru_seq_body(x_ref, halo_ref, buf_ref, cw_ref, cb_ref, wr_ref, br_ref, wi_ref, bi_ref, lam_ref,
                    a_ref, b_ref, *, tt):
    i = pl.program_id(1)
    halo = jnp.where(i == 0, buf_ref[...], halo_ref[...])
    full = jnp.concatenate([halo, x_ref[...]], axis=0)
    taps = [pltpu.roll(full, CONV_W - 1 - k, axis=0)[8:8 + tt] for k in range(CONV_W - 1)]
    a, b = _rglru_gates(taps[0], taps[1], taps[2], x_ref[...], cw_ref[...], cb_ref[...], wr_ref[...], br_ref[...],
                        wi_ref[...], bi_ref[...], lam_ref[...])
    a_ref[...] = a
    b_ref[...] = b


def _rglru_param_specs(nidx):
    z2 = lambda *_: (0, 0)
    z3 = lambda *_: (0, 0, 0)
    del nidx
    return [pl.BlockSpec((CONV_W, W_A), z2), pl.BlockSpec((1, W_A), z2),
            pl.BlockSpec((NB_A, BS_A, BS_A), z3), pl.BlockSpec((1, W_A), z2),
            pl.BlockSpec((NB_A, BS_A, BS_A), z3), pl.BlockSpec((1, W_A), z2),
            pl.BlockSpec((1, W_A), z2)]


def _rglru_seq(z, buf8, prm, tt):
    B, L, _ = z.shape
    hb = tt // 8
    out = jax.ShapeDtypeStruct((B, L, W_A), F32)
    return pl.pallas_call(
        functools.partial(_rglru_seq_body, tt=tt),
        grid=(B, L // tt),
        in_specs=[pl.BlockSpec((None, tt, W_A), lambda b, i: (b, i, 0)),
                  pl.BlockSpec((None, 8, W_A), lambda b, i: (b, jnp.maximum(i * hb - 1, 0), 0)),
                  pl.BlockSpec((None, 8, W_A), lambda b, i: (b, 0, 0))] + _rglru_param_specs(2),
        out_specs=[pl.BlockSpec((None, tt, W_A), lambda b, i: (b, i, 0))] * 2,
        out_shape=[out, out],
        compiler_params=_cp(("arbitrary", "arbitrary")),
        name="rglru_conv_gates",
    )(z, z, buf8, *prm)


def _rglru_scan_body(a_ref, b_ref, ag_ref, h0_ref, y_ref, hl_ref, h_scr, *, tt):
    @pl.when(pl.program_id(1) == 0)
    def _():
        h_scr[...] = h0_ref[...]

    def step(t, h):
        h = a_ref[t] * h + b_ref[t]
        y_ref[t] = h
        return h

    h = lax.fori_loop(0, tt, step, h_scr[...], unroll=8)
    h_scr[...] = h
    hl_ref[...] = h
    y_ref[...] = y_ref[...] * _gelu_tanh(ag_ref[...])


def _rglru_scan(a, b, z, h0, tt):
    B, L, _ = a.shape
    a4 = a.reshape(B, L, 8, LANES)
    b4 = b.reshape(B, L, 8, LANES)
    z4 = z.reshape(B, L, z.shape[-1] // LANES, LANES)
    spec = pl.BlockSpec((None, tt, 8, LANES), lambda bb, i: (bb, i, 0, 0))
    y, hl = pl.pallas_call(
        functools.partial(_rglru_scan_body, tt=tt),
        grid=(B, L // tt),
        in_specs=[spec, spec,
                  pl.BlockSpec((None, tt, 8, LANES), lambda bb, i: (bb, i, 1, 0)),
                  pl.BlockSpec((None, 8, LANES), lambda bb, i: (bb, 0, 0))],
        out_specs=[spec, pl.BlockSpec((None, 8, LANES), lambda bb, i: (bb, 0, 0))],
        out_shape=[jax.ShapeDtypeStruct((B, L, 8, LANES), F32), jax.ShapeDtypeStruct((B, 8, LANES), F32)],
        scratch_shapes=[pltpu.VMEM((8, LANES), F32)],
        compiler_params=_cp(("arbitrary", "arbitrary")),
        name="rglru_scan_gelu_gate",
    )(a4, b4, z4, h0.reshape(B, 8, LANES))
    return y.reshape(B, L, W_A), hl.reshape(B, W_A)


def _rglru_step_body(x3_ref, ag_ref, x0_ref, x1_ref, x2_ref, h0_ref, cw_ref, cb_ref, wr_ref, br_ref, wi_ref, bi_ref,
                     lam_ref, y_ref, h_ref):
    a, b = _rglru_gates(x0_ref[...], x1_ref[...], x2_ref[...], x3_ref[...], cw_ref[...], cb_ref[...], wr_ref[...],
                        br_ref[...], wi_ref[...], bi_ref[...], lam_ref[...])
    h = a * h0_ref[...] + b
    h_ref[...] = h
    y_ref[...] = h * _gelu_tanh(ag_ref[...])


def _rglru_step(z, buf, h0, prm):
    n = z.shape[0]
    row = lambda c: pl.BlockSpec((n, W_A), lambda i: (0, c))
    out = jax.ShapeDtypeStruct((n, W_A), F32)
    return pl.pallas_call(
        _rglru_step_body,
        grid=(1,),
        in_specs=[row(0), row(1), row(0), row(0), row(0), row(0)] + _rglru_param_specs(1),
        out_specs=[row(0), row(0)],
        out_shape=[out, out],
        compiler_params=_cp(("arbitrary",)),
        name="rglru_decode_step",
    )(z, z, buf[:, 0], buf[:, 1], buf[:, 2], h0, *prm)


def _hgrn2_chunk(q, fp, v, gate, lb, nw, S, *, C, valid):
    SB = min(16, C)
    k = (1.0 - lb) * _sigmoid(-fp)
    lf = jnp.log1p(-k)
    if valid < C:
        live = _iota((C, 1), 0) < valid
        k = jnp.where(live, k, 0.0)
        lf = jnp.where(live, lf, 0.0)
    g_col = _dot_sel_tn(lf, jnp.ones((C, DV_B), BF16))
    if valid == 1:
        kv = _dot_tn(k, v)
        yield
        S_new = jnp.exp(g_col) * S + kv
        o = _dot(q, S_new)
        yield
        return _rms(o) * nw * (gate * _sigmoid(gate)), S_new
    tril = _iota((C, C), 0) >= _iota((C, C), 1)
    b = _sel_dot(jnp.where(tril, 1.0, 0.0).astype(BF16), lf)
    yield
    g_row = b[C - 1:C, :]
    o = _dot(q * jnp.exp(b), S)
    khat = k * jnp.exp(g_row - b)
    S_new = jnp.exp(g_col) * S + _dot_tn(khat, v)

    tril_sb = _iota((SB, SB), 0) >= _iota((SB, SB), 1)
    atts = []
    for i in range(C // SB):
        lo = i * SB
        qi, bi, ki = q[lo:lo + SB], b[lo:lo + SB], k[lo:lo + SB]
        dec = jnp.exp(jnp.minimum(bi[:, None, :] - bi[None, :, :], 0.0))
        att = jnp.where(tril_sb, jnp.sum(qi[:, None, :] * ki[None, :, :] * dec, axis=-1), 0.0)
        off = None
        if i > 0:
            ref = b[lo - 1:lo, :]
            qt = qi * jnp.exp(bi - ref)
            kt = k[0:lo] * jnp.exp(ref - b[0:lo])
            off = _dot_nt(qt, kt)
        atts.append((att, off))
    yield
    parts = []
    for i, (att, off) in enumerate(atts):
        lo = i * SB
        oi = _dot(att, v[lo:lo + SB])
        parts.append(oi if off is None else oi + _dot(off, v[0:lo]))
    yield
    o = o + (parts[0] if len(parts) == 1 else jnp.concatenate(parts, axis=0))
    return _rms(o) * nw * (gate * _sigmoid(gate)), S_new


def _hgrn2_body(q_ref, f_ref, v_ref, g_ref, lb_ref, nw_ref, s0_ref, y_ref, s1_ref, *, C, valid, nb, nchunk):
    @pl.when(pl.program_id(2) == 0)
    def _():
        s1_ref[...] = s0_ref[...]

    lb = lb_ref[...]
    nw = nw_ref[...]

    def chunk(c, carry):
        rows = pl.ds(pl.multiple_of(c * C, C), C)
        outs = _interleave([_hgrn2_chunk(q_ref[bi, rows, :], f_ref[bi, rows, :], v_ref[bi, rows, :],
                                         g_ref[bi, rows, :], lb, nw, s1_ref[bi], C=C, valid=valid)
                            for bi in range(nb)])
        for bi, (y, s_new) in enumerate(outs):
            y_ref[bi, rows, :] = y
            s1_ref[bi] = s_new
        return carry

    lax.fori_loop(0, nchunk, chunk, 0)


def _hgrn2(z, lb, nw, s0, *, C, valid, nb, tt):
    B, L, _ = z.shape
    col = lambda base: pl.BlockSpec((nb, tt, LANES), lambda b, h, i: (b, i, base + h))
    st = pl.BlockSpec((nb, None, DK_B, DV_B), lambda b, h, i: (b, h, 0, 0))
    return pl.pallas_call(
        functools.partial(_hgrn2_body, C=C, valid=valid, nb=nb, nchunk=tt // C),
        grid=(B // nb, H_B, L // tt),
        in_specs=[col(16), col(24), col(32), col(40),
                  pl.BlockSpec((1, LANES), lambda b, h, i: (0, h)),
                  pl.BlockSpec((1, LANES), lambda b, h, i: (0, 0)), st],
        out_specs=[col(0), st],
        out_shape=[jax.ShapeDtypeStruct((B, L, H_B * DV_B), F32), jax.ShapeDtypeStruct(s0.shape, F32)],
        compiler_params=_cp(("arbitrary", "arbitrary", "arbitrary")),
        name="hgrn2_chunked",
    )(z, z, z, z, lb, nw, s0)


def _mlstm_step(q, k, v, op, gt, h, nw, Cst, n, m, *, C):
    lane = _iota((C, LANES), 1)
    live = _iota((C, 1), 0) < 1
    i0 = jnp.sum(jnp.where(live & (lane == h), gt, 0.0), keepdims=True)
    lf0 = _log_sigmoid(jnp.sum(jnp.where(live & (lane == h + H_C), gt, 0.0), keepdims=True))
    inter = lf0 + m
    m_t = jnp.maximum(inter, i0)
    w_inter = jnp.exp(inter - m_t)
    wk = jnp.exp(i0 - m_t)
    kh = k * (DK_C ** -0.5)
    s = jnp.sum(q * kh, axis=1, keepdims=True) * wk
    qc = _dot(q, Cst)
    kw = jnp.where(live, kh * wk, 0.0)
    kv = _dot_tn(kw, v)
    yield
    num = w_inter * qc + s * v
    den = w_inter * jnp.sum(q * n, axis=1, keepdims=True) + s
    hh = num / jnp.maximum(jnp.abs(den), jnp.exp(-m_t))
    C_new = w_inter * Cst + kv
    n_new = w_inter * n + jnp.sum(kw, axis=0, keepdims=True)
    y = _rms(hh) * nw * _sigmoid(op)
    return y, C_new, n_new, m_t


def _mlstm_chunk(q, k, v, op, gt, h, nw, Cst, n, m, *, C, valid):
    if valid == 1:
        return (yield from _mlstm_step(q, k, v, op, gt, h, nw, Cst, n, m, C=C))
    gtT = gt.T
    lane = _iota((C, LANES), 1)
    sub = _iota((LANES, C), 0)
    i_col = jnp.sum(jnp.where(lane == h, gt, 0.0), axis=1, keepdims=True)
    f_col = jnp.sum(jnp.where(lane == h + H_C, gt, 0.0), axis=1, keepdims=True)
    i_row = jnp.sum(jnp.where(sub == h, gtT, 0.0), axis=0, keepdims=True)
    f_row = jnp.sum(jnp.where(sub == h + H_C, gtT, 0.0), axis=0, keepdims=True)
    lf_col = _log_sigmoid(f_col)
    lf_row = _log_sigmoid(f_row)
    if valid < C:
        live_c = _iota((C, 1), 0) < valid
        live_r = _iota((1, C), 1) < valid
        i_col = jnp.where(live_c, i_col, NEG_BIG)
        i_row = jnp.where(live_r, i_row, NEG_BIG)
        lf_col = jnp.where(live_c, lf_col, 0.0)
        lf_row = jnp.where(live_r, lf_row, 0.0)
    tril = _iota((C, C), 0) >= _iota((C, C), 1)
    b_col = jnp.sum(jnp.where(tril, lf_row, 0.0), axis=1, keepdims=True)
    b_row = jnp.sum(jnp.where(_iota((C, C), 0) <= _iota((C, C), 1), lf_col, 0.0), axis=0, keepdims=True)
    dm = jnp.where(tril, b_col - b_row + i_row, NEG_BIG)
    inter = b_col + m
    m_t = jnp.maximum(inter, jnp.max(dm, axis=1, keepdims=True))
    w_inter = jnp.exp(inter - m_t)
    kh = k * (DK_C ** -0.5)
    qk = _dot_nt(q, kh)
    qc = _dot(q, Cst)
    g = b_col[C - 1:C, :]
    m_new = m_t[C - 1:C, :]
    wk = jnp.exp(g - b_col + i_col - m_new)
    f_state = jnp.exp(g + m - m_new)
    kw = kh * wk
    kv = _dot_tn(kw, v)
    yield
    s = qk * jnp.exp(dm - m_t)
    sv = _dot(s, v)
    yield
    num = w_inter * qc + sv
    den = w_inter * jnp.sum(q * n, axis=1, keepdims=True) + jnp.sum(s, axis=1, keepdims=True)
    hh = num / jnp.maximum(jnp.abs(den), jnp.exp(-m_t))
    C_new = f_state * Cst + kv
    n_new = f_state * n + jnp.sum(kw, axis=0, keepdims=True)
    y = _rms(hh) * nw * _sigmoid(op)
    return y, C_new, n_new, m_new


def _mlstm_body(q_ref, k_ref, v_ref, o_ref, g_ref, gb_ref, nw_ref, c0_ref, n0_ref, m0_ref,
                y_ref, c1_ref, n1_ref, m1_ref, *, C, valid, nb, nchunk):
    @pl.when(pl.program_id(2) == 0)
    def _():
        c1_ref[...] = c0_ref[...]
        n1_ref[...] = n0_ref[...]
        m1_ref[...] = m0_ref[...]

    h = pl.program_id(1)
    gb = gb_ref[...]
    nw = nw_ref[...]

    def chunk(c, carry):
        rows = pl.ds(pl.multiple_of(c * C, C), C)
        outs = _interleave([_mlstm_chunk(
            q_ref[bi, rows, :], k_ref[bi, rows, :], v_ref[bi, rows, :], o_ref[bi, rows, :],
            g_ref[bi, rows, :] + gb, h, nw, c1_ref[bi], n1_ref[bi], m1_ref[bi][:, 0:1], C=C, valid=valid)
            for bi in range(nb)])
        for bi, (y, c_new, n_new, m_new) in enumerate(outs):
            y_ref[bi, rows, :] = y
            c1_ref[bi] = c_new
            n1_ref[bi] = n_new
            m1_ref[bi] = jnp.broadcast_to(m_new, (1, LANES))
        return carry

    lax.fori_loop(0, nchunk, chunk, 0)


def _mlstm(z, gb, nw, c0, n0, m0, *, C, valid, nb, tt):
    B, L, _ = z.shape
    c128 = lambda base: pl.BlockSpec((nb, tt, LANES), lambda b, h, i: (b, i, base + h))
    c256 = lambda base: pl.BlockSpec((nb, tt, DV_C), lambda b, h, i: (b, i, base + h))
    cst = pl.BlockSpec((nb, None, DK_C, DV_C), lambda b, h, i: (b, h, 0, 0))
    vec = pl.BlockSpec((nb, None, 1, LANES), lambda b, h, i: (b, h, 0, 0))
    return pl.pallas_call(
        functools.partial(_mlstm_body, C=C, valid=valid, nb=nb, nchunk=tt // C),
        grid=(B // nb, H_C, L // tt),
        in_specs=[c128(26), c128(30), c256(17), c256(21),
                  pl.BlockSpec((nb, tt, LANES), lambda b, h, i: (b, i, 50)),
                  pl.BlockSpec((1, LANES), lambda b, h, i: (0, 0)),
                  pl.BlockSpec((1, DV_C), lambda b, h, i: (0, 0)), cst, vec, vec],
        out_specs=[c256(0), cst, vec, vec],
        out_shape=[jax.ShapeDtypeStruct((B, L, H_C * DV_C), F32), jax.ShapeDtypeStruct(c0.shape, F32),
                   jax.ShapeDtypeStruct(n0.shape, F32), jax.ShapeDtypeStruct(m0.shape, F32)],
        compiler_params=_cp(("arbitrary", "arbitrary", "arbitrary")),
        name="mlstm_chunked",
    )(z, z, z, z, z, gb, nw, c0, n0, m0)


def _head_sum(x, bd):
    return jnp.concatenate([_dot_sel(x[:, n * LANES:(n + 1) * LANES], bd) for n in range(W_D // LANES)], axis=1)


def _rwkv_pre(zd, sh, mu, w0, a0, kkw, ka, rk, w2p, a2p, g2, bd):
    zs = zd + (sh - zd) * mu
    r, k, v = zs[:, 0:W_D], zs[:, W_D:2 * W_D], zs[:, 2 * W_D:3 * W_D]
    wa = zs[:, 3 * W_D:3 * W_D + LANES]
    gl = zs[:, 3 * W_D + LANES:]
    w = -_softplus(-(w0 + _dot(jnp.tanh(wa), w2p))) - 0.5
    logd = -jnp.exp(w)
    a = _sigmoid(a0 + _dot(wa, a2p))
    gate = _dot(_sigmoid(gl), g2)
    kk = k * kkw
    kk = kk / jnp.maximum(jnp.sqrt(_head_sum(kk * kk, bd)), 1e-12)
    kmod = k * (1.0 + (a - 1.0) * ka)
    bonus = _head_sum(r * kmod * rk, bd) * v
    return r, logd, kmod, v, kk, kk * a, gate, bonus


def _head_block_ones():
    return jnp.where((_iota((LANES, LANES), 0) >> 6) == (_iota((LANES, LANES), 1) >> 6), 1.0, 0.0).astype(BF16)


def _rwkv_pre_seq_body(z_ref, halo_ref, prev_ref, mu_ref, w0_ref, a0_ref, kk_ref, ka_ref, rk_ref, w2_ref, a2_ref,
                       g2_ref, *out_refs, tt):
    i = pl.program_id(1)
    halo = jnp.where(i == 0, prev_ref[...], halo_ref[...])
    zd = z_ref[...]
    sh = pltpu.roll(jnp.concatenate([halo, zd], axis=0), 1, axis=0)[8:8 + tt]
    outs = _rwkv_pre(zd, sh, mu_ref[...], w0_ref[...], a0_ref[...], kk_ref[...], ka_ref[...], rk_ref[...],
                     w2_ref[...], a2_ref[...], g2_ref[...], _head_block_ones())
    for ref, val in zip(out_refs, outs):
        ref[...] = val


def _rwkv_pre_step_body(z_ref, sh_ref, mu_ref, w0_ref, a0_ref, kk_ref, ka_ref, rk_ref, w2_ref, a2_ref, g2_ref,
                        *out_refs):
    outs = _rwkv_pre(z_ref[...], sh_ref[...], mu_ref[...], w0_ref[...], a0_ref[...], kk_ref[...], ka_ref[...],
                     rk_ref[...], w2_ref[...], a2_ref[...], g2_ref[...], _head_block_ones())
    for ref, val in zip(out_refs, outs):
        ref[...] = val


def _rwkv_pre_param_specs():
    z2 = lambda *_: (0, 0)
    return ([pl.BlockSpec((1, P_D), z2)] + [pl.BlockSpec((1, W_D), z2)] * 5
            + [pl.BlockSpec((LANES, W_D), z2)] * 3)


def _rwkv_pre_seq(z, prev8, prm, tt):
    B, L, _ = z.shape
    hb = tt // 8
    out = jax.ShapeDtypeStruct((B, L, W_D), F32)
    return pl.pallas_call(
        functools.partial(_rwkv_pre_seq_body, tt=tt),
        grid=(B, L // tt),
        in_specs=[pl.BlockSpec((None, tt, P_D), lambda b, i: (b, i, 0)),
                  pl.BlockSpec((None, 8, P_D), lambda b, i: (b, jnp.maximum(i * hb - 1, 0), 0)),
                  pl.BlockSpec((None, 8, P_D), lambda b, i: (b, 0, 0))] + _rwkv_pre_param_specs(),
        out_specs=[pl.BlockSpec((None, tt, W_D), lambda b, i: (b, i, 0))] * 8,
        out_shape=[out] * 8,
        compiler_params=_cp(("arbitrary", "arbitrary")),
        name="rwkv7_token_shift_features",
    )(z, z, prev8, *prm)


def _rwkv_pre_step(z, shifted, prm):
    n = z.shape[0]
    out = jax.ShapeDtypeStruct((n, W_D), F32)
    return pl.pallas_call(
        _rwkv_pre_step_body,
        grid=(1,),
        in_specs=[pl.BlockSpec((n, P_D), lambda i: (0, 0)), pl.BlockSpec((n, P_D), lambda i: (0, 0))]
        + _rwkv_pre_param_specs(),
        out_specs=[pl.BlockSpec((n, W_D), lambda i: (0, 0))] * 8,
        out_shape=[out] * 8,
        compiler_params=_cp(("arbitrary",)),
        name="rwkv7_decode_features",
    )(z, shifted, *prm)


def _unit_lower_inverse(N, C):
    n = N.shape[0]
    ri, ci = _iota((n, n), 0), _iota((n, n), 1)
    base = min(8, C)
    kb = _log2(base)
    X = jnp.where((ri >> kb) == (ci >> kb), N, 0.0)
    T = jnp.where(ri == ci, 1.0, 0.0) + X
    for _ in range(kb - 1):
        X = _dot(X, X)
        yield
        T = T + _dot(T, X)
        yield
    size = base
    while size < C:
        ks = _log2(size)
        sel = ((ri >> (ks + 1)) == (ci >> (ks + 1))) & (((ri >> ks) & 1) == 1) & (((ci >> ks) & 1) == 0)
        TN_ = _dot(T, jnp.where(sel, N, 0.0))
        yield
        T = T + _dot(TN_, T)
        yield
        size *= 2
    return T


def _interleave(gens):
    results = [None] * len(gens)
    live = list(range(len(gens)))
    while live:
        for i in list(live):
            try:
                next(gens[i])
            except StopIteration as stop:
                results[i] = stop.value
                live.remove(i)
    return results


def _rwkv_groupnorm_gate(y, gate, bonus, lnw, lnb):
    bd = _head_block_ones()
    mean = _dot_sel(y, bd) * (1.0 / N_D)
    yield
    dy = y - mean
    var = _dot_sel(dy * dy, bd) * (1.0 / N_D)
    yield
    return (dy * lax.rsqrt(var + GN_EPS_D) * lnw + lnb + bonus) * gate


def _rwkv_step(r, ld, k, v, kk, be, gate, bonus, lnw, lnb, P, *, C):
    live = _iota((C, 1), 0) < 1
    zero = lambda x: jnp.where(live, x, 0.0)
    U = _mmx(zero(-kk), P)
    g_col = _dot_sel_tn(zero(ld), jnp.ones((C, LANES), BF16))
    yield
    upd = _mmx(jnp.concatenate([zero(be), zero(k)], axis=0), jnp.concatenate([U, zero(v)], axis=0), _TN)
    yield
    same_head = (_iota((LANES, LANES), 0) >> 6) == (_iota((LANES, LANES), 1) >> 6)
    P_new = jnp.exp(g_col) * P + jnp.where(same_head, upd, 0.0)
    y = _mmx(r, P_new)
    yield
    out = yield from _rwkv_groupnorm_gate(y, gate, bonus, lnw, lnb)
    return out, P_new


def _rwkv_chunk(r, ld, k, v, kk, be, gate, bonus, lnw, lnb, P, *, C, valid):
    if valid == 1:
        return (yield from _rwkv_step(r, ld, k, v, kk, be, gate, bonus, lnw, lnb, P, C=C))
    if valid < C:
        live = _iota((C, 1), 0) < valid
        ld = jnp.where(live, ld, 0.0)
        be = jnp.where(live, be, 0.0)
        k = jnp.where(live, k, 0.0)
        v = jnp.where(live, v, 0.0)
    tril = _iota((C, C), 0) >= _iota((C, C), 1)
    c = _sel_dot(jnp.where(tril, 1.0, 0.0).astype(BF16), ld)
    g_col = _dot_sel_tn(ld, jnp.ones((C, LANES), BF16))
    yield
    ec, enc = jnp.exp(c), jnp.exp(-c)
    lane0 = _iota((C, LANES), 1) < N_D

    def stack(x):
        return jnp.concatenate([jnp.where(lane0, x, 0.0), jnp.where(lane0, 0.0, x)], axis=0)

    n2 = 2 * C
    AR = jnp.concatenate([stack(-kk * jnp.exp(c - ld)), stack(r * ec)], axis=0).astype(BF16)
    BK = jnp.concatenate([stack(be * enc), stack(k * enc)], axis=0).astype(BF16)
    V2 = stack(v).astype(BF16)
    kc = _log2(C)
    ri, ci = _iota((2 * n2, 2 * n2), 0), _iota((2 * n2, 2 * n2), 1)
    tpos, spos = ri & (C - 1), ci & (C - 1)
    keep = (((ri >> kc) & 1) == ((ci >> kc) & 1)) & ((spos < tpos) | ((ri >= n2) & (spos == tpos)))
    G = jnp.where(keep, _dg(AR, BK, _NT), 0.0)
    LP = _mm(AR, P.astype(BF16))
    yield
    GV = _mm(G[:, n2:].astype(BF16), V2)
    T = yield from _unit_lower_inverse(G[0:n2, 0:n2], C)
    U = _dot(T, LP[0:n2] + GV[0:n2])
    yield
    Y2 = LP[n2:] + GV[n2:] + _dot(G[n2:, 0:n2], U)
    P_new = jnp.exp(g_col) * (P + _dg(BK, jnp.concatenate([U.astype(BF16), V2], axis=0), _TN))
    yield
    y = Y2[0:C] + Y2[C:n2]
    out = yield from _rwkv_groupnorm_gate(y, gate, bonus, lnw, lnb)
    return out, P_new


def _rwkv_body(r_ref, ld_ref, k_ref, v_ref, kk_ref, be_ref, gt_ref, bo_ref, lnw_ref, lnb_ref, p0_ref,
               y_ref, p1_ref, *, C, valid, nb, nchunk):
    @pl.when(pl.program_id(2) == 0)
    def _():
        p1_ref[...] = p0_ref[...]

    lnw = lnw_ref[...]
    lnb = lnb_ref[...]

    def chunk(c, carry):
        rows = pl.ds(pl.multiple_of(c * C, C), C)
        ins = [[ref[bi, rows, :] for ref in (r_ref, ld_ref, k_ref, v_ref, kk_ref, be_ref, gt_ref, bo_ref)]
               + [lnw, lnb, p1_ref[bi]] for bi in range(nb)]
        outs = _interleave([_rwkv_chunk(*args, C=C, valid=valid) for args in ins])
        for bi, (y, p_new) in enumerate(outs):
            y_ref[bi, rows, :] = y
            p1_ref[bi] = p_new
        return carry

    lax.fori_loop(0, nchunk, chunk, 0)


def _rwkv(feats, lnw, lnb, p0, *, C, valid, nb, tt):
    B, L, _ = feats[0].shape
    col = pl.BlockSpec((nb, tt, LANES), lambda b, p, i: (b, i, p))
    vec = pl.BlockSpec((1, LANES), lambda b, p, i: (0, p))
    st = pl.BlockSpec((nb, None, LANES, LANES), lambda b, p, i: (b, p, 0, 0))
    return pl.pallas_call(
        functools.partial(_rwkv_body, C=C, valid=valid, nb=nb, nchunk=tt // C),
        grid=(B // nb, H_D // 2, L // tt),
        in_specs=[col] * 8 + [vec, vec, st],
        out_specs=[col, st],
        out_shape=[jax.ShapeDtypeStruct((B, L, W_D), F32), jax.ShapeDtypeStruct(p0.shape, F32)],
        compiler_params=_cp(("arbitrary", "arbitrary", "arbitrary")),
        name="rwkv7_chunked",
    )(*feats, lnw, lnb, p0)


def _pair_blockdiag(s):
    B = s.shape[0]
    st = jnp.swapaxes(s, -1, -2).reshape(B, H_D // 2, 2, N_D, N_D)
    zero = jnp.zeros_like(st[:, :, 0])
    top = jnp.concatenate([st[:, :, 0], zero], axis=-1)
    bot = jnp.concatenate([zero, st[:, :, 1]], axis=-1)
    return jnp.concatenate([top, bot], axis=-2)


def _pair_unblockdiag(p):
    B = p.shape[0]
    st = jnp.stack([p[:, :, :N_D, :N_D], p[:, :, N_D:, N_D:]], axis=2)
    return jnp.swapaxes(st.reshape(B, H_D, N_D, N_D), -1, -2)


def _pad_rows(t):
    return jnp.pad(t[:, None, :], ((0, 0), (0, PAD_L - 1), (0, 0)))


def _trunk(x, mod, st, prm, *, decode):
    G, Lg, _ = x.shape
    tm = min(128 if decode else 512, Lg)
    tm_in = tm if decode or Lg % 1024 else 1024
    assert decode or Lg % 512 == 0
    nseq = Lg if decode else G
    new = {k: [] for k in ("a_conv", "a_h", "b_s", "c_c", "c_n", "c_m", "d_shift", "d_s")}
    for l in range(DEPTH):
        j = l // 2
        x = _ffn(x, mod, prm["wg"], prm["wu"], prm["wd"], l, 0, tm)
        if l % 2 == 0:
            z = _inproj(x, mod, prm["w_in_even"], l, j, tm_in)
            a_prm = prm["rglru"][j]
            conv0, h0, s0 = st["a_conv"][j], st["a_h"][j], st["b_s"][j]
            if decode:
                z2 = z[0]
                y1, h1 = _rglru_step(z2, conv0, h0, a_prm)
                conv1 = jnp.concatenate([conv0[:, 1:], z2[:, None, :W_A]], axis=1)
                yb, s1 = _hgrn2(_pad_rows(z2), prm["lb"][j], prm["b_norm_w"][j], s0, C=PAD_L, valid=1, nb=8, tt=PAD_L)
                y1, y2 = y1[None], yb[None, :, 0]
            else:
                buf8 = jnp.concatenate([jnp.zeros((nseq, 8 - (CONV_W - 1), W_A), F32), conv0], axis=1)
                a, b = _rglru_seq(z, buf8, a_prm, 512)
                y1, h1 = _rglru_scan(a, b, z, h0, 512)
                conv1 = z[:, Lg - (CONV_W - 1):, :W_A]
                y2, s1 = _hgrn2(z, prm["lb"][j], prm["b_norm_w"][j], s0, C=CHUNK, valid=CHUNK, nb=nseq, tt=512)
            new["a_conv"].append(conv1)
            new["a_h"].append(h1)
            new["b_s"].append(s1)
            x = _outproj(x, mod, y1, y2, prm["w_out_even"], l, j, tm)
        else:
            z = _inproj(x, mod, prm["w_in_odd"], l, j, tm_in)
            c0, n0, m0 = st["c_c"][j], st["c_n"][j], st["c_m"][j]
            n0 = n0[:, :, None, :]
            m0 = jnp.broadcast_to(m0[:, :, None, None], m0.shape + (1, LANES))
            d_prm = prm["rwkv"][j]
            p0 = _pair_blockdiag(st["d_s"][j])
            if decode:
                z2 = z[0]
                zp = _pad_rows(z2)
                yc, c1, n1, m1 = _mlstm(zp, prm["c_gate_b"][j], prm["c_norm_w"][j], c0, n0, m0,
                                        C=PAD_L, valid=1, nb=8, tt=PAD_L)
                feats = _rwkv_pre_step(z2, st["d_shift"][j], d_prm)
                feats = [_pad_rows(t) for t in feats]
                yd, p1 = _rwkv(feats, prm["d_ln_w"][j], prm["d_ln_b"][j], p0, C=PAD_L, valid=1, nb=8, tt=PAD_L)
                y1, y2 = yc[None, :, 0], yd[None, :, 0]
                shift1 = z2[:, :P_D]
            else:
                y1, c1, n1, m1 = _mlstm(z, prm["c_gate_b"][j], prm["c_norm_w"][j], c0, n0, m0,
                                        C=CHUNK, valid=CHUNK, nb=nseq, tt=256)
                prev8 = jnp.concatenate([jnp.zeros((nseq, 7, P_D), F32), st["d_shift"][j][:, None]], axis=1)
                feats = _rwkv_pre_seq(z, prev8, d_prm, 128)
                y2, p1 = _rwkv(feats, prm["d_ln_w"][j], prm["d_ln_b"][j], p0, C=CHUNK, valid=CHUNK, nb=nseq, tt=256)
                shift1 = z[:, -1, :P_D]
            new["c_c"].append(c1)
            new["c_n"].append(n1[:, :, 0, :])
            new["c_m"].append(m1[:, :, 0, 0])
            new["d_shift"].append(shift1)
            new["d_s"].append(_pair_unblockdiag(p1))
            x = _outproj(x, mod, y1, y2, prm["w_out_odd"], l, j, tm)
        x = _ffn(x, mod, prm["wg"], prm["wu"], prm["wd"], l, 1, tm)
    y = _final_norm(x, prm["final_norm_w"], tm)
    return (y,) + tuple(jnp.stack(new[k]) for k in ("a_conv", "a_h", "b_s", "c_c", "c_n", "c_m", "d_shift", "d_s"))


def _prepare(w_mod, b_mod, ffn_w_gate, ffn_w_up, ffn_w_down, w_in_even, w_out_even, a_conv_w, a_conv_b, a_gate_r_w,
             a_gate_r_b, a_gate_i_w, a_gate_i_b, a_lambda, b_lb_gamma, b_norm_w, w_in_odd, w_out_odd, c_igate_b,
             c_fgate_b, c_norm_w, d_mu, d_w0, d_w2, d_a0, d_a2, d_g2, d_k_k, d_k_a, d_r_k, d_ln_w, d_ln_b,
             final_norm_w):
    n_gate = 2 * H_C
    odd = jnp.concatenate(
        [w_in_odd[:, :, P_ODD - P_D:], w_in_odd[:, :, :P_ODD - P_D - n_gate],
         w_in_odd[:, :, P_ODD - P_D - n_gate:P_ODD - P_D],
         jnp.zeros((N_ODD, D, P_ODD_PAD - P_ODD), F32)], axis=-1)
    row = lambda t: t[:, None, :]
    zpad = jnp.zeros((N_ODD, LANES - n_gate), F32)
    half = jnp.zeros((N_ODD, R_W, W_D), F32)
    rglru = [(a_conv_w[j], a_conv_b[j][None], a_gate_r_w[j].astype(BF16), a_gate_r_b[j][None],
              a_gate_i_w[j].astype(BF16), a_gate_i_b[j][None], a_lambda[j][None]) for j in range(N_EVEN)]
    w2p = jnp.concatenate([d_w2, half], axis=1).astype(BF16)
    a2p = jnp.concatenate([half, d_a2], axis=1).astype(BF16)
    g2 = d_g2.astype(BF16)
    rwkv = [(d_mu[j][None], d_w0[j][None], d_a0[j][None], d_k_k[j][None], d_k_a[j][None],
             d_r_k[j].reshape(1, W_D), w2p[j], a2p[j], g2[j]) for j in range(N_ODD)]
    return dict(
        wg=ffn_w_gate.astype(BF16), wu=ffn_w_up.astype(BF16), wd=ffn_w_down.astype(BF16),
        w_in_even=w_in_even.astype(BF16), w_in_odd=odd.astype(BF16),
        w_out_even=w_out_even.astype(BF16).reshape(N_EVEN, 2, D // 2, D),
        w_out_odd=w_out_odd.astype(BF16).reshape(N_ODD, 2, D // 2, D),
        rglru=rglru, rwkv=rwkv,
        lb=row(_lower_bounds(b_lb_gamma)), b_norm_w=row(b_norm_w),
        c_gate_b=row(jnp.concatenate([c_igate_b, c_fgate_b, zpad], axis=-1)), c_norm_w=row(c_norm_w),
        d_ln_w=row(d_ln_w), d_ln_b=row(d_ln_b), final_norm_w=final_norm_w)


def kernel(x_prompt, x_sample, c_prompt, c_sample, state_a_conv, state_a_h, state_b_s, state_c_c, state_c_n, state_c_m, state_d_shift, state_d_s, w_mod, b_mod, ffn_w_gate, ffn_w_up, ffn_w_down, w_in_even, w_out_even, a_conv_w, a_conv_b, a_gate_r_w, a_gate_r_b, a_gate_i_w, a_gate_i_b, a_lambda, b_lb_gamma, b_norm_w, w_in_odd, w_out_odd, c_igate_b, c_fgate_b, c_norm_w, d_mu, d_w0, d_w2, d_a0, d_a2, d_g2, d_k_k, d_k_a, d_r_k, d_ln_w, d_ln_b, final_norm_w):
    prm = _prepare(w_mod, b_mod, ffn_w_gate, ffn_w_up, ffn_w_down, w_in_even, w_out_even, a_conv_w, a_conv_b,
                   a_gate_r_w, a_gate_r_b, a_gate_i_w, a_gate_i_b, a_lambda, b_lb_gamma, b_norm_w, w_in_odd,
                   w_out_odd, c_igate_b, c_fgate_b, c_norm_w, d_mu, d_w0, d_w2, d_a0, d_a2, d_g2, d_k_k, d_k_a,
                   d_r_k, d_ln_w, d_ln_b, final_norm_w)
    bp, lp, _ = x_prompt.shape
    bs = x_sample.shape[0]
    n_rows = -(-(bs + bp) // 8) * 8
    c_all = jnp.concatenate([c_sample, c_prompt, jnp.zeros((n_rows - bs - bp, D), F32)], axis=0)
    mod_all = _modulation(c_all, w_mod, b_mod)
    mod_s = mod_all[:, None, :bs]
    mod_p = mod_all[:, bs:bs + bp, None]

    zeros = lambda *s: jnp.zeros(s, F32)
    st_p = dict(a_conv=zeros(N_EVEN, bp, CONV_W - 1, W_A), a_h=zeros(N_EVEN, bp, W_A),
                b_s=zeros(N_EVEN, bp, H_B, DK_B, DV_B), c_c=zeros(N_ODD, bp, H_C, DK_C, DV_C),
                c_n=zeros(N_ODD, bp, H_C, DK_C), c_m=zeros(N_ODD, bp, H_C), d_shift=zeros(N_ODD, bp, P_D),
                d_s=zeros(N_ODD, bp, H_D, N_D, N_D))
    st_s = dict(a_conv=state_a_conv, a_h=state_a_h, b_s=state_b_s, c_c=state_c_c, c_n=state_c_n, c_m=state_c_m,
                d_shift=state_d_shift, d_s=state_d_s)
    out_p = _trunk(x_prompt, mod_p, st_p, prm, decode=False)
    out_s = _trunk(x_sample.reshape(1, bs, D), mod_s, st_s, prm, decode=True)
    y_s = out_s[0].reshape(bs, 1, D)
    return (out_p[0], y_s) + out_p[1:] + out_s[1:]
```

```python
import functools

import jax
import jax.numpy as jnp
from jax import lax
from jax.experimental import pallas as pl
from jax.experimental.pallas import tpu as pltpu

F32 = jnp.float32
BF16 = jnp.bfloat16

D = 2048
DEPTH = 4
N_EVEN = 2
N_ODD = 2
D_FF = 5632
N_MOD = 9
EPS = 1e-6
CHUNK = 64
NEG_BIG = -1e30

W_A = 1024
NB_A = 8
BS_A = 128
CONV_W = 4
C_RGLRU = 8.0
H_B = 8
DK_B = 128
DV_B = 128
H_C = 4
DK_C = 128
DV_C = 256
N_D = 64
H_D = 16
W_D = 1024
R_W = 64
R_A = 64
R_G = 128
GN_EPS_D = 64e-5
P_EVEN = 6144
P_D = 3328
P_ODD = 6408
P_ODD_PAD = 6656
DEC_NB = 8
LANES = 128
VMEM_LIMIT = 48 * 2**20


def _cp(sem, vmem=VMEM_LIMIT):
    return pltpu.CompilerParams(dimension_semantics=sem, vmem_limit_bytes=vmem)


def _mm(a, b):
    return jnp.dot(a, b, preferred_element_type=F32)


def _dot(a, b):
    return _mm(a.astype(BF16), b.astype(BF16))


_NT = (((1,), (1,)), ((), ()))
_TN = (((0,), (0,)), ((), ()))
_NN = (((1,), (0,)), ((), ()))


def _dg(a, b, dn):
    return lax.dot_general(a, b, dn, preferred_element_type=F32)


def _dot_nt(a, b):
    return _dg(a.astype(BF16), b.astype(BF16), _NT)


def _dot_tn(a, b):
    return _dg(a.astype(BF16), b.astype(BF16), _TN)


def _split3(x):
    hi = x.astype(BF16)
    r1 = x - hi.astype(F32)
    mid = r1.astype(BF16)
    lo = (r1 - mid.astype(F32)).astype(BF16)
    return hi, mid, lo


def _split2(x):
    hi = x.astype(BF16)
    lo = (x - hi.astype(F32)).astype(BF16)
    return hi, lo


def _sel_dot(mask_bf, x):
    hi, mid, lo = _split3(x)
    return _mm(mask_bf, hi) + _mm(mask_bf, mid) + _mm(mask_bf, lo)


def _dot_sel(x, mask_bf):
    hi, mid, lo = _split3(x)
    return _mm(hi, mask_bf) + _mm(mid, mask_bf) + _mm(lo, mask_bf)


def _dot_sel_tn(x, mask_bf):
    hi, mid, lo = _split3(x)
    return _dg(hi, mask_bf, _TN) + _dg(mid, mask_bf, _TN) + _dg(lo, mask_bf, _TN)


def _mmx(a, b, dn=_NN):
    a_hi, a_lo = _split2(a)
    b_hi, b_lo = _split2(b)
    return _dg(a_hi, b_hi, dn) + _dg(a_hi, b_lo, dn) + _dg(a_lo, b_hi, dn)


def _iota(shape, dim):
    return lax.broadcasted_iota(jnp.int32, shape, dim)


def _sigmoid(x):
    return jax.nn.sigmoid(x)


def _softplus(x):
    return jnp.maximum(x, 0.0) + jnp.log1p(jnp.exp(-jnp.abs(x)))


def _log_sigmoid(x):
    return jnp.minimum(x, 0.0) - jnp.log1p(jnp.exp(-jnp.abs(x)))


def _gelu_tanh(x):
    return x * (0.5 * (1.0 + jnp.tanh(0.7978845608028654 * (x + 0.044715 * (x * x * x)))))


def _rms(x):
    return x * lax.rsqrt(jnp.mean(x * x, axis=-1, keepdims=True) + EPS)


def _adaln(x, shift, scale):
    return _rms(x) * (1.0 + scale) + shift


def _log2(n):
    k = n.bit_length() - 1
    assert (1 << k) == n
    return k


def _mod_spec(rows, width, layer, col):
    return pl.BlockSpec((None, None, rows, width), lambda g, i, n: (layer, g, 0, col))


def _mod_body(c_ref, w_ref, b_ref, o_ref):
    c = c_ref[...]
    cs = (c * _sigmoid(c)).astype(BF16)
    o_ref[...] = _mm(cs, w_ref[...].astype(BF16)) + b_ref[...]


def _modulation(c_all, w_mod, b_mod):
    rows = c_all.shape[0]
    tn = 1024
    return pl.pallas_call(
        _mod_body,
        grid=(DEPTH, N_MOD * D // tn),
        in_specs=[pl.BlockSpec((rows, D), lambda l, n: (0, 0)),
                  pl.BlockSpec((None, D, tn), lambda l, n: (l, 0, n)),
                  pl.BlockSpec((None, 1, tn), lambda l, n: (l, 0, n))],
        out_specs=pl.BlockSpec((None, rows, tn), lambda l, n: (l, 0, n)),
        out_shape=jax.ShapeDtypeStruct((DEPTH, rows, N_MOD * D), F32),
        compiler_params=_cp(("arbitrary", "arbitrary")),
        name="modulation",
    )(c_all, w_mod, b_mod.reshape(DEPTH, 1, N_MOD * D))


def _lower_bounds_body(g_ref, o_ref):
    g = g_ref[...]
    e = jnp.exp(g - jnp.max(g, axis=0, keepdims=True))
    sm = e / jnp.sum(e, axis=0, keepdims=True)
    acc = jnp.zeros_like(sm[0:1])
    for j in range(N_EVEN):
        acc = acc + sm[j:j + 1]
        o_ref[j:j + 1, :] = acc - sm[0:1]


def _lower_bounds(gamma):
    return pl.pallas_call(
        _lower_bounds_body,
        out_shape=jax.ShapeDtypeStruct(gamma.shape, F32),
        name="hgrn2_lower_bounds",
    )(gamma)


def _ffn_body(x_ref, sh_ref, sc_ref, gt_ref, wg_ref, wu_ref, wd_ref, o_ref, hf_ref, acc_ref):
    f = pl.program_id(2)

    @pl.when(f == 0)
    def _():
        hf_ref[...] = _adaln(x_ref[...], sh_ref[...], sc_ref[...]).astype(BF16)
        acc_ref[...] = jnp.zeros_like(acc_ref)

    hf = hf_ref[...]
    g = _mm(hf, wg_ref[...])
    u = _mm(hf, wu_ref[...])
    act = (g * _sigmoid(g) * u).astype(BF16)
    acc_ref[...] += _mm(act, wd_ref[...])

    @pl.when(f == pl.num_programs(2) - 1)
    def _():
        o_ref[...] = x_ref[...] + 0.5 * (1.0 + gt_ref[...]) * acc_ref[...]


def _ffn(x, mod, wg, wu, wd, layer, which, tm):
    G, Lg, _ = x.shape
    R = 1 if mod.shape[2] == 1 else tm
    tf = 512
    kb = 6 * which
    return pl.pallas_call(
        _ffn_body,
        grid=(G, Lg // tm, D_FF // tf),
        in_specs=[pl.BlockSpec((None, tm, D), lambda g, i, f: (g, i, 0)),
                  _mod_spec(R, D, layer, kb), _mod_spec(R, D, layer, kb + 1), _mod_spec(R, D, layer, kb + 2),
                  pl.BlockSpec((None, None, D, tf), lambda g, i, f: (layer, which, 0, f)),
                  pl.BlockSpec((None, None, D, tf), lambda g, i, f: (layer, which, 0, f)),
                  pl.BlockSpec((None, None, tf, D), lambda g, i, f: (layer, which, f, 0))],
        out_specs=pl.BlockSpec((None, tm, D), lambda g, i, f: (g, i, 0)),
        out_shape=jax.ShapeDtypeStruct(x.shape, F32),
        scratch_shapes=[pltpu.VMEM((tm, D), BF16), pltpu.VMEM((tm, D), F32)],
        compiler_params=_cp(("arbitrary", "arbitrary", "arbitrary")),
        name="adaln_swiglu_ffn",
    )(x, mod, mod, mod, wg, wu, wd)


def _inproj_body(x_ref, sh_ref, sc_ref, w_ref, o_ref, hf_ref):
    @pl.when(pl.program_id(2) == 0)
    def _():
        hf_ref[...] = _adaln(x_ref[...], sh_ref[...], sc_ref[...]).astype(BF16)

    o_ref[...] = _mm(hf_ref[...], w_ref[...])


def _inproj(x, mod, w, layer, j, tm):
    G, Lg, _ = x.shape
    R = 1 if mod.shape[2] == 1 else tm
    P = w.shape[-1]
    tn = 512
    return pl.pallas_call(
        _inproj_body,
        grid=(G, Lg // tm, P // tn),
        in_specs=[pl.BlockSpec((None, tm, D), lambda g, i, n: (g, i, 0)),
                  _mod_spec(R, D, layer, 3), _mod_spec(R, D, layer, 4),
                  pl.BlockSpec((None, D, tn), lambda g, i, n: (j, 0, n))],
        out_specs=pl.BlockSpec((None, tm, tn), lambda g, i, n: (g, i, n)),
        out_shape=jax.ShapeDtypeStruct((G, Lg, P), F32),
        scratch_shapes=[pltpu.VMEM((tm, D), BF16)],
        compiler_params=_cp(("arbitrary", "arbitrary", "arbitrary")),
        name="adaln_in_projection",
    )(x, mod, mod, w)


def _outproj_body(x_ref, gt_ref, y1_ref, y2_ref, w1_ref, w2_ref, o_ref):
    y = _dot(y1_ref[...], w1_ref[...]) + _dot(y2_ref[...], w2_ref[...])
    o_ref[...] = x_ref[...] + (1.0 + gt_ref[...]) * y


def _outproj(x, mod, y1, y2, w, layer, j, tm):
    G, Lg, _ = x.shape
    R = 1 if mod.shape[2] == 1 else tm
    tn = 1024
    nb = D // tn
    half = y1.shape[-1]
    return pl.pallas_call(
        _outproj_body,
        grid=(G, Lg // tm, nb),
        in_specs=[pl.BlockSpec((None, tm, tn), lambda g, i, n: (g, i, n)),
                  pl.BlockSpec((None, None, R, tn), lambda g, i, n: (layer, g, 0, 5 * nb + n)),
                  pl.BlockSpec((None, tm, half), lambda g, i, n: (g, i, 0)),
                  pl.BlockSpec((None, tm, half), lambda g, i, n: (g, i, 0)),
                  pl.BlockSpec((None, None, half, tn), lambda g, i, n: (j, 0, 0, n)),
                  pl.BlockSpec((None, None, half, tn), lambda g, i, n: (j, 1, 0, n))],
        out_specs=pl.BlockSpec((None, tm, tn), lambda g, i, n: (g, i, n)),
        out_shape=jax.ShapeDtypeStruct(x.shape, F32),
        compiler_params=_cp(("arbitrary", "arbitrary", "arbitrary")),
        name="out_projection_residual",
    )(x, mod, y1, y2, w, w)


def _final_body(x_ref, w_ref, o_ref):
    o_ref[...] = _rms(x_ref[...]) * w_ref[...]


def _final_norm(x, w, tm):
    G, Lg, _ = x.shape
    return pl.pallas_call(
        _final_body,
        grid=(G, Lg // tm),
        in_specs=[pl.BlockSpec((None, tm, D), lambda g, i: (g, i, 0)),
                  pl.BlockSpec((1, D), lambda g, i: (0, 0))],
        out_specs=pl.BlockSpec((None, tm, D), lambda g, i: (g, i, 0)),
        out_shape=jax.ShapeDtypeStruct(x.shape, F32),
        compiler_params=_cp(("arbitrary", "arbitrary")),
        name="final_rmsnorm",
    )(x, w.reshape(1, D))


def _rglru_gates(x0, x1, x2, x3, cw, cb, wr, br, wi, bi, lam):
    u = cb + x0 * cw[0:1] + x1 * cw[1:2] + x2 * cw[2:3] + x3 * cw[3:4]
    r_parts, i_parts = [], []
    for n in range(NB_A):
        un = u[:, n * BS_A:(n + 1) * BS_A].astype(BF16)
        r_parts.append(_mm(un, wr[n]))
        i_parts.append(_mm(un, wi[n]))
    r = _sigmoid(jnp.concatenate(r_parts, axis=1) + br)
    ig = _sigmoid(jnp.concatenate(i_parts, axis=1) + bi)
    log_a = -C_RGLRU * r * _softplus(-lam)
    a = jnp.exp(log_a)
    b = jnp.sqrt(-jnp.tanh(log_a) * (a * a + 1.0)) * (ig * u)
    return a, b


def _rglru_seq_body(x_ref, halo_ref, buf_ref, cw_ref, cb_ref, wr_ref, br_ref, wi_ref, bi_ref, lam_ref,
                    a_ref, b_ref, *, tt):
    i = pl.program_id(1)
    halo = jnp.where(i == 0, buf_ref[...], halo_ref[...])
    full = jnp.concatenate([halo, x_ref[...]], axis=0)
    taps = [pltpu.roll(full, CONV_W - 1 - k, axis=0)[8:8 + tt] for k in range(CONV_W - 1)]
    a, b = _rglru_gates(taps[0], taps[1], taps[2], x_ref[...], cw_ref[...], cb_ref[...], wr_ref[...], br_ref[...],
                        wi_ref[...], bi_ref[...], lam_ref[...])
    a_ref[...] = a
    b_ref[...] = b


def _rglru_param_specs(nidx):
    z2 = lambda *_: (0, 0)
    z3 = lambda *_: (0, 0, 0)
    del nidx
    return [pl.BlockSpec((CONV_W, W_A), z2), pl.BlockSpec((1, W_A), z2),
            pl.BlockSpec((NB_A, BS_A, BS_A), z3), pl.BlockSpec((1, W_A), z2),
            pl.BlockSpec((NB_A, BS_A, BS_A), z3), pl.BlockSpec((1, W_A), z2),
            pl.BlockSpec((1, W_A), z2)]


def _rglru_seq(z, buf8, prm, tt):
    B, L, _ = z.shape
    hb = tt // 8
    out = jax.ShapeDtypeStruct((B, L, W_A), F32)
    return pl.pallas_call(
        functools.partial(_rglru_seq_body, tt=tt),
        grid=(B, L // tt),
        in_specs=[pl.BlockSpec((None, tt, W_A), lambda b, i: (b, i, 0)),
                  pl.BlockSpec((None, 8, W_A), lambda b, i: (b, jnp.maximum(i * hb - 1, 0), 0)),
                  pl.BlockSpec((None, 8, W_A), lambda b, i: (b, 0, 0))] + _rglru_param_specs(2),
        out_specs=[pl.BlockSpec((None, tt, W_A), lambda b, i: (b, i, 0))] * 2,
        out_shape=[out, out],
        compiler_params=_cp(("arbitrary", "arbitrary")),
        name="rglru_conv_gates",
    )(z, z, buf8, *prm)


def _rglru_scan_body(a_ref, b_ref, ag_ref, h0_ref, y_ref, hl_ref, h_scr, *, tt):
    @pl.when(pl.program_id(1) == 0)
    def _():
        h_scr[...] = h0_ref[...]

    def step(t, h):
        h = a_ref[t] * h + b_ref[t]
        y_ref[t] = h
        return h

    h = lax.fori_loop(0, tt, step, h_scr[...], unroll=8)
    h_scr[...] = h
    hl_ref[...] = h
    y_ref[...] = y_ref[...] * _gelu_tanh(ag_ref[...])


def _rglru_scan(a, b, z, h0, tt):
    B, L, _ = a.shape
    a4 = a.reshape(B, L, 8, LANES)
    b4 = b.reshape(B, L, 8, LANES)
    z4 = z.reshape(B, L, z.shape[-1] // LANES, LANES)
    spec = pl.BlockSpec((None, tt, 8, LANES), lambda bb, i: (bb, i, 0, 0))
    y, hl = pl.pallas_call(
        functools.partial(_rglru_scan_body, tt=tt),
        grid=(B, L // tt),
        in_specs=[spec, spec,
                  pl.BlockSpec((None, tt, 8, LANES), lambda bb, i: (bb, i, 1, 0)),
                  pl.BlockSpec((None, 8, LANES), lambda bb, i: (bb, 0, 0))],
        out_specs=[spec, pl.BlockSpec((None, 8, LANES), lambda bb, i: (bb, 0, 0))],
        out_shape=[jax.ShapeDtypeStruct((B, L, 8, LANES), F32), jax.ShapeDtypeStruct((B, 8, LANES), F32)],
        scratch_shapes=[pltpu.VMEM((8, LANES), F32)],
        compiler_params=_cp(("arbitrary", "arbitrary")),
        name="rglru_scan_gelu_gate",
    )(a4, b4, z4, h0.reshape(B, 8, LANES))
    return y.reshape(B, L, W_A), hl.reshape(B, W_A)


def _rglru_step_body(x3_ref, ag_ref, x0_ref, x1_ref, x2_ref, h0_ref, cw_ref, cb_ref, wr_ref, br_ref, wi_ref, bi_ref,
                     lam_ref, y_ref, h_ref):
    a, b = _rglru_gates(x0_ref[...], x1_ref[...], x2_ref[...], x3_ref[...], cw_ref[...], cb_ref[...], wr_ref[...],
                        br_ref[...], wi_ref[...], bi_ref[...], lam_ref[...])
    h = a * h0_ref[...] + b
    h_ref[...] = h
    y_ref[...] = h * _gelu_tanh(ag_ref[...])


def _rglru_step(z, buf, h0, prm):
    n = z.shape[0]
    row = lambda c: pl.BlockSpec((n, W_A), lambda i: (0, c))
    out = jax.ShapeDtypeStruct((n, W_A), F32)
    return pl.pallas_call(
        _rglru_step_body,
        grid=(1,),
        in_specs=[row(0), row(1), row(0), row(0), row(0), row(0)] + _rglru_param_specs(1),
        out_specs=[row(0), row(0)],
        out_shape=[out, out],
        compiler_params=_cp(("arbitrary",)),
        name="rglru_decode_step",
    )(z, z, buf[:, 0], buf[:, 1], buf[:, 2], h0, *prm)


def _live_rows(C, first, valid):
    row = _iota((C, 1), 0)
    return (row >= first) & (row < first + valid)


def _hgrn2_chunk(q, fp, v, gate, lb, nw, S, *, C, valid, first=0):
    SB = min(16, C)
    k = (1.0 - lb) * _sigmoid(-fp)
    lf = jnp.log1p(-k)
    if valid < C:
        live = _live_rows(C, first, valid)
        k = jnp.where(live, k, 0.0)
        lf = jnp.where(live, lf, 0.0)
    g_col = _dot_sel_tn(lf, jnp.ones((C, DV_B), BF16))
    if valid == 1:
        kv = _dot_tn(k, v)
        yield
        S_new = jnp.exp(g_col) * S + kv
        o = _dot(q, S_new)
        yield
        return _rms(o) * nw * (gate * _sigmoid(gate)), S_new
    tril = _iota((C, C), 0) >= _iota((C, C), 1)
    b = _sel_dot(jnp.where(tril, 1.0, 0.0).astype(BF16), lf)
    yield
    g_row = b[C - 1:C, :]
    o = _dot(q * jnp.exp(b), S)
    khat = k * jnp.exp(g_row - b)
    S_new = jnp.exp(g_col) * S + _dot_tn(khat, v)

    tril_sb = _iota((SB, SB), 0) >= _iota((SB, SB), 1)
    atts = []
    for i in range(C // SB):
        lo = i * SB
        qi, bi, ki = q[lo:lo + SB], b[lo:lo + SB], k[lo:lo + SB]
        dec = jnp.exp(jnp.minimum(bi[:, None, :] - bi[None, :, :], 0.0))
        att = jnp.where(tril_sb, jnp.sum(qi[:, None, :] * ki[None, :, :] * dec, axis=-1), 0.0)
        off = None
        if i > 0:
            ref = b[lo - 1:lo, :]
            qt = qi * jnp.exp(bi - ref)
            kt = k[0:lo] * jnp.exp(ref - b[0:lo])
            off = _dot_nt(qt, kt)
        atts.append((att, off))
    yield
    parts = []
    for i, (att, off) in enumerate(atts):
        lo = i * SB
        oi = _dot(att, v[lo:lo + SB])
        parts.append(oi if off is None else oi + _dot(off, v[0:lo]))
    yield
    o = o + (parts[0] if len(parts) == 1 else jnp.concatenate(parts, axis=0))
    return _rms(o) * nw * (gate * _sigmoid(gate)), S_new


def _merge_rows(outs, C):
    row = _iota((C, 1), 0)
    y = jnp.where(row == 0, outs[0], 0.0)
    for s in range(1, len(outs)):
        y = jnp.where(row == s, outs[s], y)
    return y


def _hgrn2_body(q_ref, f_ref, v_ref, g_ref, lb_ref, nw_ref, s0_ref, y_ref, s1_ref, *, C, valid, nb, hp, decode,
                nchunk):
    @pl.when(pl.program_id(2) == 0)
    def _():
        s1_ref[...] = s0_ref[...]

    nw = nw_ref[...]
    probs = [(bi, hh) for bi in range(nb) for hh in range(hp)]

    def chunk(c, carry):
        rows = pl.ds(pl.multiple_of(c * C, C), C)
        gens = []
        for bi, hh in probs:
            lanes = slice(hh * LANES, (hh + 1) * LANES)
            src = 0 if decode else bi
            gens.append(_hgrn2_chunk(q_ref[src, rows, lanes], f_ref[src, rows, lanes], v_ref[src, rows, lanes],
                                     g_ref[src, rows, lanes], lb_ref[:, lanes], nw, s1_ref[bi, hh],
                                     C=C, valid=valid, first=bi if decode else 0))
        outs = _interleave(gens)
        for (bi, hh), (y, s_new) in zip(probs, outs):
            s1_ref[bi, hh] = s_new
            if not decode:
                y_ref[bi, rows, hh * LANES:(hh + 1) * LANES] = y
        if decode:
            for hh in range(hp):
                y_ref[0, rows, hh * LANES:(hh + 1) * LANES] = _merge_rows([outs[bi * hp + hh][0] for bi in range(nb)], C)
        return carry

    lax.fori_loop(0, nchunk, chunk, 0)


def _hgrn2(z, lb, nw, s0, *, C, valid, nb, hp, tt, decode):
    B, L, _ = z.shape
    zb = 1 if decode else nb
    w = hp * LANES
    col = lambda base: pl.BlockSpec((zb, tt, w), lambda b, h, i: (b, i, base // hp + h))
    st = pl.BlockSpec((nb, hp, DK_B, DV_B), lambda b, h, i: (b, h, 0, 0))
    return pl.pallas_call(
        functools.partial(_hgrn2_body, C=C, valid=valid, nb=nb, hp=hp, decode=decode, nchunk=tt // C),
        grid=(B // zb, H_B // hp, L // tt),
        in_specs=[col(16), col(24), col(32), col(40),
                  pl.BlockSpec((1, w), lambda b, h, i: (0, h)),
                  pl.BlockSpec((1, LANES), lambda b, h, i: (0, 0)), st],
        out_specs=[col(0), st],
        out_shape=[jax.ShapeDtypeStruct((B, L, H_B * DV_B), F32), jax.ShapeDtypeStruct(s0.shape, F32)],
        compiler_params=_cp(("arbitrary", "arbitrary", "arbitrary")),
        name="hgrn2_chunked",
    )(z, z, z, z, lb, nw, s0)


def _mlstm_step(q, k, v, op, gt, h, nw, Cst, n, m, *, C, first):
    lane = _iota((C, LANES), 1)
    live = _live_rows(C, first, 1)
    i0 = jnp.sum(jnp.where(live & (lane == h), gt, 0.0), keepdims=True)
    lf0 = _log_sigmoid(jnp.sum(jnp.where(live & (lane == h + H_C), gt, 0.0), keepdims=True))
    inter = lf0 + m
    m_t = jnp.maximum(inter, i0)
    w_inter = jnp.exp(inter - m_t)
    wk = jnp.exp(i0 - m_t)
    kh = k * (DK_C ** -0.5)
    s = jnp.sum(q * kh, axis=1, keepdims=True) * wk
    qc = _dot(q, Cst)
    kw = jnp.where(live, kh * wk, 0.0)
    kv = _dot_tn(kw, v)
    yield
    num = w_inter * qc + s * v
    den = w_inter * jnp.sum(q * n, axis=1, keepdims=True) + s
    hh = num / jnp.maximum(jnp.abs(den), jnp.exp(-m_t))
    C_new = w_inter * Cst + kv
    n_new = w_inter * n + jnp.sum(kw, axis=0, keepdims=True)
    y = _rms(hh) * nw * _sigmoid(op)
    return y, C_new, n_new, m_t


def _mlstm_chunk(q, k, v, op, gt, h, nw, Cst, n, m, *, C, valid, first=0):
    if valid == 1:
        return (yield from _mlstm_step(q, k, v, op, gt, h, nw, Cst, n, m, C=C, first=first))
    assert first == 0
    gtT = gt.T
    lane = _iota((C, LANES), 1)
    sub = _iota((LANES, C), 0)
    i_col = jnp.sum(jnp.where(lane == h, gt, 0.0), axis=1, keepdims=True)
    f_col = jnp.sum(jnp.where(lane == h + H_C, gt, 0.0), axis=1, keepdims=True)
    i_row = jnp.sum(jnp.where(sub == h, gtT, 0.0), axis=0, keepdims=True)
    f_row = jnp.sum(jnp.where(sub == h + H_C, gtT, 0.0), axis=0, keepdims=True)
    lf_col = _log_sigmoid(f_col)
    lf_row = _log_sigmoid(f_row)
    if valid < C:
        live_c = _iota((C, 1), 0) < valid
        live_r = _iota((1, C), 1) < valid
        i_col = jnp.where(live_c, i_col, NEG_BIG)
        i_row = jnp.where(live_r, i_row, NEG_BIG)
        lf_col = jnp.where(live_c, lf_col, 0.0)
        lf_row = jnp.where(live_r, lf_row, 0.0)
    tril = _iota((C, C), 0) >= _iota((C, C), 1)
    b_col = jnp.sum(jnp.where(tril, lf_row, 0.0), axis=1, keepdims=True)
    b_row = jnp.sum(jnp.where(_iota((C, C), 0) <= _iota((C, C), 1), lf_col, 0.0), axis=0, keepdims=True)
    dm = jnp.where(tril, b_col - b_row + i_row, NEG_BIG)
    inter = b_col + m
    m_t = jnp.maximum(inter, jnp.max(dm, axis=1, keepdims=True))
    w_inter = jnp.exp(inter - m_t)
    kh = k * (DK_C ** -0.5)
    qk = _dot_nt(q, kh)
    qc = _dot(q, Cst)
    g = b_col[C - 1:C, :]
    m_new = m_t[C - 1:C, :]
    wk = jnp.exp(g - b_col + i_col - m_new)
    f_state = jnp.exp(g + m - m_new)
    kw = kh * wk
    kv = _dot_tn(kw, v)
    yield
    s = qk * jnp.exp(dm - m_t)
    sv = _dot(s, v)
    yield
    num = w_inter * qc + sv
    den = w_inter * jnp.sum(q * n, axis=1, keepdims=True) + jnp.sum(s, axis=1, keepdims=True)
    hh = num / jnp.maximum(jnp.abs(den), jnp.exp(-m_t))
    C_new = f_state * Cst + kv
    n_new = f_state * n + jnp.sum(kw, axis=0, keepdims=True)
    y = _rms(hh) * nw * _sigmoid(op)
    return y, C_new, n_new, m_new


def _mlstm_body(q_ref, k_ref, v_ref, o_ref, g_ref, gb_ref, nw_ref, c0_ref, n0_ref, m0_ref,
                y_ref, c1_ref, n1_ref, m1_ref, *, C, valid, nb, decode, nchunk):
    @pl.when(pl.program_id(2) == 0)
    def _():
        c1_ref[...] = c0_ref[...]
        n1_ref[...] = n0_ref[...]
        m1_ref[...] = m0_ref[...]

    h = pl.program_id(1)
    gb = gb_ref[...]
    nw = nw_ref[...]

    def chunk(c, carry):
        rows = pl.ds(pl.multiple_of(c * C, C), C)
        gens = []
        for bi in range(nb):
            src = 0 if decode else bi
            gens.append(_mlstm_chunk(
                q_ref[src, rows, :], k_ref[src, rows, :], v_ref[src, rows, :], o_ref[src, rows, :],
                g_ref[src, rows, :] + gb, h, nw, c1_ref[bi], n1_ref[bi], m1_ref[bi][:, 0:1],
                C=C, valid=valid, first=bi if decode else 0))
        outs = _interleave(gens)
        for bi, (y, c_new, n_new, m_new) in enumerate(outs):
            if not decode:
                y_ref[bi, rows, :] = y
            c1_ref[bi] = c_new
            n1_ref[bi] = n_new
            m1_ref[bi] = jnp.broadcast_to(m_new, (1, LANES))
        if decode:
            y_ref[0, rows, :] = _merge_rows([o[0] for o in outs], C)
        return carry

    lax.fori_loop(0, nchunk, chunk, 0)


def _mlstm(z, gb, nw, c0, n0, m0, *, C, valid, nb, tt, decode):
    B, L, _ = z.shape
    zb = 1 if decode else nb
    c128 = lambda base: pl.BlockSpec((zb, tt, LANES), lambda b, h, i: (b, i, base + h))
    c256 = lambda base: pl.BlockSpec((zb, tt, DV_C), lambda b, h, i: (b, i, base + h))
    cst = pl.BlockSpec((nb, None, DK_C, DV_C), lambda b, h, i: (b, h, 0, 0))
    vec = pl.BlockSpec((nb, None, 1, LANES), lambda b, h, i: (b, h, 0, 0))
    return pl.pallas_call(
        functools.partial(_mlstm_body, C=C, valid=valid, nb=nb, decode=decode, nchunk=tt // C),
        grid=(B // zb, H_C, L // tt),
        in_specs=[c128(26), c128(30), c256(17), c256(21),
                  pl.BlockSpec((zb, tt, LANES), lambda b, h, i: (b, i, 50)),
                  pl.BlockSpec((1, LANES), lambda b, h, i: (0, 0)),
                  pl.BlockSpec((1, DV_C), lambda b, h, i: (0, 0)), cst, vec, vec],
        out_specs=[c256(0), cst, vec, vec],
        out_shape=[jax.ShapeDtypeStruct((B, L, H_C * DV_C), F32), jax.ShapeDtypeStruct(c0.shape, F32),
                   jax.ShapeDtypeStruct(n0.shape, F32), jax.ShapeDtypeStruct(m0.shape, F32)],
        compiler_params=_cp(("arbitrary", "arbitrary", "arbitrary")),
        name="mlstm_chunked",
    )(z, z, z, z, z, gb, nw, c0, n0, m0)


def _head_sum(x, bd):
    return jnp.concatenate([_dot_sel(x[:, n * LANES:(n + 1) * LANES], bd) for n in range(W_D // LANES)], axis=1)


def _rwkv_pre(zd, sh, mu, w0, a0, kkw, ka, rk, w2p, a2p, g2, bd):
    zs = zd + (sh - zd) * mu
    r, k, v = zs[:, 0:W_D], zs[:, W_D:2 * W_D], zs[:, 2 * W_D:3 * W_D]
    wa = zs[:, 3 * W_D:3 * W_D + LANES]
    gl = zs[:, 3 * W_D + LANES:]
    w = -_softplus(-(w0 + _dot(jnp.tanh(wa), w2p))) - 0.5
    logd = -jnp.exp(w)
    a = _sigmoid(a0 + _dot(wa, a2p))
    gate = _dot(_sigmoid(gl), g2)
    kk = k * kkw
    kk = kk / jnp.maximum(jnp.sqrt(_head_sum(kk * kk, bd)), 1e-12)
    kmod = k * (1.0 + (a - 1.0) * ka)
    bonus = _head_sum(r * kmod * rk, bd) * v
    return r, logd, kmod, v, kk, kk * a, gate, bonus


def _head_block_ones():
    return jnp.where((_iota((LANES, LANES), 0) >> 6) == (_iota((LANES, LANES), 1) >> 6), 1.0, 0.0).astype(BF16)


def _rwkv_pre_seq_body(z_ref, halo_ref, prev_ref, mu_ref, w0_ref, a0_ref, kk_ref, ka_ref, rk_ref, w2_ref, a2_ref,
                       g2_ref, *out_refs, tt):
    i = pl.program_id(1)
    halo = jnp.where(i == 0, prev_ref[...], halo_ref[...])
    zd = z_ref[...]
    sh = pltpu.roll(jnp.concatenate([halo, zd], axis=0), 1, axis=0)[8:8 + tt]
    outs = _rwkv_pre(zd, sh, mu_ref[...], w0_ref[...], a0_ref[...], kk_ref[...], ka_ref[...], rk_ref[...],
                     w2_ref[...], a2_ref[...], g2_ref[...], _head_block_ones())
    for ref, val in zip(out_refs, outs):
        ref[...] = val


def _rwkv_pre_step_body(z_ref, sh_ref, mu_ref, w0_ref, a0_ref, kk_ref, ka_ref, rk_ref, w2_ref, a2_ref, g2_ref,
                        *out_refs):
    outs = _rwkv_pre(z_ref[...], sh_ref[...], mu_ref[...], w0_ref[...], a0_ref[...], kk_ref[...], ka_ref[...],
                     rk_ref[...], w2_ref[...], a2_ref[...], g2_ref[...], _head_block_ones())
    for ref, val in zip(out_refs, outs):
        ref[...] = val


def _rwkv_pre_param_specs():
    z2 = lambda *_: (0, 0)
    return ([pl.BlockSpec((1, P_D), z2)] + [pl.BlockSpec((1, W_D), z2)] * 5
            + [pl.BlockSpec((LANES, W_D), z2)] * 3)


def _rwkv_pre_seq(z, prev8, prm, tt):
    B, L, _ = z.shape
    hb = tt // 8
    out = jax.ShapeDtypeStruct((B, L, W_D), F32)
    return pl.pallas_call(
        functools.partial(_rwkv_pre_seq_body, tt=tt),
        grid=(B, L // tt),
        in_specs=[pl.BlockSpec((None, tt, P_D), lambda b, i: (b, i, 0)),
                  pl.BlockSpec((None, 8, P_D), lambda b, i: (b, jnp.maximum(i * hb - 1, 0), 0)),
                  pl.BlockSpec((None, 8, P_D), lambda b, i: (b, 0, 0))] + _rwkv_pre_param_specs(),
        out_specs=[pl.BlockSpec((None, tt, W_D), lambda b, i: (b, i, 0))] * 8,
        out_shape=[out] * 8,
        compiler_params=_cp(("arbitrary", "arbitrary")),
        name="rwkv7_token_shift_features",
    )(z, z, prev8, *prm)


def _rwkv_pre_step(z, shifted, prm):
    n = z.shape[0]
    out = jax.ShapeDtypeStruct((n, W_D), F32)
    return pl.pallas_call(
        _rwkv_pre_step_body,
        grid=(1,),
        in_specs=[pl.BlockSpec((n, P_D), lambda i: (0, 0)), pl.BlockSpec((n, P_D), lambda i: (0, 0))]
        + _rwkv_pre_param_specs(),
        out_specs=[pl.BlockSpec((n, W_D), lambda i: (0, 0))] * 8,
        out_shape=[out] * 8,
        compiler_params=_cp(("arbitrary",)),
        name="rwkv7_decode_features",
    )(z, shifted, *prm)


def _unit_lower_inverse(N, C):
    n = N.shape[0]
    ri, ci = _iota((n, n), 0), _iota((n, n), 1)
    base = min(8, C)
    kb = _log2(base)
    X = jnp.where((ri >> kb) == (ci >> kb), N, 0.0)
    T = jnp.where(ri == ci, 1.0, 0.0) + X
    for _ in range(kb - 1):
        X = _dot(X, X)
        yield
        T = T + _dot(T, X)
        yield
    size = base
    while size < C:
        ks = _log2(size)
        sel = ((ri >> (ks + 1)) == (ci >> (ks + 1))) & (((ri >> ks) & 1) == 1) & (((ci >> ks) & 1) == 0)
        TN_ = _dot(T, jnp.where(sel, N, 0.0))
        yield
        T = T + _dot(TN_, T)
        yield
        size *= 2
    return T


def _interleave(gens):
    results = [None] * len(gens)
    live = list(range(len(gens)))
    while live:
        for i in list(live):
            try:
                next(gens[i])
            except StopIteration as stop:
                results[i] = stop.value
                live.remove(i)
    return results


def _rwkv_groupnorm_gate(y, gate, bonus, lnw, lnb):
    bd = _head_block_ones()
    mean = _dot_sel(y, bd) * (1.0 / N_D)
    yield
    dy = y - mean
    var = _dot_sel(dy * dy, bd) * (1.0 / N_D)
    yield
    return (dy * lax.rsqrt(var + GN_EPS_D) * lnw + lnb + bonus) * gate


def _rwkv_step(r, ld, k, v, kk, be, gate, bonus, lnw, lnb, Pn, *, C, first):
    live = _live_rows(C, first, 1)
    zero = lambda x: jnp.where(live, x, 0.0)
    d_row = jnp.exp(jnp.sum(zero(ld), axis=0, keepdims=True))
    U = _dot_nt(zero(-kk), Pn)
    yield
    upd = _dot_tn(jnp.concatenate([U, zero(v)], axis=0), jnp.concatenate([zero(be), zero(k)], axis=0))
    yield
    same_head = (_iota((LANES, LANES), 0) >> 6) == (_iota((LANES, LANES), 1) >> 6)
    P_new = Pn * d_row + jnp.where(same_head, upd, 0.0)
    y = _dot_nt(r, P_new)
    yield
    out = yield from _rwkv_groupnorm_gate(y, gate, bonus, lnw, lnb)
    return out, P_new


def _rwkv_chunk(r, ld, k, v, kk, be, gate, bonus, lnw, lnb, P, *, C, valid, first=0):
    if valid == 1:
        return (yield from _rwkv_step(r, ld, k, v, kk, be, gate, bonus, lnw, lnb, P, C=C, first=first))
    if valid < C:
        live = _live_rows(C, first, valid)
        ld = jnp.where(live, ld, 0.0)
        be = jnp.where(live, be, 0.0)
        k = jnp.where(live, k, 0.0)
        v = jnp.where(live, v, 0.0)
    tril = _iota((C, C), 0) >= _iota((C, C), 1)
    c = _sel_dot(jnp.where(tril, 1.0, 0.0).astype(BF16), ld)
    g_col = _dot_sel_tn(ld, jnp.ones((C, LANES), BF16))
    yield
    ec, enc = jnp.exp(c), jnp.exp(-c)
    lane0 = _iota((C, LANES), 1) < N_D

    def stack(x):
        return jnp.concatenate([jnp.where(lane0, x, 0.0), jnp.where(lane0, 0.0, x)], axis=0)

    n2 = 2 * C
    AR = jnp.concatenate([stack(-kk * jnp.exp(c - ld)), stack(r * ec)], axis=0).astype(BF16)
    BK = jnp.concatenate([stack(be * enc), stack(k * enc)], axis=0).astype(BF16)
    V2 = stack(v).astype(BF16)
    kc = _log2(C)
    ri, ci = _iota((2 * n2, 2 * n2), 0), _iota((2 * n2, 2 * n2), 1)
    tpos, spos = ri & (C - 1), ci & (C - 1)
    keep = (((ri >> kc) & 1) == ((ci >> kc) & 1)) & ((spos < tpos) | ((ri >= n2) & (spos == tpos)))
    G = jnp.where(keep, _dg(AR, BK, _NT), 0.0)
    LP = _mm(AR, P.astype(BF16))
    yield
    GV = _mm(G[:, n2:].astype(BF16), V2)
    T = yield from _unit_lower_inverse(G[0:n2, 0:n2], C)
    U = _dot(T, LP[0:n2] + GV[0:n2])
    yield
    Y2 = LP[n2:] + GV[n2:] + _dot(G[n2:, 0:n2], U)
    P_new = jnp.exp(g_col) * (P + _dg(BK, jnp.concatenate([U.astype(BF16), V2], axis=0), _TN))
    yield
    y = Y2[0:C] + Y2[C:n2]
    out = yield from _rwkv_groupnorm_gate(y, gate, bonus, lnw, lnb)
    return out, P_new


def _rwkv_body(r_ref, ld_ref, k_ref, v_ref, kk_ref, be_ref, gt_ref, bo_ref, lnw_ref, lnb_ref, p0_ref,
               y_ref, p1_ref, *, C, valid, nb, hp, decode, nchunk):
    @pl.when(pl.program_id(2) == 0)
    def _():
        p1_ref[...] = p0_ref[...]

    probs = [(bi, pp) for bi in range(nb) for pp in range(hp)]
    in_refs = (r_ref, ld_ref, k_ref, v_ref, kk_ref, be_ref, gt_ref, bo_ref)

    def chunk(c, carry):
        rows = pl.ds(pl.multiple_of(c * C, C), C)
        gens = []
        for bi, pp in probs:
            lanes = slice(pp * LANES, (pp + 1) * LANES)
            src = 0 if decode else bi
            args = [ref[src, rows, lanes] for ref in in_refs] + [lnw_ref[:, lanes], lnb_ref[:, lanes], p1_ref[bi, pp]]
            gens.append(_rwkv_chunk(*args, C=C, valid=valid, first=bi if decode else 0))
        outs = _interleave(gens)
        for (bi, pp), (y, p_new) in zip(probs, outs):
            p1_ref[bi, pp] = p_new
            if not decode:
                y_ref[bi, rows, pp * LANES:(pp + 1) * LANES] = y
        if decode:
            for pp in range(hp):
                y_ref[0, rows, pp * LANES:(pp + 1) * LANES] = _merge_rows([outs[bi * hp + pp][0] for bi in range(nb)], C)
        return carry

    lax.fori_loop(0, nchunk, chunk, 0)


def _rwkv(feats, lnw, lnb, p0, *, C, valid, nb, hp, tt, decode):
    B, L, _ = feats[0].shape
    zb = 1 if decode else nb
    w = hp * LANES
    col = pl.BlockSpec((zb, tt, w), lambda b, p, i: (b, i, p))
    vec = pl.BlockSpec((1, w), lambda b, p, i: (0, p))
    st = pl.BlockSpec((nb, hp, LANES, LANES), lambda b, p, i: (b, p, 0, 0))
    return pl.pallas_call(
        functools.partial(_rwkv_body, C=C, valid=valid, nb=nb, hp=hp, decode=decode, nchunk=tt // C),
        grid=(B // zb, H_D // 2 // hp, L // tt),
        in_specs=[col] * 8 + [vec, vec, st],
        out_specs=[col, st],
        out_shape=[jax.ShapeDtypeStruct((B, L, W_D), F32), jax.ShapeDtypeStruct(p0.shape, F32)],
        compiler_params=_cp(("arbitrary", "arbitrary", "arbitrary")),
        name="rwkv7_chunked",
    )(*feats, lnw, lnb, p0)


def _pair_blockdiag(s):
    B = s.shape[0]
    st = s.reshape(B, H_D // 2, 2, N_D, N_D)
    zero = jnp.zeros_like(st[:, :, 0])
    top = jnp.concatenate([st[:, :, 0], zero], axis=-1)
    bot = jnp.concatenate([zero, st[:, :, 1]], axis=-1)
    return jnp.concatenate([top, bot], axis=-2)


def _pair_unblockdiag(p):
    B = p.shape[0]
    st = jnp.stack([p[:, :, :N_D, :N_D], p[:, :, N_D:, N_D:]], axis=2)
    return st.reshape(B, H_D, N_D, N_D)


def _trunk(x, mod, st, prm, *, decode):
    G, Lg, _ = x.shape
    tm = min(128 if decode else 512, Lg)
    tm_in = tm if decode or Lg % 1024 else 1024
    assert decode or Lg % 512 == 0
    nseq = Lg if decode else G
    new = {k: [] for k in ("a_conv", "a_h", "b_s", "c_c", "c_n", "c_m", "d_shift", "d_s")}
    for l in range(DEPTH):
        j = l // 2
        x = _ffn(x, mod, prm["wg"], prm["wu"], prm["wd"], l, 0, tm)
        if l % 2 == 0:
            z = _inproj(x, mod, prm["w_in_even"], l, j, tm_in)
            a_prm = prm["rglru"][j]
            conv0, h0, s0 = st["a_conv"][j], st["a_h"][j], st["b_s"][j]
            if decode:
                z2 = z[0]
                y1, h1 = _rglru_step(z2, conv0, h0, a_prm)
                conv1 = jnp.concatenate([conv0[:, 1:], z2[:, None, :W_A]], axis=1)
                yb, s1 = _hgrn2(z2.reshape(nseq // DEC_NB, DEC_NB, -1), prm["lb"][j], prm["b_norm_w"][j], s0,
                                C=DEC_NB, valid=1, nb=DEC_NB, hp=1, tt=DEC_NB, decode=True)
                y1, y2 = y1[None], yb.reshape(1, nseq, -1)
            else:
                buf8 = jnp.concatenate([jnp.zeros((nseq, 8 - (CONV_W - 1), W_A), F32), conv0], axis=1)
                a, b = _rglru_seq(z, buf8, a_prm, 512)
                y1, h1 = _rglru_scan(a, b, z, h0, 512)
                conv1 = z[:, Lg - (CONV_W - 1):, :W_A]
                y2, s1 = _hgrn2(z, prm["lb"][j], prm["b_norm_w"][j], s0, C=CHUNK, valid=CHUNK, nb=nseq, hp=2, tt=512,
                                decode=False)
            new["a_conv"].append(conv1)
            new["a_h"].append(h1)
            new["b_s"].append(s1)
            x = _outproj(x, mod, y1, y2, prm["w_out_even"], l, j, tm)
        else:
            z = _inproj(x, mod, prm["w_in_odd"], l, j, tm_in)
            c0, n0, m0 = st["c_c"][j], st["c_n"][j], st["c_m"][j]
            n0 = n0[:, :, None, :]
            m0 = jnp.broadcast_to(m0[:, :, None, None], m0.shape + (1, LANES))
            d_prm = prm["rwkv"][j]
            if decode:
                z2 = z[0]
                blocks = lambda t: t.reshape(nseq // DEC_NB, DEC_NB, -1)
                yc, c1, n1, m1 = _mlstm(blocks(z2), prm["c_gate_b"][j], prm["c_norm_w"][j], c0, n0, m0,
                                        C=DEC_NB, valid=1, nb=DEC_NB, tt=DEC_NB, decode=True)
                feats = [blocks(t) for t in _rwkv_pre_step(z2, st["d_shift"][j], d_prm)]
                yd, p1 = _rwkv(feats, prm["d_ln_w"][j], prm["d_ln_b"][j], _pair_blockdiag(st["d_s"][j]),
                               C=DEC_NB, valid=1, nb=DEC_NB, hp=1, tt=DEC_NB, decode=True)
                y1, y2 = yc.reshape(1, nseq, -1), yd.reshape(1, nseq, -1)
                shift1 = z2[:, :P_D]
                sd1 = _pair_unblockdiag(p1)
            else:
                y1, c1, n1, m1 = _mlstm(z, prm["c_gate_b"][j], prm["c_norm_w"][j], c0, n0, m0,
                                        C=CHUNK, valid=CHUNK, nb=nseq, tt=256, decode=False)
                prev8 = jnp.concatenate([jnp.zeros((nseq, 7, P_D), F32), st["d_shift"][j][:, None]], axis=1)
                feats = _rwkv_pre_seq(z, prev8, d_prm, 128)
                p0 = _pair_blockdiag(jnp.swapaxes(st["d_s"][j], -1, -2))
                y2, p1 = _rwkv(feats, prm["d_ln_w"][j], prm["d_ln_b"][j], p0, C=CHUNK, valid=CHUNK, nb=nseq, hp=2,
                               tt=256, decode=False)
                shift1 = z[:, -1, :P_D]
                sd1 = jnp.swapaxes(_pair_unblockdiag(p1), -1, -2)
            new["c_c"].append(c1)
            new["c_n"].append(n1[:, :, 0, :])
            new["c_m"].append(m1[:, :, 0, 0])
            new["d_shift"].append(shift1)
            new["d_s"].append(sd1)
            x = _outproj(x, mod, y1, y2, prm["w_out_odd"], l, j, tm)
        x = _ffn(x, mod, prm["wg"], prm["wu"], prm["wd"], l, 1, tm)
    y = _final_norm(x, prm["final_norm_w"], tm)
    return (y,) + tuple(jnp.stack(new[k]) for k in ("a_conv", "a_h", "b_s", "c_c", "c_n", "c_m", "d_shift", "d_s"))


def _prepare(w_mod, b_mod, ffn_w_gate, ffn_w_up, ffn_w_down, w_in_even, w_out_even, a_conv_w, a_conv_b, a_gate_r_w,
             a_gate_r_b, a_gate_i_w, a_gate_i_b, a_lambda, b_lb_gamma, b_norm_w, w_in_odd, w_out_odd, c_igate_b,
             c_fgate_b, c_norm_w, d_mu, d_w0, d_w2, d_a0, d_a2, d_g2, d_k_k, d_k_a, d_r_k, d_ln_w, d_ln_b,
             final_norm_w):
    n_gate = 2 * H_C
    odd = jnp.concatenate(
        [w_in_odd[:, :, P_ODD - P_D:], w_in_odd[:, :, :P_ODD - P_D - n_gate],
         w_in_odd[:, :, P_ODD - P_D - n_gate:P_ODD - P_D],
         jnp.zeros((N_ODD, D, P_ODD_PAD - P_ODD), F32)], axis=-1)
    row = lambda t: t[:, None, :]
    zpad = jnp.zeros((N_ODD, LANES - n_gate), F32)
    half = jnp.zeros((N_ODD, R_W, W_D), F32)
    rglru = [(a_conv_w[j], a_conv_b[j][None], a_gate_r_w[j].astype(BF16), a_gate_r_b[j][None],
              a_gate_i_w[j].astype(BF16), a_gate_i_b[j][None], a_lambda[j][None]) for j in range(N_EVEN)]
    w2p = jnp.concatenate([d_w2, half], axis=1).astype(BF16)
    a2p = jnp.concatenate([half, d_a2], axis=1).astype(BF16)
    g2 = d_g2.astype(BF16)
    rwkv = [(d_mu[j][None], d_w0[j][None], d_a0[j][None], d_k_k[j][None], d_k_a[j][None],
             d_r_k[j].reshape(1, W_D), w2p[j], a2p[j], g2[j]) for j in range(N_ODD)]
    return dict(
        wg=ffn_w_gate.astype(BF16), wu=ffn_w_up.astype(BF16), wd=ffn_w_down.astype(BF16),
        w_in_even=w_in_even.astype(BF16), w_in_odd=odd.astype(BF16),
        w_out_even=w_out_even.astype(BF16).reshape(N_EVEN, 2, D // 2, D),
        w_out_odd=w_out_odd.astype(BF16).reshape(N_ODD, 2, D // 2, D),
        rglru=rglru, rwkv=rwkv,
        lb=row(_lower_bounds(b_lb_gamma)), b_norm_w=row(b_norm_w),
        c_gate_b=row(jnp.concatenate([c_igate_b, c_fgate_b, zpad], axis=-1)), c_norm_w=row(c_norm_w),
        d_ln_w=row(d_ln_w), d_ln_b=row(d_ln_b), final_norm_w=final_norm_w)


def kernel(x_prompt, x_sample, c_prompt, c_sample, state_a_conv, state_a_h, state_b_s, state_c_c, state_c_n, state_c_m, state_d_shift, state_d_s, w_mod, b_mod, ffn_w_gate, ffn_w_up, ffn_w_down, w_in_even, w_out_even, a_conv_w, a_conv_b, a_gate_r_w, a_gate_r_b, a_gate_i_w, a_gate_i_b, a_lambda, b_lb_gamma, b_norm_w, w_in_odd, w_out_odd, c_igate_b, c_fgate_b, c_norm_w, d_mu, d_w0, d_w2, d_a0, d_a2, d_g2, d_k_k, d_k_a, d_r_k, d_ln_w, d_ln_b, final_norm_w):
    prm = _prepare(w_mod, b_mod, ffn_w_gate, ffn_w_up, ffn_w_down, w_in_even, w_out_even, a_conv_w, a_conv_b,
                   a_gate_r_w, a_gate_r_b, a_gate_i_w, a_gate_i_b, a_lambda, b_lb_gamma, b_norm_w, w_in_odd,
                   w_out_odd, c_igate_b, c_fgate_b, c_norm_w, d_mu, d_w0, d_w2, d_a0, d_a2, d_g2, d_k_k, d_k_a,
                   d_r_k, d_ln_w, d_ln_b, final_norm_w)
    bp, lp, _ = x_prompt.shape
    bs = x_sample.shape[0]
    n_rows = -(-(bs + bp) // 8) * 8
    c_all = jnp.concatenate([c_sample, c_prompt, jnp.zeros((n_rows - bs - bp, D), F32)], axis=0)
    mod_all = _modulation(c_all, w_mod, b_mod)
    mod_s = mod_all[:, None]
    mod_p = mod_all[:, bs:bs + bp, None]

    zeros = lambda *s: jnp.zeros(s, F32)
    st_p = dict(a_conv=zeros(N_EVEN, bp, CONV_W - 1, W_A), a_h=zeros(N_EVEN, bp, W_A),
                b_s=zeros(N_EVEN, bp, H_B, DK_B, DV_B), c_c=zeros(N_ODD, bp, H_C, DK_C, DV_C),
                c_n=zeros(N_ODD, bp, H_C, DK_C), c_m=zeros(N_ODD, bp, H_C), d_shift=zeros(N_ODD, bp, P_D),
                d_s=zeros(N_ODD, bp, H_D, N_D, N_D))
    st_s = dict(a_conv=state_a_conv, a_h=state_a_h, b_s=state_b_s, c_c=state_c_c, c_n=state_c_n, c_m=state_c_m,
                d_shift=state_d_shift, d_s=state_d_s)
    out_p = _trunk(x_prompt, mod_p, st_p, prm, decode=False)
    out_s = _trunk(x_sample.reshape(1, bs, D), mod_s, st_s, prm, decode=True)
    y_s = out_s[0].reshape(bs, 1, D)
    return (out_p[0], y_s) + out_p[1:] + out_s[1:]
```

```python
import functools

import jax
import jax.numpy as jnp
from jax import lax
from jax.experimental import pallas as pl
from jax.experimental.pallas import tpu as pltpu

F32 = jnp.float32
BF16 = jnp.bfloat16

D = 2048
DEPTH = 4
N_EVEN = 2
N_ODD = 2
D_FF = 5632
N_MOD = 9
EPS = 1e-6
CHUNK = 64
NEG_BIG = -1e30

W_A = 1024
NB_A = 8
BS_A = 128
CONV_W = 4
C_RGLRU = 8.0
H_B = 8
DK_B = 128
DV_B = 128
H_C = 4
DK_C = 128
DV_C = 256
N_D = 64
H_D = 16
W_D = 1024
R_W = 64
R_A = 64
R_G = 128
GN_EPS_D = 64e-5
P_EVEN = 6144
P_D = 3328
P_ODD = 6408
MLSTM_COLS = 3584
DEC_NB = 8
LANES = 128
VMEM_LIMIT = 48 * 2**20


def _cp(sem, vmem=VMEM_LIMIT):
    return pltpu.CompilerParams(dimension_semantics=sem, vmem_limit_bytes=vmem)


def _mm(a, b):
    return jnp.dot(a, b, preferred_element_type=F32)


def _dot(a, b):
    return _mm(a.astype(BF16), b.astype(BF16))


_NT = (((1,), (1,)), ((), ()))
_TN = (((0,), (0,)), ((), ()))
_NN = (((1,), (0,)), ((), ()))


def _dg(a, b, dn):
    return lax.dot_general(a, b, dn, preferred_element_type=F32)


def _dot_nt(a, b):
    return _dg(a.astype(BF16), b.astype(BF16), _NT)


def _dot_tn(a, b):
    return _dg(a.astype(BF16), b.astype(BF16), _TN)


def _split3(x):
    hi = x.astype(BF16)
    r1 = x - hi.astype(F32)
    mid = r1.astype(BF16)
    lo = (r1 - mid.astype(F32)).astype(BF16)
    return hi, mid, lo


def _split2(x):
    hi = x.astype(BF16)
    lo = (x - hi.astype(F32)).astype(BF16)
    return hi, lo


def _sel_dot(mask_bf, x):
    hi, mid, lo = _split3(x)
    return _mm(mask_bf, hi) + _mm(mask_bf, mid) + _mm(mask_bf, lo)


def _dot_sel(x, mask_bf):
    hi, mid, lo = _split3(x)
    return _mm(hi, mask_bf) + _mm(mid, mask_bf) + _mm(lo, mask_bf)


def _dot_sel_tn(x, mask_bf):
    hi, mid, lo = _split3(x)
    return _dg(hi, mask_bf, _TN) + _dg(mid, mask_bf, _TN) + _dg(lo, mask_bf, _TN)


def _mmx(a, b, dn=_NN):
    a_hi, a_lo = _split2(a)
    b_hi, b_lo = _split2(b)
    return _dg(a_hi, b_hi, dn) + _dg(a_hi, b_lo, dn) + _dg(a_lo, b_hi, dn)


def _iota(shape, dim):
    return lax.broadcasted_iota(jnp.int32, shape, dim)


def _sigmoid(x):
    return jax.nn.sigmoid(x)


def _softplus(x):
    return jnp.maximum(x, 0.0) + jnp.log1p(jnp.exp(-jnp.abs(x)))


def _log_sigmoid(x):
    return jnp.minimum(x, 0.0) - jnp.log1p(jnp.exp(-jnp.abs(x)))


def _gelu_tanh(x):
    return x * (0.5 * (1.0 + jnp.tanh(0.7978845608028654 * (x + 0.044715 * (x * x * x)))))


def _rms(x):
    return x * lax.rsqrt(jnp.mean(x * x, axis=-1, keepdims=True) + EPS)


def _adaln(x, shift, scale):
    return _rms(x) * (1.0 + scale) + shift


def _log2(n):
    k = n.bit_length() - 1
    assert (1 << k) == n
    return k


def _mod_spec(rows, width, layer, col):
    return pl.BlockSpec((None, None, rows, width), lambda g, i, n: (layer, g, 0, col))


def _mod_body(c_ref, w_ref, b_ref, o_ref):
    c = c_ref[...]
    cs = (c * _sigmoid(c)).astype(BF16)
    o_ref[...] = _mm(cs, w_ref[...].astype(BF16)) + b_ref[...]


def _modulation(c_all, w_mod, b_mod):
    rows = c_all.shape[0]
    tn = 1024
    return pl.pallas_call(
        _mod_body,
        grid=(DEPTH, N_MOD * D // tn),
        in_specs=[pl.BlockSpec((rows, D), lambda l, n: (0, 0)),
                  pl.BlockSpec((None, D, tn), lambda l, n: (l, 0, n)),
                  pl.BlockSpec((None, 1, tn), lambda l, n: (l, 0, n))],
        out_specs=pl.BlockSpec((None, rows, tn), lambda l, n: (l, 0, n)),
        out_shape=jax.ShapeDtypeStruct((DEPTH, rows, N_MOD * D), F32),
        compiler_params=_cp(("arbitrary", "arbitrary")),
        name="modulation",
    )(c_all, w_mod, b_mod.reshape(DEPTH, 1, N_MOD * D))


def _lower_bounds_body(g_ref, o_ref):
    g = g_ref[...]
    e = jnp.exp(g - jnp.max(g, axis=0, keepdims=True))
    sm = e / jnp.sum(e, axis=0, keepdims=True)
    acc = jnp.zeros_like(sm[0:1])
    for j in range(N_EVEN):
        acc = acc + sm[j:j + 1]
        o_ref[j:j + 1, :] = acc - sm[0:1]


def _lower_bounds(gamma):
    return pl.pallas_call(
        _lower_bounds_body,
        out_shape=jax.ShapeDtypeStruct(gamma.shape, F32),
        name="hgrn2_lower_bounds",
    )(gamma)


def _ffn_body(x_ref, sh_ref, sc_ref, gt_ref, wg_ref, wu_ref, wd_ref, o_ref, hf_ref, acc_ref):
    f = pl.program_id(2)

    @pl.when(f == 0)
    def _():
        hf_ref[...] = _adaln(x_ref[...], sh_ref[...], sc_ref[...]).astype(BF16)
        acc_ref[...] = jnp.zeros_like(acc_ref)

    hf = hf_ref[...]
    g = _mm(hf, wg_ref[...])
    u = _mm(hf, wu_ref[...])
    act = (g * _sigmoid(g) * u).astype(BF16)
    acc_ref[...] += _mm(act, wd_ref[...])

    @pl.when(f == pl.num_programs(2) - 1)
    def _():
        o_ref[...] = x_ref[...] + 0.5 * (1.0 + gt_ref[...]) * acc_ref[...]


def _ffn(x, mod, wg, wu, wd, layer, which, tm):
    G, Lg, _ = x.shape
    R = 1 if mod.shape[2] == 1 else tm
    tf = 512
    kb = 6 * which
    return pl.pallas_call(
        _ffn_body,
        grid=(G, Lg // tm, D_FF // tf),
        in_specs=[pl.BlockSpec((None, tm, D), lambda g, i, f: (g, i, 0)),
                  _mod_spec(R, D, layer, kb), _mod_spec(R, D, layer, kb + 1), _mod_spec(R, D, layer, kb + 2),
                  pl.BlockSpec((None, None, D, tf), lambda g, i, f: (layer, which, 0, f)),
                  pl.BlockSpec((None, None, D, tf), lambda g, i, f: (layer, which, 0, f)),
                  pl.BlockSpec((None, None, tf, D), lambda g, i, f: (layer, which, f, 0))],
        out_specs=pl.BlockSpec((None, tm, D), lambda g, i, f: (g, i, 0)),
        out_shape=jax.ShapeDtypeStruct(x.shape, F32),
        scratch_shapes=[pltpu.VMEM((tm, D), BF16), pltpu.VMEM((tm, D), F32)],
        compiler_params=_cp(("arbitrary", "arbitrary", "arbitrary")),
        name="adaln_swiglu_ffn",
    )(x, mod, mod, mod, wg, wu, wd)


def _inproj_body(x_ref, sh_ref, sc_ref, w_ref, o_ref, hf_ref):
    @pl.when(pl.program_id(2) == 0)
    def _():
        hf_ref[...] = _adaln(x_ref[...], sh_ref[...], sc_ref[...]).astype(BF16)

    o_ref[...] = _mm(hf_ref[...], w_ref[...])


def _inproj(x, mod, w, layer, j, tm, width=None, tn=512):
    G, Lg, _ = x.shape
    R = 1 if mod.shape[2] == 1 else tm
    P = w.shape[-1] if width is None else width
    assert P % tn == 0 and P <= w.shape[-1]
    return pl.pallas_call(
        _inproj_body,
        grid=(G, Lg // tm, P // tn),
        in_specs=[pl.BlockSpec((None, tm, D), lambda g, i, n: (g, i, 0)),
                  _mod_spec(R, D, layer, 3), _mod_spec(R, D, layer, 4),
                  pl.BlockSpec((None, D, tn), lambda g, i, n: (j, 0, n))],
        out_specs=pl.BlockSpec((None, tm, tn), lambda g, i, n: (g, i, n)),
        out_shape=jax.ShapeDtypeStruct((G, Lg, P), F32),
        scratch_shapes=[pltpu.VMEM((tm, D), BF16)],
        compiler_params=_cp(("arbitrary", "arbitrary", "arbitrary")),
        name="adaln_in_projection",
    )(x, mod, mod, w)


def _outproj_body(x_ref, gt_ref, y1_ref, y2_ref, w1_ref, w2_ref, o_ref):
    y = _dot(y1_ref[...], w1_ref[...]) + _dot(y2_ref[...], w2_ref[...])
    o_ref[...] = x_ref[...] + (1.0 + gt_ref[...]) * y


def _outproj(x, mod, y1, y2, w, layer, j, tm):
    G, Lg, _ = x.shape
    R = 1 if mod.shape[2] == 1 else tm
    tn = D
    nb = D // tn
    half = y1.shape[-1]
    return pl.pallas_call(
        _outproj_body,
        grid=(G, Lg // tm, nb),
        in_specs=[pl.BlockSpec((None, tm, tn), lambda g, i, n: (g, i, n)),
                  pl.BlockSpec((None, None, R, tn), lambda g, i, n: (layer, g, 0, 5 * nb + n)),
                  pl.BlockSpec((None, tm, half), lambda g, i, n: (g, i, 0)),
                  pl.BlockSpec((None, tm, half), lambda g, i, n: (g, i, 0)),
                  pl.BlockSpec((None, None, half, tn), lambda g, i, n: (j, 0, 0, n)),
                  pl.BlockSpec((None, None, half, tn), lambda g, i, n: (j, 1, 0, n))],
        out_specs=pl.BlockSpec((None, tm, tn), lambda g, i, n: (g, i, n)),
        out_shape=jax.ShapeDtypeStruct(x.shape, F32),
        compiler_params=_cp(("arbitrary", "arbitrary", "arbitrary")),
        name="out_projection_residual",
    )(x, mod, y1, y2, w, w)


def _final_body(x_ref, w_ref, o_ref):
    o_ref[...] = _rms(x_ref[...]) * w_ref[...]


def _final_norm(x, w, tm):
    G, Lg, _ = x.shape
    return pl.pallas_call(
        _final_body,
        grid=(G, Lg // tm),
        in_specs=[pl.BlockSpec((None, tm, D), lambda g, i: (g, i, 0)),
                  pl.BlockSpec((1, D), lambda g, i: (0, 0))],
        out_specs=pl.BlockSpec((None, tm, D), lambda g, i: (g, i, 0)),
        out_shape=jax.ShapeDtypeStruct(x.shape, F32),
        compiler_params=_cp(("arbitrary", "arbitrary")),
        name="final_rmsnorm",
    )(x, w.reshape(1, D))


def _rglru_gates(x0, x1, x2, x3, cw, cb, wr, br, wi, bi, lam):
    u = cb + x0 * cw[0:1] + x1 * cw[1:2] + x2 * cw[2:3] + x3 * cw[3:4]
    r_parts, i_parts = [], []
    for n in range(NB_A):
        un = u[:, n * BS_A:(n + 1) * BS_A].astype(BF16)
        r_parts.append(_mm(un, wr[n]))
        i_parts.append(_mm(un, wi[n]))
    r = _sigmoid(jnp.concatenate(r_parts, axis=1) + br)
    ig = _sigmoid(jnp.concatenate(i_parts, axis=1) + bi)
    log_a = -C_RGLRU * r * _softplus(-lam)
    a = jnp.exp(log_a)
    b = jnp.sqrt(-jnp.tanh(log_a) * (a * a + 1.0)) * (ig * u)
    return a, b


def _rglru_seq_body(x_ref, halo_ref, buf_ref, cw_ref, cb_ref, wr_ref, br_ref, wi_ref, bi_ref, lam_ref,
                    a_ref, b_ref, *, tt):
    i = pl.program_id(1)
    halo = jnp.where(i == 0, buf_ref[...], halo_ref[...])
    full = jnp.concatenate([halo, x_ref[...]], axis=0)
    taps = [pltpu.roll(full, CONV_W - 1 - k, axis=0)[8:8 + tt] for k in range(CONV_W - 1)]
    a, b = _rglru_gates(taps[0], taps[1], taps[2], x_ref[...], cw_ref[...], cb_ref[...], wr_ref[...], br_ref[...],
                        wi_ref[...], bi_ref[...], lam_ref[...])
    a_ref[...] = a
    b_ref[...] = b


def _rglru_param_specs(nidx):
    z2 = lambda *_: (0, 0)
    z3 = lambda *_: (0, 0, 0)
    del nidx
    return [pl.BlockSpec((CONV_W, W_A), z2), pl.BlockSpec((1, W_A), z2),
            pl.BlockSpec((NB_A, BS_A, BS_A), z3), pl.BlockSpec((1, W_A), z2),
            pl.BlockSpec((NB_A, BS_A, BS_A), z3), pl.BlockSpec((1, W_A), z2),
            pl.BlockSpec((1, W_A), z2)]


def _rglru_seq(z, buf8, prm, tt):
    B, L, _ = z.shape
    hb = tt // 8
    out = jax.ShapeDtypeStruct((B, L, W_A), F32)
    return pl.pallas_call(
        functools.partial(_rglru_seq_body, tt=tt),
        grid=(B, L // tt),
        in_specs=[pl.BlockSpec((None, tt, W_A), lambda b, i: (b, i, 0)),
                  pl.BlockSpec((None, 8, W_A), lambda b, i: (b, jnp.maximum(i * hb - 1, 0), 0)),
                  pl.BlockSpec((None, 8, W_A), lambda b, i: (b, 0, 0))] + _rglru_param_specs(2),
        out_specs=[pl.BlockSpec((None, tt, W_A), lambda b, i: (b, i, 0))] * 2,
        out_shape=[out, out],
        compiler_params=_cp(("arbitrary", "arbitrary")),
        name="rglru_conv_gates",
    )(z, z, buf8, *prm)


def _rglru_scan_body(a_ref, b_ref, ag_ref, h0_ref, y_ref, hl_ref, h_scr, *, tt):
    @pl.when(pl.program_id(1) == 0)
    def _():
        h_scr[...] = h0_ref[...]

    def step(t, h):
        h = a_ref[t] * h + b_ref[t]
        y_ref[t] = h
        return h

    h = lax.fori_loop(0, tt, step, h_scr[...], unroll=8)
    h_scr[...] = h
    hl_ref[...] = h
    y_ref[...] = y_ref[...] * _gelu_tanh(ag_ref[...])


def _rglru_scan(a, b, z, h0, tt):
    B, L, _ = a.shape
    a4 = a.reshape(B, L, 8, LANES)
    b4 = b.reshape(B, L, 8, LANES)
    z4 = z.reshape(B, L, z.shape[-1] // LANES, LANES)
    spec = pl.BlockSpec((None, tt, 8, LANES), lambda bb, i: (bb, i, 0, 0))
    y, hl = pl.pallas_call(
        functools.partial(_rglru_scan_body, tt=tt),
        grid=(B, L // tt),
        in_specs=[spec, spec,
                  pl.BlockSpec((None, tt, 8, LANES), lambda bb, i: (bb, i, 1, 0)),
                  pl.BlockSpec((None, 8, LANES), lambda bb, i: (bb, 0, 0))],
        out_specs=[spec, pl.BlockSpec((None, 8, LANES), lambda bb, i: (bb, 0, 0))],
        out_shape=[jax.ShapeDtypeStruct((B, L, 8, LANES), F32), jax.ShapeDtypeStruct((B, 8, LANES), F32)],
        scratch_shapes=[pltpu.VMEM((8, LANES), F32)],
        compiler_params=_cp(("arbitrary", "arbitrary")),
        name="rglru_scan_gelu_gate",
    )(a4, b4, z4, h0.reshape(B, 8, LANES))
    return y.reshape(B, L, W_A), hl.reshape(B, W_A)


def _rglru_step_body(x3_ref, ag_ref, x0_ref, x1_ref, x2_ref, h0_ref, cw_ref, cb_ref, wr_ref, br_ref, wi_ref, bi_ref,
                     lam_ref, y_ref, h_ref):
    a, b = _rglru_gates(x0_ref[...], x1_ref[...], x2_ref[...], x3_ref[...], cw_ref[...], cb_ref[...], wr_ref[...],
                        br_ref[...], wi_ref[...], bi_ref[...], lam_ref[...])
    h = a * h0_ref[...] + b
    h_ref[...] = h
    y_ref[...] = h * _gelu_tanh(ag_ref[...])


def _rglru_step(z, buf, h0, prm):
    n = z.shape[0]
    row = lambda c: pl.BlockSpec((n, W_A), lambda i: (0, c))
    out = jax.ShapeDtypeStruct((n, W_A), F32)
    return pl.pallas_call(
        _rglru_step_body,
        grid=(1,),
        in_specs=[row(0), row(1), row(0), row(0), row(0), row(0)] + _rglru_param_specs(1),
        out_specs=[row(0), row(0)],
        out_shape=[out, out],
        compiler_params=_cp(("arbitrary",)),
        name="rglru_decode_step",
    )(z, z, buf[:, 0], buf[:, 1], buf[:, 2], h0, *prm)


def _live_rows(C, first, valid):
    row = _iota((C, 1), 0)
    return (row >= first) & (row < first + valid)


def _hgrn2_chunk(q, fp, v, gate, lb, nw, S, *, C, valid, first=0):
    SB = min(16, C)
    k = (1.0 - lb) * _sigmoid(-fp)
    lf = jnp.log1p(-k)
    if valid < C:
        live = _live_rows(C, first, valid)
        k = jnp.where(live, k, 0.0)
        lf = jnp.where(live, lf, 0.0)
    g_col = _dot_sel_tn(lf, jnp.ones((C, DV_B), BF16))
    if valid == 1:
        kv = _dot_tn(k, v)
        yield
        S_new = jnp.exp(g_col) * S + kv
        o = _dot(q, S_new)
        yield
        return _rms(o) * nw * (gate * _sigmoid(gate)), S_new
    tril = _iota((C, C), 0) >= _iota((C, C), 1)
    b = _sel_dot(jnp.where(tril, 1.0, 0.0).astype(BF16), lf)
    yield
    g_row = b[C - 1:C, :]
    o = _dot(q * jnp.exp(b), S)
    khat = k * jnp.exp(g_row - b)
    S_new = jnp.exp(g_col) * S + _dot_tn(khat, v)

    tril_sb = _iota((SB, SB), 0) >= _iota((SB, SB), 1)
    atts = []
    for i in range(C // SB):
        lo = i * SB
        qi, bi, ki = q[lo:lo + SB], b[lo:lo + SB], k[lo:lo + SB]
        dec = jnp.exp(jnp.minimum(bi[:, None, :] - bi[None, :, :], 0.0))
        att = jnp.where(tril_sb, jnp.sum(qi[:, None, :] * ki[None, :, :] * dec, axis=-1), 0.0)
        off = None
        if i > 0:
            ref = b[lo - 1:lo, :]
            qt = qi * jnp.exp(bi - ref)
            kt = k[0:lo] * jnp.exp(ref - b[0:lo])
            off = _dot_nt(qt, kt)
        atts.append((att, off))
    yield
    parts = []
    for i, (att, off) in enumerate(atts):
        lo = i * SB
        oi = _dot(att, v[lo:lo + SB])
        parts.append(oi if off is None else oi + _dot(off, v[0:lo]))
    yield
    o = o + (parts[0] if len(parts) == 1 else jnp.concatenate(parts, axis=0))
    return _rms(o) * nw * (gate * _sigmoid(gate)), S_new


def _merge_rows(outs, C):
    row = _iota((C, 1), 0)
    y = jnp.where(row == 0, outs[0], 0.0)
    for s in range(1, len(outs)):
        y = jnp.where(row == s, outs[s], y)
    return y


def _chained(prev, out_index, operands, specs):
    if prev is None:
        return {}
    operands.append(prev)
    specs.append(pl.BlockSpec(memory_space=pl.ANY))
    return {len(operands) - 1: out_index}


def _hgrn2_body(q_ref, f_ref, v_ref, g_ref, lb_ref, nw_ref, s0_ref, *rest, C, valid, nb, hp, decode, nchunk):
    y_ref, s1_ref = rest[-2:]

    @pl.when(pl.program_id(2) == 0)
    def _():
        s1_ref[...] = s0_ref[...]

    nw = nw_ref[...]
    probs = [(bi, hh) for bi in range(nb) for hh in range(hp)]

    def chunk(c, carry):
        rows = pl.ds(pl.multiple_of(c * C, C), C)
        gens = []
        for bi, hh in probs:
            lanes = slice(hh * LANES, (hh + 1) * LANES)
            src = 0 if decode else bi
            gens.append(_hgrn2_chunk(q_ref[src, rows, lanes], f_ref[src, rows, lanes], v_ref[src, rows, lanes],
                                     g_ref[src, rows, lanes], lb_ref[:, lanes], nw, s1_ref[bi, hh],
                                     C=C, valid=valid, first=bi if decode else 0))
        outs = _interleave(gens)
        for (bi, hh), (y, s_new) in zip(probs, outs):
            s1_ref[bi, hh] = s_new
            if not decode:
                y_ref[bi, rows, hh * LANES:(hh + 1) * LANES] = y
        if decode:
            for hh in range(hp):
                y_ref[0, rows, hh * LANES:(hh + 1) * LANES] = _merge_rows([outs[bi * hp + hh][0] for bi in range(nb)], C)
        return carry

    lax.fori_loop(0, nchunk, chunk, 0)


def _hgrn2(z, lb, nw, s_all, s_prev, j, *, C, valid, nb, hp, tt, decode):
    B, L, _ = z.shape
    zb = 1 if decode else nb
    w = hp * LANES
    col = lambda base: pl.BlockSpec((zb, tt, w), lambda b, h, i: (b, i, base // hp + h))
    st = pl.BlockSpec((None, nb, hp, DK_B, DV_B), lambda b, h, i: (j, b, h, 0, 0))
    operands = [z, z, z, z, lb, nw, s_all]
    specs = [col(16), col(24), col(32), col(40), pl.BlockSpec((1, w), lambda b, h, i: (0, h)),
             pl.BlockSpec((1, LANES), lambda b, h, i: (0, 0)), st]
    aliases = _chained(s_prev, 1, operands, specs)
    return pl.pallas_call(
        functools.partial(_hgrn2_body, C=C, valid=valid, nb=nb, hp=hp, decode=decode, nchunk=tt // C),
        grid=(B // zb, H_B // hp, L // tt),
        in_specs=specs,
        out_specs=[col(0), st],
        out_shape=[jax.ShapeDtypeStruct((B, L, H_B * DV_B), F32), jax.ShapeDtypeStruct(s_all.shape, F32)],
        input_output_aliases=aliases,
        compiler_params=_cp(("arbitrary", "arbitrary", "arbitrary")),
        name="hgrn2_chunked",
    )(*operands)


def _mlstm_step(q, k, v, op, gt, h, nw, Cst, n, m, *, C, first):
    lane = _iota((C, LANES), 1)
    live = _live_rows(C, first, 1)
    i0 = jnp.sum(jnp.where(live & (lane == h), gt, 0.0), keepdims=True)
    lf0 = _log_sigmoid(jnp.sum(jnp.where(live & (lane == h + H_C), gt, 0.0), keepdims=True))
    inter = lf0 + m
    m_t = jnp.maximum(inter, i0)
    w_inter = jnp.exp(inter - m_t)
    wk = jnp.exp(i0 - m_t)
    kh = k * (DK_C ** -0.5)
    s = jnp.sum(q * kh, axis=1, keepdims=True) * wk
    qc = _dot(q, Cst)
    kw = jnp.where(live, kh * wk, 0.0)
    kv = _dot_tn(kw, v)
    yield
    num = w_inter * qc + s * v
    den = w_inter * jnp.sum(q * n, axis=1, keepdims=True) + s
    hh = num / jnp.maximum(jnp.abs(den), jnp.exp(-m_t))
    C_new = w_inter * Cst + kv
    n_new = w_inter * n + jnp.sum(kw, axis=0, keepdims=True)
    y = _rms(hh) * nw * _sigmoid(op)
    return y, C_new, n_new, m_t


def _mlstm_chunk(q, k, v, op, gt, h, nw, Cst, n, m, *, C, valid, first=0):
    if valid == 1:
        return (yield from _mlstm_step(q, k, v, op, gt, h, nw, Cst, n, m, C=C, first=first))
    assert first == 0
    gtT = gt.T
    lane = _iota((C, LANES), 1)
    sub = _iota((LANES, C), 0)
    i_col = jnp.sum(jnp.where(lane == h, gt, 0.0), axis=1, keepdims=True)
    f_col = jnp.sum(jnp.where(lane == h + H_C, gt, 0.0), axis=1, keepdims=True)
    i_row = jnp.sum(jnp.where(sub == h, gtT, 0.0), axis=0, keepdims=True)
    f_row = jnp.sum(jnp.where(sub == h + H_C, gtT, 0.0), axis=0, keepdims=True)
    lf_col = _log_sigmoid(f_col)
    lf_row = _log_sigmoid(f_row)
    if valid < C:
        live_c = _iota((C, 1), 0) < valid
        live_r = _iota((1, C), 1) < valid
        i_col = jnp.where(live_c, i_col, NEG_BIG)
        i_row = jnp.where(live_r, i_row, NEG_BIG)
        lf_col = jnp.where(live_c, lf_col, 0.0)
        lf_row = jnp.where(live_r, lf_row, 0.0)
    tril = _iota((C, C), 0) >= _iota((C, C), 1)
    b_col = jnp.sum(jnp.where(tril, lf_row, 0.0), axis=1, keepdims=True)
    b_row = jnp.sum(jnp.where(_iota((C, C), 0) <= _iota((C, C), 1), lf_col, 0.0), axis=0, keepdims=True)
    dm = jnp.where(tril, b_col - b_row + i_row, NEG_BIG)
    inter = b_col + m
    m_t = jnp.maximum(inter, jnp.max(dm, axis=1, keepdims=True))
    w_inter = jnp.exp(inter - m_t)
    kh = k * (DK_C ** -0.5)
    qk = _dot_nt(q, kh)
    qc = _dot(q, Cst)
    g = b_col[C - 1:C, :]
    m_new = m_t[C - 1:C, :]
    wk = jnp.exp(g - b_col + i_col - m_new)
    f_state = jnp.exp(g + m - m_new)
    kw = kh * wk
    kv = _dot_tn(kw, v)
    yield
    s = qk * jnp.exp(dm - m_t)
    sv = _dot(s, v)
    yield
    num = w_inter * qc + sv
    den = w_inter * jnp.sum(q * n, axis=1, keepdims=True) + jnp.sum(s, axis=1, keepdims=True)
    hh = num / jnp.maximum(jnp.abs(den), jnp.exp(-m_t))
    C_new = f_state * Cst + kv
    n_new = f_state * n + jnp.sum(kw, axis=0, keepdims=True)
    y = _rms(hh) * nw * _sigmoid(op)
    return y, C_new, n_new, m_new


def _mlstm_body(q_ref, k_ref, v_ref, o_ref, g_ref, gb_ref, nw_ref, c0_ref, n0_ref, m0_ref, *rest,
                C, valid, nb, decode, nchunk):
    y_ref, c1_ref, n1_ref, m1_ref = rest[-4:]

    @pl.when(pl.program_id(2) == 0)
    def _():
        c1_ref[...] = c0_ref[...]
        n1_ref[...] = n0_ref[...]
        m1_ref[...] = m0_ref[...]

    h = pl.program_id(1)
    gb = gb_ref[...]
    nw = nw_ref[...]

    def chunk(c, carry):
        rows = pl.ds(pl.multiple_of(c * C, C), C)
        gens = []
        for bi in range(nb):
            src = 0 if decode else bi
            gens.append(_mlstm_chunk(
                q_ref[src, rows, :], k_ref[src, rows, :], v_ref[src, rows, :], o_ref[src, rows, :],
                g_ref[src, rows, :] + gb, h, nw, c1_ref[bi], n1_ref[bi], m1_ref[bi][:, 0:1],
                C=C, valid=valid, first=bi if decode else 0))
        outs = _interleave(gens)
        for bi, (y, c_new, n_new, m_new) in enumerate(outs):
            if not decode:
                y_ref[bi, rows, :] = y
            c1_ref[bi] = c_new
            n1_ref[bi] = n_new
            m1_ref[bi] = jnp.broadcast_to(m_new, (1, LANES))
        if decode:
            y_ref[0, rows, :] = _merge_rows([o[0] for o in outs], C)
        return carry

    lax.fori_loop(0, nchunk, chunk, 0)


def _mlstm(z, gb, nw, states, prev, j, *, C, valid, nb, tt, decode):
    B, L, _ = z.shape
    zb = 1 if decode else nb
    c128 = lambda base: pl.BlockSpec((zb, tt, LANES), lambda b, h, i: (b, i, base + h))
    c256 = lambda base: pl.BlockSpec((zb, tt, DV_C), lambda b, h, i: (b, i, base + h))
    cst = pl.BlockSpec((None, nb, None, DK_C, DV_C), lambda b, h, i: (j, b, h, 0, 0))
    vec = pl.BlockSpec((None, nb, None, 1, LANES), lambda b, h, i: (j, b, h, 0, 0))
    operands = [z, z, z, z, z, gb, nw, *states]
    specs = [c128(0), c128(4), c256(4), c256(8), pl.BlockSpec((zb, tt, LANES), lambda b, h, i: (b, i, 24)),
             pl.BlockSpec((1, LANES), lambda b, h, i: (0, 0)), pl.BlockSpec((1, DV_C), lambda b, h, i: (0, 0)),
             cst, vec, vec]
    aliases = {}
    for k, p in enumerate(prev or ()):
        aliases.update(_chained(p, 1 + k, operands, specs))
    outs = pl.pallas_call(
        functools.partial(_mlstm_body, C=C, valid=valid, nb=nb, decode=decode, nchunk=tt // C),
        grid=(B // zb, H_C, L // tt),
        in_specs=specs,
        out_specs=[c256(0), cst, vec, vec],
        out_shape=[jax.ShapeDtypeStruct((B, L, H_C * DV_C), F32)] + [jax.ShapeDtypeStruct(s.shape, F32) for s in states],
        input_output_aliases=aliases,
        compiler_params=_cp(("arbitrary", "arbitrary", "arbitrary")),
        name="mlstm_chunked",
    )(*operands)
    return outs[0], tuple(outs[1:])


def _head_sum(x, bd):
    return jnp.concatenate([_dot_sel(x[:, n * LANES:(n + 1) * LANES], bd) for n in range(W_D // LANES)], axis=1)


def _rwkv_pre(zd, sh, mu, w0, a0, kkw, ka, rk, w2p, a2p, g2, bd):
    zs = zd + (sh - zd) * mu
    r, k, v = zs[:, 0:W_D], zs[:, W_D:2 * W_D], zs[:, 2 * W_D:3 * W_D]
    wa = zs[:, 3 * W_D:3 * W_D + LANES]
    gl = zs[:, 3 * W_D + LANES:]
    w = -_softplus(-(w0 + _dot(jnp.tanh(wa), w2p))) - 0.5
    logd = -jnp.exp(w)
    a = _sigmoid(a0 + _dot(wa, a2p))
    gate = _dot(_sigmoid(gl), g2)
    kk = k * kkw
    kk = kk / jnp.maximum(jnp.sqrt(_head_sum(kk * kk, bd)), 1e-12)
    kmod = k * (1.0 + (a - 1.0) * ka)
    bonus = _head_sum(r * kmod * rk, bd) * v
    return r, logd, kmod, v, kk, kk * a, gate, bonus


def _head_block_ones():
    return jnp.where((_iota((LANES, LANES), 0) >> 6) == (_iota((LANES, LANES), 1) >> 6), 1.0, 0.0).astype(BF16)


def _rwkv_pre_seq_body(z_ref, halo_ref, prev_ref, mu_ref, w0_ref, a0_ref, kk_ref, ka_ref, rk_ref, w2_ref, a2_ref,
                       g2_ref, *out_refs, tt):
    i = pl.program_id(1)
    halo = jnp.where(i == 0, prev_ref[...], halo_ref[...])
    zd = z_ref[...]
    sh = pltpu.roll(jnp.concatenate([halo, zd], axis=0), 1, axis=0)[8:8 + tt]
    outs = _rwkv_pre(zd, sh, mu_ref[...], w0_ref[...], a0_ref[...], kk_ref[...], ka_ref[...], rk_ref[...],
                     w2_ref[...], a2_ref[...], g2_ref[...], _head_block_ones())
    for ref, val in zip(out_refs, outs):
        ref[...] = val


def _rwkv_pre_step_body(z_ref, sh_ref, mu_ref, w0_ref, a0_ref, kk_ref, ka_ref, rk_ref, w2_ref, a2_ref, g2_ref,
                        *out_refs):
    outs = _rwkv_pre(z_ref[...], sh_ref[...], mu_ref[...], w0_ref[...], a0_ref[...], kk_ref[...], ka_ref[...],
                     rk_ref[...], w2_ref[...], a2_ref[...], g2_ref[...], _head_block_ones())
    for ref, val in zip(out_refs, outs):
        ref[...] = val


def _rwkv_pre_param_specs():
    z2 = lambda *_: (0, 0)
    return ([pl.BlockSpec((1, P_D), z2)] + [pl.BlockSpec((1, W_D), z2)] * 5
            + [pl.BlockSpec((LANES, W_D), z2)] * 3)


def _rwkv_pre_seq(z, prev8, prm, tt):
    B, L, _ = z.shape
    hb = tt // 8
    out = jax.ShapeDtypeStruct((B, L, W_D), F32)
    return pl.pallas_call(
        functools.partial(_rwkv_pre_seq_body, tt=tt),
        grid=(B, L // tt),
        in_specs=[pl.BlockSpec((None, tt, P_D), lambda b, i: (b, i, 0)),
                  pl.BlockSpec((None, 8, P_D), lambda b, i: (b, jnp.maximum(i * hb - 1, 0), 0)),
                  pl.BlockSpec((None, 8, P_D), lambda b, i: (b, 0, 0))] + _rwkv_pre_param_specs(),
        out_specs=[pl.BlockSpec((None, tt, W_D), lambda b, i: (b, i, 0))] * 8,
        out_shape=[out] * 8,
        compiler_params=_cp(("arbitrary", "arbitrary")),
        name="rwkv7_token_shift_features",
    )(z, z, prev8, *prm)


def _rwkv_pre_step(z, shifted, prm):
    n = z.shape[0]
    out = jax.ShapeDtypeStruct((n, W_D), F32)
    return pl.pallas_call(
        _rwkv_pre_step_body,
        grid=(1,),
        in_specs=[pl.BlockSpec((n, P_D), lambda i: (0, 0)), pl.BlockSpec((n, P_D), lambda i: (0, 0))]
        + _rwkv_pre_param_specs(),
        out_specs=[pl.BlockSpec((n, W_D), lambda i: (0, 0))] * 8,
        out_shape=[out] * 8,
        compiler_params=_cp(("arbitrary",)),
        name="rwkv7_decode_features",
    )(z, shifted, *prm)


def _unit_lower_inverse(N, C):
    n = N.shape[0]
    ri, ci = _iota((n, n), 0), _iota((n, n), 1)
    base = min(8, C)
    kb = _log2(base)
    X = jnp.where((ri >> kb) == (ci >> kb), N, 0.0)
    T = jnp.where(ri == ci, 1.0, 0.0) + X
    for _ in range(kb - 1):
        X = _dot(X, X)
        yield
        T = T + _dot(T, X)
        yield
    size = base
    while size < C:
        ks = _log2(size)
        sel = ((ri >> (ks + 1)) == (ci >> (ks + 1))) & (((ri >> ks) & 1) == 1) & (((ci >> ks) & 1) == 0)
        TN_ = _dot(T, jnp.where(sel, N, 0.0))
        yield
        T = T + _dot(TN_, T)
        yield
        size *= 2
    return T


def _interleave(gens):
    results = [None] * len(gens)
    live = list(range(len(gens)))
    while live:
        for i in list(live):
            try:
                next(gens[i])
            except StopIteration as stop:
                results[i] = stop.value
                live.remove(i)
    return results


def _rwkv_groupnorm_gate(y, gate, bonus, lnw, lnb):
    bd = _head_block_ones()
    mean = _dot_sel(y, bd) * (1.0 / N_D)
    yield
    dy = y - mean
    var = _dot_sel(dy * dy, bd) * (1.0 / N_D)
    yield
    return (dy * lax.rsqrt(var + GN_EPS_D) * lnw + lnb + bonus) * gate


def _rwkv_step(r, ld, k, v, kk, be, gate, bonus, lnw, lnb, S2, *, C, first):
    live = _live_rows(C, first, 1)
    zero = lambda x: jnp.where(live, x, 0.0)
    same_head = (_iota((LANES, LANES), 0) >> 6) == (_iota((LANES, LANES), 1) >> 6)
    spread = jnp.where(_iota((N_D, LANES), 0) == (_iota((N_D, LANES), 1) & (N_D - 1)), 1.0, 0.0).astype(BF16)
    fold = jnp.where((_iota((LANES, N_D), 0) & (N_D - 1)) == _iota((LANES, N_D), 1), 1.0, 0.0).astype(BF16)
    Pn = jnp.where(same_head, _dot_sel(S2, spread), 0.0)
    yield
    d_row = jnp.exp(jnp.sum(zero(ld), axis=0, keepdims=True))
    U = _dot_nt(zero(-kk), Pn)
    yield
    upd = _dot_tn(jnp.concatenate([U, zero(v)], axis=0), jnp.concatenate([zero(be), zero(k)], axis=0))
    yield
    P_new = Pn * d_row + jnp.where(same_head, upd, 0.0)
    y = _dot_nt(r, P_new)
    S2_new = _dot_sel(P_new, fold)
    yield
    out = yield from _rwkv_groupnorm_gate(y, gate, bonus, lnw, lnb)
    return out, S2_new


def _rwkv_chunk(r, ld, k, v, kk, be, gate, bonus, lnw, lnb, P, *, C, valid, first=0):
    if valid < C:
        live = _live_rows(C, first, valid)
        ld = jnp.where(live, ld, 0.0)
        be = jnp.where(live, be, 0.0)
        k = jnp.where(live, k, 0.0)
        v = jnp.where(live, v, 0.0)
    tril = _iota((C, C), 0) >= _iota((C, C), 1)
    c = _sel_dot(jnp.where(tril, 1.0, 0.0).astype(BF16), ld)
    g_col = _dot_sel_tn(ld, jnp.ones((C, LANES), BF16))
    yield
    ec, enc = jnp.exp(c), jnp.exp(-c)
    lane0 = _iota((C, LANES), 1) < N_D

    def stack(x):
        return jnp.concatenate([jnp.where(lane0, x, 0.0), jnp.where(lane0, 0.0, x)], axis=0)

    n2 = 2 * C
    AR = jnp.concatenate([stack(-kk * jnp.exp(c - ld)), stack(r * ec)], axis=0).astype(BF16)
    BK = jnp.concatenate([stack(be * enc), stack(k * enc)], axis=0).astype(BF16)
    V2 = stack(v).astype(BF16)
    kc = _log2(C)
    ri, ci = _iota((2 * n2, 2 * n2), 0), _iota((2 * n2, 2 * n2), 1)
    tpos, spos = ri & (C - 1), ci & (C - 1)
    keep = (((ri >> kc) & 1) == ((ci >> kc) & 1)) & ((spos < tpos) | ((ri >= n2) & (spos == tpos)))
    G = jnp.where(keep, _dg(AR, BK, _NT), 0.0)
    LP = _mm(AR, P.astype(BF16))
    yield
    GV = _mm(G[:, n2:].astype(BF16), V2)
    T = yield from _unit_lower_inverse(G[0:n2, 0:n2], C)
    U = _dot(T, LP[0:n2] + GV[0:n2])
    yield
    Y2 = LP[n2:] + GV[n2:] + _dot(G[n2:, 0:n2], U)
    P_new = jnp.exp(g_col) * (P + _dg(BK, jnp.concatenate([U.astype(BF16), V2], axis=0), _TN))
    yield
    y = Y2[0:C] + Y2[C:n2]
    out = yield from _rwkv_groupnorm_gate(y, gate, bonus, lnw, lnb)
    return out, P_new


def _rwkv_body(r_ref, ld_ref, k_ref, v_ref, kk_ref, be_ref, gt_ref, bo_ref, lnw_ref, lnb_ref, p0_ref,
               y_ref, p1_ref, *, C, valid, nb, hp, nchunk):
    @pl.when(pl.program_id(2) == 0)
    def _():
        p1_ref[...] = p0_ref[...]

    probs = [(bi, pp) for bi in range(nb) for pp in range(hp)]
    in_refs = (r_ref, ld_ref, k_ref, v_ref, kk_ref, be_ref, gt_ref, bo_ref)

    def chunk(c, carry):
        rows = pl.ds(pl.multiple_of(c * C, C), C)
        gens = []
        for bi, pp in probs:
            lanes = slice(pp * LANES, (pp + 1) * LANES)
            args = [ref[bi, rows, lanes] for ref in in_refs] + [lnw_ref[:, lanes], lnb_ref[:, lanes], p1_ref[bi, pp]]
            gens.append(_rwkv_chunk(*args, C=C, valid=valid))
        for (bi, pp), (y, p_new) in zip(probs, _interleave(gens)):
            p1_ref[bi, pp] = p_new
            y_ref[bi, rows, pp * LANES:(pp + 1) * LANES] = y
        return carry

    lax.fori_loop(0, nchunk, chunk, 0)


def _rwkv_decode_body(r_ref, ld_ref, k_ref, v_ref, kk_ref, be_ref, gt_ref, bo_ref, lnw_ref, lnb_ref, s0_ref, *rest,
                      nb):
    y_ref, s1_ref = rest[-2:]
    tiles = [ref[0] for ref in (r_ref, ld_ref, k_ref, v_ref, kk_ref, be_ref, gt_ref, bo_ref)]
    gens = [_rwkv_step(*tiles, lnw_ref[...], lnb_ref[...], jnp.concatenate([s0_ref[bi, 0], s0_ref[bi, 1]], axis=0),
                       C=nb, first=bi) for bi in range(nb)]
    outs = _interleave(gens)
    for bi, (_, s2) in enumerate(outs):
        s1_ref[bi, 0] = s2[0:N_D]
        s1_ref[bi, 1] = s2[N_D:]
    y_ref[0] = _merge_rows([o[0] for o in outs], nb)


def _rwkv_decode(feats, lnw, lnb, s_all, s_prev, j, *, nb):
    B = feats[0].shape[0]
    col = pl.BlockSpec((1, nb, LANES), lambda b, p: (b, 0, p))
    vec = pl.BlockSpec((1, LANES), lambda b, p: (0, p))
    st = pl.BlockSpec((None, nb, 2, N_D, N_D), lambda b, p: (j, b, p, 0, 0))
    operands = [*feats, lnw, lnb, s_all]
    specs = [col] * 8 + [vec, vec, st]
    aliases = _chained(s_prev, 1, operands, specs)
    return pl.pallas_call(
        functools.partial(_rwkv_decode_body, nb=nb),
        grid=(B, H_D // 2),
        in_specs=specs,
        out_specs=[col, st],
        out_shape=[jax.ShapeDtypeStruct((B, nb, W_D), F32), jax.ShapeDtypeStruct(s_all.shape, F32)],
        input_output_aliases=aliases,
        compiler_params=_cp(("arbitrary", "arbitrary")),
        name="rwkv7_decode_step",
    )(*operands)


def _rwkv(feats, lnw, lnb, p0, *, C, valid, nb, hp, tt):
    B, L, _ = feats[0].shape
    w = hp * LANES
    col = pl.BlockSpec((nb, tt, w), lambda b, p, i: (b, i, p))
    vec = pl.BlockSpec((1, w), lambda b, p, i: (0, p))
    st = pl.BlockSpec((nb, hp, LANES, LANES), lambda b, p, i: (b, p, 0, 0))
    return pl.pallas_call(
        functools.partial(_rwkv_body, C=C, valid=valid, nb=nb, hp=hp, nchunk=tt // C),
        grid=(B // nb, H_D // 2 // hp, L // tt),
        in_specs=[col] * 8 + [vec, vec, st],
        out_specs=[col, st],
        out_shape=[jax.ShapeDtypeStruct((B, L, W_D), F32), jax.ShapeDtypeStruct(p0.shape, F32)],
        compiler_params=_cp(("arbitrary", "arbitrary", "arbitrary")),
        name="rwkv7_chunked",
    )(*feats, lnw, lnb, p0)


def _pair_blockdiag(s):
    B = s.shape[0]
    st = s.reshape(B, H_D // 2, 2, N_D, N_D)
    zero = jnp.zeros_like(st[:, :, 0])
    top = jnp.concatenate([st[:, :, 0], zero], axis=-1)
    bot = jnp.concatenate([zero, st[:, :, 1]], axis=-1)
    return jnp.concatenate([top, bot], axis=-2)


def _pair_unblockdiag(p):
    B = p.shape[0]
    st = jnp.stack([p[:, :, :N_D, :N_D], p[:, :, N_D:, N_D:]], axis=2)
    return st.reshape(B, H_D, N_D, N_D)


def _trunk(x, mod, st, prm, *, decode):
    G, Lg, _ = x.shape
    tm = min(128 if decode else 512, Lg)
    tm_in = tm if decode or Lg % 1024 else 1024
    assert decode or Lg % 512 == 0
    nseq = Lg if decode else G
    new = {k: [] for k in ("a_conv", "a_h", "d_shift", "d_s")}
    b_s = None
    c_in = (st["c_c"], st["c_n"][:, :, :, None, :],
            jnp.broadcast_to(st["c_m"][..., None, None], st["c_m"].shape + (1, LANES)))
    c_out = d_s = None
    blocks = lambda t: t.reshape(nseq // DEC_NB, DEC_NB, -1)
    for l in range(DEPTH):
        j = l // 2
        x = _ffn(x, mod, prm["wg"], prm["wu"], prm["wd"], l, 0, tm)
        if l % 2 == 0:
            z = _inproj(x, mod, prm["w_in_even"], l, j, tm_in)
            a_prm = prm["rglru"][j]
            conv0, h0 = st["a_conv"][j], st["a_h"][j]
            if decode:
                z2 = z[0]
                y1, h1 = _rglru_step(z2, conv0, h0, a_prm)
                conv1 = jnp.concatenate([conv0[:, 1:], z2[:, None, :W_A]], axis=1)
                yb, b_s = _hgrn2(blocks(z2), prm["lb"][j], prm["b_norm_w"][j], st["b_s"], b_s, j,
                                 C=DEC_NB, valid=1, nb=DEC_NB, hp=1, tt=DEC_NB, decode=True)
                y1, y2 = y1[None], yb.reshape(1, nseq, -1)
            else:
                buf8 = jnp.concatenate([jnp.zeros((nseq, 8 - (CONV_W - 1), W_A), F32), conv0], axis=1)
                a, b = _rglru_seq(z, buf8, a_prm, 512)
                y1, h1 = _rglru_scan(a, b, z, h0, 512)
                conv1 = z[:, Lg - (CONV_W - 1):, :W_A]
                y2, b_s = _hgrn2(z, prm["lb"][j], prm["b_norm_w"][j], st["b_s"], b_s, j,
                                 C=CHUNK, valid=CHUNK, nb=nseq, hp=2, tt=512, decode=False)
            new["a_conv"].append(conv1)
            new["a_h"].append(h1)
            x = _outproj(x, mod, y1, y2, prm["w_out_even"], l, j, tm)
        else:
            z = _inproj(x, mod, prm["w_in_odd"], l, j, tm_in, width=MLSTM_COLS)
            zd = _inproj(x, mod, prm["w_in_rwkv"], l, j, tm, tn=P_D // 2)
            d_prm = prm["rwkv"][j]
            if decode:
                yc, c_out = _mlstm(blocks(z[0]), prm["c_gate_b"][j], prm["c_norm_w"][j], c_in, c_out, j,
                                   C=DEC_NB, valid=1, nb=DEC_NB, tt=DEC_NB, decode=True)
                feats = [blocks(t) for t in _rwkv_pre_step(zd[0], st["d_shift"][j], d_prm)]
                yd, d_s = _rwkv_decode(feats, prm["d_ln_w"][j], prm["d_ln_b"][j], st["d_s"], d_s, j, nb=DEC_NB)
                y1, y2 = yc.reshape(1, nseq, -1), yd.reshape(1, nseq, -1)
                shift1 = zd[0]
            else:
                y1, c_out = _mlstm(z, prm["c_gate_b"][j], prm["c_norm_w"][j], c_in, c_out, j,
                                   C=CHUNK, valid=CHUNK, nb=nseq, tt=256, decode=False)
                prev8 = jnp.concatenate([jnp.zeros((nseq, 7, P_D), F32), st["d_shift"][j][:, None]], axis=1)
                feats = _rwkv_pre_seq(zd, prev8, d_prm, 128)
                p0 = _pair_blockdiag(jnp.swapaxes(st["d_s"][j], -1, -2))
                y2, p1 = _rwkv(feats, prm["d_ln_w"][j], prm["d_ln_b"][j], p0, C=CHUNK, valid=CHUNK, nb=nseq, hp=2,
                               tt=256)
                shift1 = zd[:, -1]
                new["d_s"].append(jnp.swapaxes(_pair_unblockdiag(p1), -1, -2))
            new["d_shift"].append(shift1)
            x = _outproj(x, mod, y1, y2, prm["w_out_odd"], l, j, tm)
        x = _ffn(x, mod, prm["wg"], prm["wu"], prm["wd"], l, 1, tm)
    y = _final_norm(x, prm["final_norm_w"], tm)
    if not decode:
        d_s = jnp.stack(new["d_s"])
    return (y, jnp.stack(new["a_conv"]), jnp.stack(new["a_h"]), b_s, c_out[0], c_out[1][:, :, :, 0, :],
            c_out[2][:, :, :, 0, 0], jnp.stack(new["d_shift"]), d_s)


def _prepare(w_mod, b_mod, ffn_w_gate, ffn_w_up, ffn_w_down, w_in_even, w_out_even, a_conv_w, a_conv_b, a_gate_r_w,
             a_gate_r_b, a_gate_i_w, a_gate_i_b, a_lambda, b_lb_gamma, b_norm_w, w_in_odd, w_out_odd, c_igate_b,
             c_fgate_b, c_norm_w, d_mu, d_w0, d_w2, d_a0, d_a2, d_g2, d_k_k, d_k_a, d_r_k, d_ln_w, d_ln_b,
             final_norm_w):
    n_gate = 2 * H_C
    row = lambda t: t[:, None, :]
    zpad = jnp.zeros((N_ODD, LANES - n_gate), F32)
    half = jnp.zeros((N_ODD, R_W, W_D), F32)
    rglru = [(a_conv_w[j], a_conv_b[j][None], a_gate_r_w[j].astype(BF16), a_gate_r_b[j][None],
              a_gate_i_w[j].astype(BF16), a_gate_i_b[j][None], a_lambda[j][None]) for j in range(N_EVEN)]
    w2p = jnp.concatenate([d_w2, half], axis=1).astype(BF16)
    a2p = jnp.concatenate([half, d_a2], axis=1).astype(BF16)
    g2 = d_g2.astype(BF16)
    rwkv = [(d_mu[j][None], d_w0[j][None], d_a0[j][None], d_k_k[j][None], d_k_a[j][None],
             d_r_k[j].reshape(1, W_D), w2p[j], a2p[j], g2[j]) for j in range(N_ODD)]
    return dict(
        wg=ffn_w_gate.astype(BF16), wu=ffn_w_up.astype(BF16), wd=ffn_w_down.astype(BF16),
        w_in_even=w_in_even.astype(BF16), w_in_odd=w_in_odd.astype(BF16),
        w_in_rwkv=w_in_odd[:, :, P_ODD - P_D:].astype(BF16),
        w_out_even=w_out_even.astype(BF16).reshape(N_EVEN, 2, D // 2, D),
        w_out_odd=w_out_odd.astype(BF16).reshape(N_ODD, 2, D // 2, D),
        rglru=rglru, rwkv=rwkv,
        lb=row(_lower_bounds(b_lb_gamma)), b_norm_w=row(b_norm_w),
        c_gate_b=row(jnp.concatenate([c_igate_b, c_fgate_b, zpad], axis=-1)), c_norm_w=row(c_norm_w),
        d_ln_w=row(d_ln_w), d_ln_b=row(d_ln_b), final_norm_w=final_norm_w)


def kernel(x_prompt, x_sample, c_prompt, c_sample, state_a_conv, state_a_h, state_b_s, state_c_c, state_c_n, state_c_m, state_d_shift, state_d_s, w_mod, b_mod, ffn_w_gate, ffn_w_up, ffn_w_down, w_in_even, w_out_even, a_conv_w, a_conv_b, a_gate_r_w, a_gate_r_b, a_gate_i_w, a_gate_i_b, a_lambda, b_lb_gamma, b_norm_w, w_in_odd, w_out_odd, c_igate_b, c_fgate_b, c_norm_w, d_mu, d_w0, d_w2, d_a0, d_a2, d_g2, d_k_k, d_k_a, d_r_k, d_ln_w, d_ln_b, final_norm_w):
    prm = _prepare(w_mod, b_mod, ffn_w_gate, ffn_w_up, ffn_w_down, w_in_even, w_out_even, a_conv_w, a_conv_b,
                   a_gate_r_w, a_gate_r_b, a_gate_i_w, a_gate_i_b, a_lambda, b_lb_gamma, b_norm_w, w_in_odd,
                   w_out_odd, c_igate_b, c_fgate_b, c_norm_w, d_mu, d_w0, d_w2, d_a0, d_a2, d_g2, d_k_k, d_k_a,
                   d_r_k, d_ln_w, d_ln_b, final_norm_w)
    bp, lp, _ = x_prompt.shape
    bs = x_sample.shape[0]
    n_rows = -(-(bs + bp) // 8) * 8
    c_all = jnp.concatenate([c_sample, c_prompt, jnp.zeros((n_rows - bs - bp, D), F32)], axis=0)
    mod_all = _modulation(c_all, w_mod, b_mod)
    mod_s = mod_all[:, None]
    mod_p = mod_all[:, bs:bs + bp, None]

    zeros = lambda *s: jnp.zeros(s, F32)
    st_p = dict(a_conv=zeros(N_EVEN, bp, CONV_W - 1, W_A), a_h=zeros(N_EVEN, bp, W_A),
                b_s=zeros(N_EVEN, bp, H_B, DK_B, DV_B), c_c=zeros(N_ODD, bp, H_C, DK_C, DV_C),
                c_n=zeros(N_ODD, bp, H_C, DK_C), c_m=zeros(N_ODD, bp, H_C), d_shift=zeros(N_ODD, bp, P_D),
                d_s=zeros(N_ODD, bp, H_D, N_D, N_D))
    st_s = dict(a_conv=state_a_conv, a_h=state_a_h, b_s=state_b_s, c_c=state_c_c, c_n=state_c_n, c_m=state_c_m,
                d_shift=state_d_shift, d_s=state_d_s)
    out_p = _trunk(x_prompt, mod_p, st_p, prm, decode=False)
    out_s = _trunk(x_sample.reshape(1, bs, D), mod_s, st_s, prm, decode=True)
    y_s = out_s[0].reshape(bs, 1, D)
    return (out_p[0], y_s) + out_p[1:] + out_s[1:]
```

```python
import functools

import jax
import jax.numpy as jnp
from jax import lax
from jax.experimental import pallas as pl
from jax.experimental.pallas import tpu as pltpu

F32 = jnp.float32
BF16 = jnp.bfloat16

D = 2048
DEPTH = 4
N_EVEN = 2
N_ODD = 2
D_FF = 5632
N_MOD = 9
EPS = 1e-6
CHUNK = 64
NEG_BIG = -1e30

W_A = 1024
NB_A = 8
BS_A = 128
CONV_W = 4
C_RGLRU = 8.0
H_B = 8
DK_B = 128
DV_B = 128
H_C = 4
DK_C = 128
DV_C = 256
N_D = 64
H_D = 16
W_D = 1024
R_W = 64
R_A = 64
R_G = 128
GN_EPS_D = 64e-5
P_EVEN = 6144
P_D = 3328
P_ODD = 6408
MLSTM_COLS = 3584
DEC_NB = 8
LANES = 128
VMEM_LIMIT = 48 * 2**20


def _cp(sem, vmem=VMEM_LIMIT):
    return pltpu.CompilerParams(dimension_semantics=sem, vmem_limit_bytes=vmem)


def _mm(a, b):
    return jnp.dot(a, b, preferred_element_type=F32)


def _dot(a, b):
    return _mm(a.astype(BF16), b.astype(BF16))


_NT = (((1,), (1,)), ((), ()))
_TN = (((0,), (0,)), ((), ()))
_NN = (((1,), (0,)), ((), ()))


def _dg(a, b, dn):
    return lax.dot_general(a, b, dn, preferred_element_type=F32)


def _dot_nt(a, b):
    return _dg(a.astype(BF16), b.astype(BF16), _NT)


def _dot_tn(a, b):
    return _dg(a.astype(BF16), b.astype(BF16), _TN)


def _split3(x):
    hi = x.astype(BF16)
    r1 = x - hi.astype(F32)
    mid = r1.astype(BF16)
    lo = (r1 - mid.astype(F32)).astype(BF16)
    return hi, mid, lo


def _split2(x):
    hi = x.astype(BF16)
    lo = (x - hi.astype(F32)).astype(BF16)
    return hi, lo


def _sel_dot(mask_bf, x):
    hi, mid, lo = _split3(x)
    return _mm(mask_bf, hi) + _mm(mask_bf, mid) + _mm(mask_bf, lo)


def _dot_sel(x, mask_bf):
    hi, mid, lo = _split3(x)
    return _mm(hi, mask_bf) + _mm(mid, mask_bf) + _mm(lo, mask_bf)


def _dot_sel_tn(x, mask_bf):
    hi, mid, lo = _split3(x)
    return _dg(hi, mask_bf, _TN) + _dg(mid, mask_bf, _TN) + _dg(lo, mask_bf, _TN)


def _mmx(a, b, dn=_NN):
    a_hi, a_lo = _split2(a)
    b_hi, b_lo = _split2(b)
    return _dg(a_hi, b_hi, dn) + _dg(a_hi, b_lo, dn) + _dg(a_lo, b_hi, dn)


def _iota(shape, dim):
    return lax.broadcasted_iota(jnp.int32, shape, dim)


def _sigmoid(x):
    return jax.nn.sigmoid(x)


def _softplus(x):
    return jnp.maximum(x, 0.0) + jnp.log1p(jnp.exp(-jnp.abs(x)))


def _log_sigmoid(x):
    return jnp.minimum(x, 0.0) - jnp.log1p(jnp.exp(-jnp.abs(x)))


def _gelu_tanh(x):
    return x * (0.5 * (1.0 + jnp.tanh(0.7978845608028654 * (x + 0.044715 * (x * x * x)))))


def _rms(x):
    return x * lax.rsqrt(jnp.mean(x * x, axis=-1, keepdims=True) + EPS)


def _adaln(x, shift, scale):
    return _rms(x) * (1.0 + scale) + shift


def _log2(n):
    k = n.bit_length() - 1
    assert (1 << k) == n
    return k


def _mod_spec(rows, width, layer, col):
    return pl.BlockSpec((None, None, rows, width), lambda g, i, n: (layer, g, 0, col))


def _mod_body(c_ref, w_ref, b_ref, o_ref):
    c = c_ref[...]
    cs = (c * _sigmoid(c)).astype(BF16)
    o_ref[...] = _mm(cs, w_ref[...].astype(BF16)) + b_ref[...]


def _modulation(c_all, w_mod, b_mod):
    rows = c_all.shape[0]
    tn = 1024
    return pl.pallas_call(
        _mod_body,
        grid=(DEPTH, N_MOD * D // tn),
        in_specs=[pl.BlockSpec((rows, D), lambda l, n: (0, 0)),
                  pl.BlockSpec((None, D, tn), lambda l, n: (l, 0, n)),
                  pl.BlockSpec((None, 1, tn), lambda l, n: (l, 0, n))],
        out_specs=pl.BlockSpec((None, rows, tn), lambda l, n: (l, 0, n)),
        out_shape=jax.ShapeDtypeStruct((DEPTH, rows, N_MOD * D), F32),
        compiler_params=_cp(("arbitrary", "arbitrary")),
        name="modulation",
    )(c_all, w_mod, b_mod.reshape(DEPTH, 1, N_MOD * D))


def _lower_bounds_body(g_ref, o_ref):
    g = g_ref[...]
    e = jnp.exp(g - jnp.max(g, axis=0, keepdims=True))
    sm = e / jnp.sum(e, axis=0, keepdims=True)
    acc = jnp.zeros_like(sm[0:1])
    for j in range(N_EVEN):
        acc = acc + sm[j:j + 1]
        o_ref[j:j + 1, :] = acc - sm[0:1]


def _lower_bounds(gamma):
    return pl.pallas_call(
        _lower_bounds_body,
        out_shape=jax.ShapeDtypeStruct(gamma.shape, F32),
        name="hgrn2_lower_bounds",
    )(gamma)


def _ffn_step(x_ref, sh_ref, sc_ref, gt_ref, wg, wu, wd, o_ref, hf_ref, acc_ref):
    f = pl.program_id(2)

    @pl.when(f == 0)
    def _():
        hf_ref[...] = _adaln(x_ref[...], sh_ref[...], sc_ref[...]).astype(BF16)
        acc_ref[...] = jnp.zeros_like(acc_ref)

    hf = hf_ref[...]
    g = _mm(hf, wg)
    u = _mm(hf, wu)
    act = (g * _sigmoid(g) * u).astype(BF16)
    acc_ref[...] += _mm(act, wd)

    @pl.when(f == pl.num_programs(2) - 1)
    def _():
        o_ref[...] = x_ref[...] + 0.5 * (1.0 + gt_ref[...]) * acc_ref[...]


def _ffn_body(x_ref, sh_ref, sc_ref, gt_ref, wg_ref, wu_ref, wd_ref, o_ref, hf_ref, acc_ref):
    _ffn_step(x_ref, sh_ref, sc_ref, gt_ref, wg_ref[...], wu_ref[...], wd_ref[...], o_ref, hf_ref, acc_ref)


def _ffn_cast_body(x_ref, sh_ref, sc_ref, gt_ref, wg_ref, wu_ref, wd_ref, o_ref, wgo_ref, wuo_ref, wdo_ref,
                   hf_ref, acc_ref):
    wg, wu, wd = wg_ref[...].astype(BF16), wu_ref[...].astype(BF16), wd_ref[...].astype(BF16)
    wgo_ref[...] = wg
    wuo_ref[...] = wu
    wdo_ref[...] = wd
    _ffn_step(x_ref, sh_ref, sc_ref, gt_ref, wg, wu, wd, o_ref, hf_ref, acc_ref)


FFN_TF = 512


def _ffn_specs(x, mod, layer, which, tm):
    R = 1 if mod.shape[2] == 1 else tm
    kb = 6 * which
    return [pl.BlockSpec((None, tm, D), lambda g, i, f: (g, i, 0)),
            _mod_spec(R, D, layer, kb), _mod_spec(R, D, layer, kb + 1), _mod_spec(R, D, layer, kb + 2)]


def _ffn(x, mod, w_bf16, layer, which, tm):
    G, Lg, _ = x.shape
    tf = FFN_TF
    return pl.pallas_call(
        _ffn_body,
        grid=(G, Lg // tm, D_FF // tf),
        in_specs=_ffn_specs(x, mod, layer, which, tm) + [
            pl.BlockSpec((D, tf), lambda g, i, f: (0, f)), pl.BlockSpec((D, tf), lambda g, i, f: (0, f)),
            pl.BlockSpec((tf, D), lambda g, i, f: (f, 0))],
        out_specs=pl.BlockSpec((None, tm, D), lambda g, i, f: (g, i, 0)),
        out_shape=jax.ShapeDtypeStruct(x.shape, F32),
        scratch_shapes=[pltpu.VMEM((tm, D), BF16), pltpu.VMEM((tm, D), F32)],
        compiler_params=_cp(("arbitrary", "arbitrary", "arbitrary")),
        name="adaln_swiglu_ffn",
    )(x, mod, mod, mod, *w_bf16)


def _ffn_cast(x, mod, wg, wu, wd, layer, which):
    G, tm, _ = x.shape
    assert G == 1
    tf = FFN_TF // 2
    bf = lambda s: jax.ShapeDtypeStruct(s, BF16)
    outs = pl.pallas_call(
        _ffn_cast_body,
        grid=(1, 1, D_FF // tf),
        in_specs=_ffn_specs(x, mod, layer, which, tm) + [
            pl.BlockSpec((None, None, D, tf), lambda g, i, f: (layer, which, 0, f)),
            pl.BlockSpec((None, None, D, tf), lambda g, i, f: (layer, which, 0, f)),
            pl.BlockSpec((None, None, tf, D), lambda g, i, f: (layer, which, f, 0))],
        out_specs=[pl.BlockSpec((None, tm, D), lambda g, i, f: (g, i, 0)),
                   pl.BlockSpec((D, tf), lambda g, i, f: (0, f)), pl.BlockSpec((D, tf), lambda g, i, f: (0, f)),
                   pl.BlockSpec((tf, D), lambda g, i, f: (f, 0))],
        out_shape=[jax.ShapeDtypeStruct(x.shape, F32), bf((D, D_FF)), bf((D, D_FF)), bf((D_FF, D))],
        scratch_shapes=[pltpu.VMEM((tm, D), BF16), pltpu.VMEM((tm, D), F32)],
        compiler_params=_cp(("arbitrary", "arbitrary", "arbitrary")),
        name="adaln_swiglu_ffn_weight_cast",
    )(x, mod, mod, mod, wg, wu, wd)
    return outs[0], tuple(outs[1:])


def _inproj_body(x_ref, sh_ref, sc_ref, w_ref, o_ref, hf_ref, *, w_is_out_by_in):
    @pl.when(pl.program_id(2) == 0)
    def _():
        hf_ref[...] = _adaln(x_ref[...], sh_ref[...], sc_ref[...]).astype(BF16)

    o_ref[...] = _dg(hf_ref[...], w_ref[...], _NT if w_is_out_by_in else _NN)


def _inproj(x, mod, w, layer, j, tm, width=None, tn=512, w_is_out_by_in=False):
    G, Lg, _ = x.shape
    R = 1 if mod.shape[2] == 1 else tm
    n_out = w.shape[1] if w_is_out_by_in else w.shape[2]
    P = n_out if width is None else width
    assert P % tn == 0 and P <= n_out
    w_spec = (pl.BlockSpec((None, tn, D), lambda g, i, n: (j, n, 0)) if w_is_out_by_in
              else pl.BlockSpec((None, D, tn), lambda g, i, n: (j, 0, n)))
    return pl.pallas_call(
        functools.partial(_inproj_body, w_is_out_by_in=w_is_out_by_in),
        grid=(G, Lg // tm, P // tn),
        in_specs=[pl.BlockSpec((None, tm, D), lambda g, i, n: (g, i, 0)),
                  _mod_spec(R, D, layer, 3), _mod_spec(R, D, layer, 4), w_spec],
        out_specs=pl.BlockSpec((None, tm, tn), lambda g, i, n: (g, i, n)),
        out_shape=jax.ShapeDtypeStruct((G, Lg, P), F32),
        scratch_shapes=[pltpu.VMEM((tm, D), BF16)],
        compiler_params=_cp(("arbitrary", "arbitrary", "arbitrary")),
        name="adaln_in_projection",
    )(x, mod, mod, w)


def _outproj_body(x_ref, gt_ref, y1_ref, y2_ref, w1_ref, w2_ref, o_ref):
    y = _dot(y1_ref[...], w1_ref[...]) + _dot(y2_ref[...], w2_ref[...])
    o_ref[...] = x_ref[...] + (1.0 + gt_ref[...]) * y


def _outproj(x, mod, y1, y2, w, layer, j, tm):
    G, Lg, _ = x.shape
    R = 1 if mod.shape[2] == 1 else tm
    tn = D
    nb = D // tn
    half = y1.shape[-1]
    return pl.pallas_call(
        _outproj_body,
        grid=(G, Lg // tm, nb),
        in_specs=[pl.BlockSpec((None, tm, tn), lambda g, i, n: (g, i, n)),
                  pl.BlockSpec((None, None, R, tn), lambda g, i, n: (layer, g, 0, 5 * nb + n)),
                  pl.BlockSpec((None, tm, half), lambda g, i, n: (g, i, 0)),
                  pl.BlockSpec((None, tm, half), lambda g, i, n: (g, i, 0)),
                  pl.BlockSpec((None, None, half, tn), lambda g, i, n: (j, 0, 0, n)),
                  pl.BlockSpec((None, None, half, tn), lambda g, i, n: (j, 1, 0, n))],
        out_specs=pl.BlockSpec((None, tm, tn), lambda g, i, n: (g, i, n)),
        out_shape=jax.ShapeDtypeStruct(x.shape, F32),
        compiler_params=_cp(("arbitrary", "arbitrary", "arbitrary")),
        name="out_projection_residual",
    )(x, mod, y1, y2, w, w)


def _final_body(x_ref, w_ref, o_ref):
    o_ref[...] = _rms(x_ref[...]) * w_ref[...]


def _final_norm(x, w, tm):
    G, Lg, _ = x.shape
    return pl.pallas_call(
        _final_body,
        grid=(G, Lg // tm),
        in_specs=[pl.BlockSpec((None, tm, D), lambda g, i: (g, i, 0)),
                  pl.BlockSpec((1, D), lambda g, i: (0, 0))],
        out_specs=pl.BlockSpec((None, tm, D), lambda g, i: (g, i, 0)),
        out_shape=jax.ShapeDtypeStruct(x.shape, F32),
        compiler_params=_cp(("arbitrary", "arbitrary")),
        name="final_rmsnorm",
    )(x, w.reshape(1, D))


def _rglru_gates(x0, x1, x2, x3, cw, cb, wr, br, wi, bi, lam):
    u = cb + x0 * cw[0:1] + x1 * cw[1:2] + x2 * cw[2:3] + x3 * cw[3:4]
    r_parts, i_parts = [], []
    for n in range(NB_A):
        un = u[:, n * BS_A:(n + 1) * BS_A].astype(BF16)
        r_parts.append(_mm(un, wr[n]))
        i_parts.append(_mm(un, wi[n]))
    r = _sigmoid(jnp.concatenate(r_parts, axis=1) + br)
    ig = _sigmoid(jnp.concatenate(i_parts, axis=1) + bi)
    log_a = -C_RGLRU * r * _softplus(-lam)
    a = jnp.exp(log_a)
    b = jnp.sqrt(-jnp.tanh(log_a) * (a * a + 1.0)) * (ig * u)
    return a, b


def _rglru_seq_body(x_ref, halo_ref, buf_ref, cw_ref, cb_ref, wr_ref, br_ref, wi_ref, bi_ref, lam_ref,
                    a_ref, b_ref, *, tt):
    i = pl.program_id(1)
    halo = jnp.where(i == 0, buf_ref[...], halo_ref[...])
    full = jnp.concatenate([halo, x_ref[...]], axis=0)
    taps = [pltpu.roll(full, CONV_W - 1 - k, axis=0)[8:8 + tt] for k in range(CONV_W - 1)]
    a, b = _rglru_gates(taps[0], taps[1], taps[2], x_ref[...], cw_ref[...], cb_ref[...], wr_ref[...], br_ref[...],
                        wi_ref[...], bi_ref[...], lam_ref[...])
    a_ref[...] = a
    b_ref[...] = b


def _rglru_param_specs(nidx):
    z2 = lambda *_: (0, 0)
    z3 = lambda *_: (0, 0, 0)
    del nidx
    return [pl.BlockSpec((CONV_W, W_A), z2), pl.BlockSpec((1, W_A), z2),
            pl.BlockSpec((NB_A, BS_A, BS_A), z3), pl.BlockSpec((1, W_A), z2),
            pl.BlockSpec((NB_A, BS_A, BS_A), z3), pl.BlockSpec((1, W_A), z2),
            pl.BlockSpec((1, W_A), z2)]


def _rglru_seq(z, buf8, prm, tt):
    B, L, _ = z.shape
    hb = tt // 8
    out = jax.ShapeDtypeStruct((B, L, W_A), F32)
    return pl.pallas_call(
        functools.partial(_rglru_seq_body, tt=tt),
        grid=(B, L // tt),
        in_specs=[pl.BlockSpec((None, tt, W_A), lambda b, i: (b, i, 0)),
                  pl.BlockSpec((None, 8, W_A), lambda b, i: (b, jnp.maximum(i * hb - 1, 0), 0)),
                  pl.BlockSpec((None, 8, W_A), lambda b, i: (b, 0, 0))] + _rglru_param_specs(2),
        out_specs=[pl.BlockSpec((None, tt, W_A), lambda b, i: (b, i, 0))] * 2,
        out_shape=[out, out],
        compiler_params=_cp(("arbitrary", "arbitrary")),
        name="rglru_conv_gates",
    )(z, z, buf8, *prm)


def _rglru_scan_body(a_ref, b_ref, ag_ref, h0_ref, y_ref, hl_ref, h_scr, *, tt):
    @pl.when(pl.program_id(1) == 0)
    def _():
        h_scr[...] = h0_ref[...]

    def step(t, h):
        h = a_ref[t] * h + b_ref[t]
        y_ref[t] = h
        return h

    h = lax.fori_loop(0, tt, step, h_scr[...], unroll=8)
    h_scr[...] = h
    hl_ref[...] = h
    y_ref[...] = y_ref[...] * _gelu_tanh(ag_ref[...])


def _rglru_scan(a, b, z, h0, tt):
    B, L, _ = a.shape
    a4 = a.reshape(B, L, 8, LANES)
    b4 = b.reshape(B, L, 8, LANES)
    z4 = z.reshape(B, L, z.shape[-1] // LANES, LANES)
    spec = pl.BlockSpec((None, tt, 8, LANES), lambda bb, i: (bb, i, 0, 0))
    y, hl = pl.pallas_call(
        functools.partial(_rglru_scan_body, tt=tt),
        grid=(B, L // tt),
        in_specs=[spec, spec,
                  pl.BlockSpec((None, tt, 8, LANES), lambda bb, i: (bb, i, 1, 0)),
                  pl.BlockSpec((None, 8, LANES), lambda bb, i: (bb, 0, 0))],
        out_specs=[spec, pl.BlockSpec((None, 8, LANES), lambda bb, i: (bb, 0, 0))],
        out_shape=[jax.ShapeDtypeStruct((B, L, 8, LANES), F32), jax.ShapeDtypeStruct((B, 8, LANES), F32)],
        scratch_shapes=[pltpu.VMEM((8, LANES), F32)],
        compiler_params=_cp(("arbitrary", "arbitrary")),
        name="rglru_scan_gelu_gate",
    )(a4, b4, z4, h0.reshape(B, 8, LANES))
    return y.reshape(B, L, W_A), hl.reshape(B, W_A)


def _rglru_step_body(x3_ref, ag_ref, x0_ref, x1_ref, x2_ref, h0_ref, cw_ref, cb_ref, wr_ref, br_ref, wi_ref, bi_ref,
                     lam_ref, y_ref, h_ref):
    a, b = _rglru_gates(x0_ref[...], x1_ref[...], x2_ref[...], x3_ref[...], cw_ref[...], cb_ref[...], wr_ref[...],
                        br_ref[...], wi_ref[...], bi_ref[...], lam_ref[...])
    h = a * h0_ref[...] + b
    h_ref[...] = h
    y_ref[...] = h * _gelu_tanh(ag_ref[...])


def _rglru_step(z, buf, h0, prm):
    n = z.shape[0]
    row = lambda c: pl.BlockSpec((n, W_A), lambda i: (0, c))
    out = jax.ShapeDtypeStruct((n, W_A), F32)
    return pl.pallas_call(
        _rglru_step_body,
        grid=(1,),
        in_specs=[row(0), row(1), row(0), row(0), row(0), row(0)] + _rglru_param_specs(1),
        out_specs=[row(0), row(0)],
        out_shape=[out, out],
        compiler_params=_cp(("arbitrary",)),
        name="rglru_decode_step",
    )(z, z, buf[:, 0], buf[:, 1], buf[:, 2], h0, *prm)


def _live_rows(C, first, valid):
    row = _iota((C, 1), 0)
    return (row >= first) & (row < first + valid)


def _hgrn2_chunk(q, fp, v, gate, lb, nw, S, *, C, valid, first=0):
    SB = min(16, C)
    k = (1.0 - lb) * _sigmoid(-fp)
    lf = jnp.log1p(-k)
    if valid < C:
        live = _live_rows(C, first, valid)
        k = jnp.where(live, k, 0.0)
        lf = jnp.where(live, lf, 0.0)
    g_col = _dot_sel_tn(lf, jnp.ones((C, DV_B), BF16))
    if valid == 1:
        kv = _dot_tn(k, v)
        yield
        S_new = jnp.exp(g_col) * S + kv
        o = _dot(q, S_new)
        yield
        return _rms(o) * nw * (gate * _sigmoid(gate)), S_new
    tril = _iota((C, C), 0) >= _iota((C, C), 1)
    b = _sel_dot(jnp.where(tril, 1.0, 0.0).astype(BF16), lf)
    yield
    g_row = b[C - 1:C, :]
    o = _dot(q * jnp.exp(b), S)
    khat = k * jnp.exp(g_row - b)
    S_new = jnp.exp(g_col) * S + _dot_tn(khat, v)

    tril_sb = _iota((SB, SB), 0) >= _iota((SB, SB), 1)
    atts = []
    for i in range(C // SB):
        lo = i * SB
        qi, bi, ki = q[lo:lo + SB], b[lo:lo + SB], k[lo:lo + SB]
        dec = jnp.exp(jnp.minimum(bi[:, None, :] - bi[None, :, :], 0.0))
        att = jnp.where(tril_sb, jnp.sum(qi[:, None, :] * ki[None, :, :] * dec, axis=-1), 0.0)
        off = None
        if i > 0:
            ref = b[lo - 1:lo, :]
            qt = qi * jnp.exp(bi - ref)
            kt = k[0:lo] * jnp.exp(ref - b[0:lo])
            off = _dot_nt(qt, kt)
        atts.append((att, off))
    yield
    parts = []
    for i, (att, off) in enumerate(atts):
        lo = i * SB
        oi = _dot(att, v[lo:lo + SB])
        parts.append(oi if off is None else oi + _dot(off, v[0:lo]))
    yield
    o = o + (parts[0] if len(parts) == 1 else jnp.concatenate(parts, axis=0))
    return _rms(o) * nw * (gate * _sigmoid(gate)), S_new


def _merge_rows(outs, C):
    row = _iota((C, 1), 0)
    y = jnp.where(row == 0, outs[0], 0.0)
    for s in range(1, len(outs)):
        y = jnp.where(row == s, outs[s], y)
    return y


def _chained(prev, out_index, operands, specs):
    if prev is None:
        return {}
    operands.append(prev)
    specs.append(pl.BlockSpec(memory_space=pl.ANY))
    return {len(operands) - 1: out_index}


def _hgrn2_body(q_ref, f_ref, v_ref, g_ref, lb_ref, nw_ref, s0_ref, *rest, C, valid, nb, hp, decode, nchunk):
    y_ref, s1_ref = rest[-2:]

    @pl.when(pl.program_id(2) == 0)
    def _():
        s1_ref[...] = s0_ref[...]

    nw = nw_ref[...]
    probs = [(bi, hh) for bi in range(nb) for hh in range(hp)]

    def chunk(c, carry):
        rows = pl.ds(pl.multiple_of(c * C, C), C)
        gens = []
        for bi, hh in probs:
            lanes = slice(hh * LANES, (hh + 1) * LANES)
            src = 0 if decode else bi
            gens.append(_hgrn2_chunk(q_ref[src, rows, lanes], f_ref[src, rows, lanes], v_ref[src, rows, lanes],
                                     g_ref[src, rows, lanes], lb_ref[:, lanes], nw, s1_ref[bi, hh],
                                     C=C, valid=valid, first=bi if decode else 0))
        outs = _interleave(gens)
        for (bi, hh), (y, s_new) in zip(probs, outs):
            s1_ref[bi, hh] = s_new
            if not decode:
                y_ref[bi, rows, hh * LANES:(hh + 1) * LANES] = y
        if decode:
            for hh in range(hp):
                y_ref[0, rows, hh * LANES:(hh + 1) * LANES] = _merge_rows([outs[bi * hp + hh][0] for bi in range(nb)], C)
        return carry

    lax.fori_loop(0, nchunk, chunk, 0)


def _hgrn2(z, lb, nw, s_all, s_prev, j, *, C, valid, nb, hp, tt, decode):
    B, L, _ = z.shape
    zb = 1 if decode else nb
    w = hp * LANES
    col = lambda base: pl.BlockSpec((zb, tt, w), lambda b, h, i: (b, i, base // hp + h))
    st = pl.BlockSpec((None, nb, hp, DK_B, DV_B), lambda b, h, i: (j, b, h, 0, 0))
    operands = [z, z, z, z, lb, nw, s_all]
    specs = [col(16), col(24), col(32), col(40), pl.BlockSpec((1, w), lambda b, h, i: (0, h)),
             pl.BlockSpec((1, LANES), lambda b, h, i: (0, 0)), st]
    aliases = _chained(s_prev, 1, operands, specs)
    return pl.pallas_call(
        functools.partial(_hgrn2_body, C=C, valid=valid, nb=nb, hp=hp, decode=decode, nchunk=tt // C),
        grid=(B // zb, H_B // hp, L // tt),
        in_specs=specs,
        out_specs=[col(0), st],
        out_shape=[jax.ShapeDtypeStruct((B, L, H_B * DV_B), F32), jax.ShapeDtypeStruct(s_all.shape, F32)],
        input_output_aliases=aliases,
        compiler_params=_cp(("arbitrary", "arbitrary", "arbitrary")),
        name="hgrn2_chunked",
    )(*operands)


def _mlstm_step(q, k, v, op, gt, h, nw, Cst, n, m, *, C, first):
    lane = _iota((C, LANES), 1)
    live = _live_rows(C, first, 1)
    i0 = jnp.sum(jnp.where(live & (lane == h), gt, 0.0), keepdims=True)
    lf0 = _log_sigmoid(jnp.sum(jnp.where(live & (lane == h + H_C), gt, 0.0), keepdims=True))
    inter = lf0 + m
    m_t = jnp.maximum(inter, i0)
    w_inter = jnp.exp(inter - m_t)
    wk = jnp.exp(i0 - m_t)
    kh = k * (DK_C ** -0.5)
    s = jnp.sum(q * kh, axis=1, keepdims=True) * wk
    qc = _dot(q, Cst)
    kw = jnp.where(live, kh * wk, 0.0)
    kv = _dot_tn(kw, v)
    yield
    num = w_inter * qc + s * v
    den = w_inter * jnp.sum(q * n, axis=1, keepdims=True) + s
    hh = num / jnp.maximum(jnp.abs(den), jnp.exp(-m_t))
    C_new = w_inter * Cst + kv
    n_new = w_inter * n + jnp.sum(kw, axis=0, keepdims=True)
    y = _rms(hh) * nw * _sigmoid(op)
    return y, C_new, n_new, m_t


def _mlstm_chunk(q, k, v, op, gt, h, nw, Cst, n, m, *, C, valid, first=0):
    if valid == 1:
        return (yield from _mlstm_step(q, k, v, op, gt, h, nw, Cst, n, m, C=C, first=first))
    assert first == 0
    gtT = gt.T
    lane = _iota((C, LANES), 1)
    sub = _iota((LANES, C), 0)
    i_col = jnp.sum(jnp.where(lane == h, gt, 0.0), axis=1, keepdims=True)
    f_col = jnp.sum(jnp.where(lane == h + H_C, gt, 0.0), axis=1, keepdims=True)
    i_row = jnp.sum(jnp.where(sub == h, gtT, 0.0), axis=0, keepdims=True)
    f_row = jnp.sum(jnp.where(sub == h + H_C, gtT, 0.0), axis=0, keepdims=True)
    lf_col = _log_sigmoid(f_col)
    lf_row = _log_sigmoid(f_row)
    if valid < C:
        live_c = _iota((C, 1), 0) < valid
        live_r = _iota((1, C), 1) < valid
        i_col = jnp.where(live_c, i_col, NEG_BIG)
        i_row = jnp.where(live_r, i_row, NEG_BIG)
        lf_col = jnp.where(live_c, lf_col, 0.0)
        lf_row = jnp.where(live_r, lf_row, 0.0)
    tril = _iota((C, C), 0) >= _iota((C, C), 1)
    b_col = jnp.sum(jnp.where(tril, lf_row, 0.0), axis=1, keepdims=True)
    b_row = jnp.sum(jnp.where(_iota((C, C), 0) <= _iota((C, C), 1), lf_col, 0.0), axis=0, keepdims=True)
    dm = jnp.where(tril, b_col - b_row + i_row, NEG_BIG)
    inter = b_col + m
    m_t = jnp.maximum(inter, jnp.max(dm, axis=1, keepdims=True))
    w_inter = jnp.exp(inter - m_t)
    kh = k * (DK_C ** -0.5)
    qk = _dot_nt(q, kh)
    qc = _dot(q, Cst)
    g = b_col[C - 1:C, :]
    m_new = m_t[C - 1:C, :]
    wk = jnp.exp(g - b_col + i_col - m_new)
    f_state = jnp.exp(g + m - m_new)
    kw = kh * wk
    kv = _dot_tn(kw, v)
    yield
    s = qk * jnp.exp(dm - m_t)
    sv = _dot(s, v)
    yield
    num = w_inter * qc + sv
    den = w_inter * jnp.sum(q * n, axis=1, keepdims=True) + jnp.sum(s, axis=1, keepdims=True)
    hh = num / jnp.maximum(jnp.abs(den), jnp.exp(-m_t))
    C_new = f_state * Cst + kv
    n_new = f_state * n + jnp.sum(kw, axis=0, keepdims=True)
    y = _rms(hh) * nw * _sigmoid(op)
    return y, C_new, n_new, m_new


def _mlstm_body(q_ref, k_ref, v_ref, o_ref, g_ref, gb_ref, nw_ref, c0_ref, n0_ref, m0_ref, *rest,
                C, valid, nb, decode, nchunk):
    y_ref, c1_ref, n1_ref, m1_ref = rest[-4:]

    @pl.when(pl.program_id(2) == 0)
    def _():
        c1_ref[...] = c0_ref[...]
        n1_ref[...] = n0_ref[...]
        m1_ref[...] = m0_ref[...]

    h = pl.program_id(1)
    gb = gb_ref[...]
    nw = nw_ref[...]

    def chunk(c, carry):
        rows = pl.ds(pl.multiple_of(c * C, C), C)
        gens = []
        for bi in range(nb):
            src = 0 if decode else bi
            gens.append(_mlstm_chunk(
                q_ref[src, rows, :], k_ref[src, rows, :], v_ref[src, rows, :], o_ref[src, rows, :],
                g_ref[src, rows, :] + gb, h, nw, c1_ref[bi], n1_ref[bi], m1_ref[bi][:, 0:1],
                C=C, valid=valid, first=bi if decode else 0))
        outs = _interleave(gens)
        for bi, (y, c_new, n_new, m_new) in enumerate(outs):
            if not decode:
                y_ref[bi, rows, :] = y
            c1_ref[bi] = c_new
            n1_ref[bi] = n_new
            m1_ref[bi] = jnp.broadcast_to(m_new, (1, LANES))
        if decode:
            y_ref[0, rows, :] = _merge_rows([o[0] for o in outs], C)
        return carry

    lax.fori_loop(0, nchunk, chunk, 0)


def _mlstm(z, gb, nw, states, prev, j, *, C, valid, nb, tt, decode):
    B, L, _ = z.shape
    zb = 1 if decode else nb
    c128 = lambda base: pl.BlockSpec((zb, tt, LANES), lambda b, h, i: (b, i, base + h))
    c256 = lambda base: pl.BlockSpec((zb, tt, DV_C), lambda b, h, i: (b, i, base + h))
    cst = pl.BlockSpec((None, nb, None, DK_C, DV_C), lambda b, h, i: (j, b, h, 0, 0))
    vec = pl.BlockSpec((None, nb, None, 1, LANES), lambda b, h, i: (j, b, h, 0, 0))
    operands = [z, z, z, z, z, gb, nw, *states]
    specs = [c128(0), c128(4), c256(4), c256(8), pl.BlockSpec((zb, tt, LANES), lambda b, h, i: (b, i, 24)),
             pl.BlockSpec((1, LANES), lambda b, h, i: (0, 0)), pl.BlockSpec((1, DV_C), lambda b, h, i: (0, 0)),
             cst, vec, vec]
    aliases = {}
    for k, p in enumerate(prev or ()):
        aliases.update(_chained(p, 1 + k, operands, specs))
    outs = pl.pallas_call(
        functools.partial(_mlstm_body, C=C, valid=valid, nb=nb, decode=decode, nchunk=tt // C),
        grid=(B // zb, H_C, L // tt),
        in_specs=specs,
        out_specs=[c256(0), cst, vec, vec],
        out_shape=[jax.ShapeDtypeStruct((B, L, H_C * DV_C), F32)] + [jax.ShapeDtypeStruct(s.shape, F32) for s in states],
        input_output_aliases=aliases,
        compiler_params=_cp(("arbitrary", "arbitrary", "arbitrary")),
        name="mlstm_chunked",
    )(*operands)
    return outs[0], tuple(outs[1:])


def _head_sum(x, bd):
    return jnp.concatenate([_dot_sel(x[:, n * LANES:(n + 1) * LANES], bd) for n in range(W_D // LANES)], axis=1)


def _rwkv_pre(zd, sh, mu, w0, a0, kkw, ka, rk, w2p, a2p, g2, bd):
    zs = zd + (sh - zd) * mu
    r, k, v = zs[:, 0:W_D], zs[:, W_D:2 * W_D], zs[:, 2 * W_D:3 * W_D]
    wa = zs[:, 3 * W_D:3 * W_D + LANES]
    gl = zs[:, 3 * W_D + LANES:]
    w = -_softplus(-(w0 + _dot(jnp.tanh(wa), w2p))) - 0.5
    logd = -jnp.exp(w)
    a = _sigmoid(a0 + _dot(wa, a2p))
    gate = _dot(_sigmoid(gl), g2)
    kk = k * kkw
    kk = kk / jnp.maximum(jnp.sqrt(_head_sum(kk * kk, bd)), 1e-12)
    kmod = k * (1.0 + (a - 1.0) * ka)
    bonus = _head_sum(r * kmod * rk, bd) * v
    return r, logd, kmod, v, kk, kk * a, gate, bonus


def _head_block_ones():
    return jnp.where((_iota((LANES, LANES), 0) >> 6) == (_iota((LANES, LANES), 1) >> 6), 1.0, 0.0).astype(BF16)


def _rwkv_pre_seq_body(z_ref, halo_ref, prev_ref, mu_ref, w0_ref, a0_ref, kk_ref, ka_ref, rk_ref, w2_ref, a2_ref,
                       g2_ref, *out_refs, tt):
    i = pl.program_id(1)
    halo = jnp.where(i == 0, prev_ref[...], halo_ref[...])
    zd = z_ref[...]
    sh = pltpu.roll(jnp.concatenate([halo, zd], axis=0), 1, axis=0)[8:8 + tt]
    outs = _rwkv_pre(zd, sh, mu_ref[...], w0_ref[...], a0_ref[...], kk_ref[...], ka_ref[...], rk_ref[...],
                     w2_ref[...], a2_ref[...], g2_ref[...], _head_block_ones())
    for ref, val in zip(out_refs, outs):
        ref[...] = val


def _rwkv_pre_step_body(z_ref, sh_ref, mu_ref, w0_ref, a0_ref, kk_ref, ka_ref, rk_ref, w2_ref, a2_ref, g2_ref,
                        *out_refs):
    outs = _rwkv_pre(z_ref[...], sh_ref[...], mu_ref[...], w0_ref[...], a0_ref[...], kk_ref[...], ka_ref[...],
                     rk_ref[...], w2_ref[...], a2_ref[...], g2_ref[...], _head_block_ones())
    for ref, val in zip(out_refs, outs):
        ref[...] = val


def _rwkv_pre_param_specs():
    z2 = lambda *_: (0, 0)
    return ([pl.BlockSpec((1, P_D), z2)] + [pl.BlockSpec((1, W_D), z2)] * 5
            + [pl.BlockSpec((LANES, W_D), z2)] * 3)


def _rwkv_pre_seq(z, prev8, prm, tt):
    B, L, _ = z.shape
    hb = tt // 8
    out = jax.ShapeDtypeStruct((B, L, W_D), F32)
    return pl.pallas_call(
        functools.partial(_rwkv_pre_seq_body, tt=tt),
        grid=(B, L // tt),
        in_specs=[pl.BlockSpec((None, tt, P_D), lambda b, i: (b, i, 0)),
                  pl.BlockSpec((None, 8, P_D), lambda b, i: (b, jnp.maximum(i * hb - 1, 0), 0)),
                  pl.BlockSpec((None, 8, P_D), lambda b, i: (b, 0, 0))] + _rwkv_pre_param_specs(),
        out_specs=[pl.BlockSpec((None, tt, W_D), lambda b, i: (b, i, 0))] * 8,
        out_shape=[out] * 8,
        compiler_params=_cp(("arbitrary", "arbitrary")),
        name="rwkv7_token_shift_features",
    )(z, z, prev8, *prm)


def _rwkv_pre_step(z, shifted, prm):
    n = z.shape[0]
    out = jax.ShapeDtypeStruct((n, W_D), F32)
    return pl.pallas_call(
        _rwkv_pre_step_body,
        grid=(1,),
        in_specs=[pl.BlockSpec((n, P_D), lambda i: (0, 0)), pl.BlockSpec((n, P_D), lambda i: (0, 0))]
        + _rwkv_pre_param_specs(),
        out_specs=[pl.BlockSpec((n, W_D), lambda i: (0, 0))] * 8,
        out_shape=[out] * 8,
        compiler_params=_cp(("arbitrary",)),
        name="rwkv7_decode_features",
    )(z, shifted, *prm)


def _unit_lower_inverse(N, C):
    n = N.shape[0]
    ri, ci = _iota((n, n), 0), _iota((n, n), 1)
    base = min(8, C)
    kb = _log2(base)
    X = jnp.where((ri >> kb) == (ci >> kb), N, 0.0)
    T = jnp.where(ri == ci, 1.0, 0.0) + X
    for _ in range(kb - 1):
        X = _dot(X, X)
        yield
        T = T + _dot(T, X)
        yield
    size = base
    while size < C:
        ks = _log2(size)
        sel = ((ri >> (ks + 1)) == (ci >> (ks + 1))) & (((ri >> ks) & 1) == 1) & (((ci >> ks) & 1) == 0)
        TN_ = _dot(T, jnp.where(sel, N, 0.0))
        yield
        T = T + _dot(TN_, T)
        yield
        size *= 2
    return T


def _interleave(gens):
    results = [None] * len(gens)
    live = list(range(len(gens)))
    while live:
        for i in list(live):
            try:
                next(gens[i])
            except StopIteration as stop:
                results[i] = stop.value
                live.remove(i)
    return results


def _rwkv_groupnorm_gate(y, gate, bonus, lnw, lnb):
    bd = _head_block_ones()
    mean = _dot_sel(y, bd) * (1.0 / N_D)
    yield
    dy = y - mean
    var = _dot_sel(dy * dy, bd) * (1.0 / N_D)
    yield
    return (dy * lax.rsqrt(var + GN_EPS_D) * lnw + lnb + bonus) * gate


def _rwkv_step(r, ld, k, v, kk, be, S2, *, C, first):
    live = _live_rows(C, first, 1)
    zero = lambda x: jnp.where(live, x, 0.0)
    same_head = (_iota((LANES, LANES), 0) >> 6) == (_iota((LANES, LANES), 1) >> 6)
    spread = jnp.where(_iota((N_D, LANES), 0) == (_iota((N_D, LANES), 1) & (N_D - 1)), 1.0, 0.0).astype(BF16)
    fold = jnp.where((_iota((LANES, N_D), 0) & (N_D - 1)) == _iota((LANES, N_D), 1), 1.0, 0.0).astype(BF16)
    Pn = jnp.where(same_head, _dot_sel(S2, spread), 0.0)
    yield
    d_row = jnp.exp(jnp.sum(zero(ld), axis=0, keepdims=True))
    U = _dot_nt(zero(-kk), Pn)
    yield
    upd = _dot_tn(jnp.concatenate([U, zero(v)], axis=0), jnp.concatenate([zero(be), zero(k)], axis=0))
    yield
    P_new = Pn * d_row + jnp.where(same_head, upd, 0.0)
    y = _dot_nt(r, P_new)
    S2_new = _dot_sel(P_new, fold)
    yield
    return y, S2_new


def _rwkv_chunk(r, ld, k, v, kk, be, gate, bonus, lnw, lnb, P, *, C, valid, first=0):
    if valid < C:
        live = _live_rows(C, first, valid)
        ld = jnp.where(live, ld, 0.0)
        be = jnp.where(live, be, 0.0)
        k = jnp.where(live, k, 0.0)
        v = jnp.where(live, v, 0.0)
    tril = _iota((C, C), 0) >= _iota((C, C), 1)
    c = _sel_dot(jnp.where(tril, 1.0, 0.0).astype(BF16), ld)
    g_col = _dot_sel_tn(ld, jnp.ones((C, LANES), BF16))
    yield
    ec, enc = jnp.exp(c), jnp.exp(-c)
    lane0 = _iota((C, LANES), 1) < N_D

    def stack(x):
        return jnp.concatenate([jnp.where(lane0, x, 0.0), jnp.where(lane0, 0.0, x)], axis=0)

    n2 = 2 * C
    AR = jnp.concatenate([stack(-kk * jnp.exp(c - ld)), stack(r * ec)], axis=0).astype(BF16)
    BK = jnp.concatenate([stack(be * enc), stack(k * enc)], axis=0).astype(BF16)
    V2 = stack(v).astype(BF16)
    kc = _log2(C)
    ri, ci = _iota((2 * n2, 2 * n2), 0), _iota((2 * n2, 2 * n2), 1)
    tpos, spos = ri & (C - 1), ci & (C - 1)
    keep = (((ri >> kc) & 1) == ((ci >> kc) & 1)) & ((spos < tpos) | ((ri >= n2) & (spos == tpos)))
    G = jnp.where(keep, _dg(AR, BK, _NT), 0.0)
    LP = _mm(AR, P.astype(BF16))
    yield
    GV = _mm(G[:, n2:].astype(BF16), V2)
    T = yield from _unit_lower_inverse(G[0:n2, 0:n2], C)
    U = _dot(T, LP[0:n2] + GV[0:n2])
    yield
    Y2 = LP[n2:] + GV[n2:] + _dot(G[n2:, 0:n2], U)
    P_new = jnp.exp(g_col) * (P + _dg(BK, jnp.concatenate([U.astype(BF16), V2], axis=0), _TN))
    yield
    y = Y2[0:C] + Y2[C:n2]
    out = yield from _rwkv_groupnorm_gate(y, gate, bonus, lnw, lnb)
    return out, P_new


def _rwkv_body(r_ref, ld_ref, k_ref, v_ref, kk_ref, be_ref, gt_ref, bo_ref, lnw_ref, lnb_ref, p0_ref,
               y_ref, p1_ref, *, C, valid, nb, hp, nchunk):
    @pl.when(pl.program_id(2) == 0)
    def _():
        p1_ref[...] = p0_ref[...]

    probs = [(bi, pp) for bi in range(nb) for pp in range(hp)]
    in_refs = (r_ref, ld_ref, k_ref, v_ref, kk_ref, be_ref, gt_ref, bo_ref)

    def chunk(c, carry):
        rows = pl.ds(pl.multiple_of(c * C, C), C)
        gens = []
        for bi, pp in probs:
            lanes = slice(pp * LANES, (pp + 1) * LANES)
            args = [ref[bi, rows, lanes] for ref in in_refs] + [lnw_ref[:, lanes], lnb_ref[:, lanes], p1_ref[bi, pp]]
            gens.append(_rwkv_chunk(*args, C=C, valid=valid))
        for (bi, pp), (y, p_new) in zip(probs, _interleave(gens)):
            p1_ref[bi, pp] = p_new
            y_ref[bi, rows, pp * LANES:(pp + 1) * LANES] = y
        return carry

    lax.fori_loop(0, nchunk, chunk, 0)


def _rwkv_decode_body(r_ref, ld_ref, k_ref, v_ref, kk_ref, be_ref, gt_ref, bo_ref, lnw_ref, lnb_ref, s0_ref, *rest,
                      nb):
    y_ref, s1_ref = rest[-2:]
    tiles = [ref[0] for ref in (r_ref, ld_ref, k_ref, v_ref, kk_ref, be_ref)]
    gens = [_rwkv_step(*tiles, jnp.concatenate([s0_ref[bi, 0], s0_ref[bi, 1]], axis=0), C=nb, first=bi)
            for bi in range(nb)]
    outs = _interleave(gens)
    for bi, (_, s2) in enumerate(outs):
        s1_ref[bi, 0] = s2[0:N_D]
        s1_ref[bi, 1] = s2[N_D:]
    y = _merge_rows([o[0] for o in outs], nb)
    y_ref[0] = _interleave([_rwkv_groupnorm_gate(y, gt_ref[0], bo_ref[0], lnw_ref[...], lnb_ref[...])])[0]


def _rwkv_decode(feats, lnw, lnb, s_all, s_prev, j, *, nb):
    B = feats[0].shape[0]
    col = pl.BlockSpec((1, nb, LANES), lambda b, p: (b, 0, p))
    vec = pl.BlockSpec((1, LANES), lambda b, p: (0, p))
    st = pl.BlockSpec((None, nb, 2, N_D, N_D), lambda b, p: (j, b, p, 0, 0))
    operands = [*feats, lnw, lnb, s_all]
    specs = [col] * 8 + [vec, vec, st]
    aliases = _chained(s_prev, 1, operands, specs)
    return pl.pallas_call(
        functools.partial(_rwkv_decode_body, nb=nb),
        grid=(B, H_D // 2),
        in_specs=specs,
        out_specs=[col, st],
        out_shape=[jax.ShapeDtypeStruct((B, nb, W_D), F32), jax.ShapeDtypeStruct(s_all.shape, F32)],
        input_output_aliases=aliases,
        compiler_params=_cp(("arbitrary", "arbitrary")),
        name="rwkv7_decode_step",
    )(*operands)


def _rwkv(feats, lnw, lnb, p0, *, C, valid, nb, hp, tt):
    B, L, _ = feats[0].shape
    w = hp * LANES
    col = pl.BlockSpec((nb, tt, w), lambda b, p, i: (b, i, p))
    vec = pl.BlockSpec((1, w), lambda b, p, i: (0, p))
    st = pl.BlockSpec((nb, hp, LANES, LANES), lambda b, p, i: (b, p, 0, 0))
    return pl.pallas_call(
        functools.partial(_rwkv_body, C=C, valid=valid, nb=nb, hp=hp, nchunk=tt // C),
        grid=(B // nb, H_D // 2 // hp, L // tt),
        in_specs=[col] * 8 + [vec, vec, st],
        out_specs=[col, st],
        out_shape=[jax.ShapeDtypeStruct((B, L, W_D), F32), jax.ShapeDtypeStruct(p0.shape, F32)],
        compiler_params=_cp(("arbitrary", "arbitrary", "arbitrary")),
        name="rwkv7_chunked",
    )(*feats, lnw, lnb, p0)


def _pair_blockdiag(s):
    B = s.shape[0]
    st = s.reshape(B, H_D // 2, 2, N_D, N_D)
    zero = jnp.zeros_like(st[:, :, 0])
    top = jnp.concatenate([st[:, :, 0], zero], axis=-1)
    bot = jnp.concatenate([zero, st[:, :, 1]], axis=-1)
    return jnp.concatenate([top, bot], axis=-2)


def _pair_unblockdiag(p):
    B = p.shape[0]
    st = jnp.stack([p[:, :, :N_D, :N_D], p[:, :, N_D:, N_D:]], axis=2)
    return st.reshape(B, H_D, N_D, N_D)


def _trunk(x, mod, st, prm, *, decode):
    G, Lg, _ = x.shape
    tm = min(128 if decode else 512, Lg)
    tm_in = tm if decode or Lg % 1024 else 1024
    assert decode or Lg % 512 == 0
    nseq = Lg if decode else G
    new = {k: [] for k in ("a_conv", "a_h", "d_shift", "d_s")}
    b_s = None
    c_in = (st["c_c"], st["c_n"][:, :, :, None, :],
            jnp.broadcast_to(st["c_m"][..., None, None], st["c_m"].shape + (1, LANES)))
    c_out = d_s = None
    blocks = lambda t: t.reshape(nseq // DEC_NB, DEC_NB, -1)
    ffn_bf16 = {} if decode else prm["ffn_bf16"]

    def ffn(x, l, which):
        if decode:
            assert Lg == tm
            x, ffn_bf16[l, which] = _ffn_cast(x, mod, prm["ffn_w_gate"], prm["ffn_w_up"], prm["ffn_w_down"], l, which)
            return x
        return _ffn(x, mod, ffn_bf16[l, which], l, which, tm)

    for l in range(DEPTH):
        j = l // 2
        x = ffn(x, l, 0)
        if l % 2 == 0:
            z = _inproj(x, mod, prm["w_in_even"], l, j, tm_in)
            a_prm = prm["rglru"][j]
            conv0, h0 = st["a_conv"][j], st["a_h"][j]
            if decode:
                z2 = z[0]
                y1, h1 = _rglru_step(z2, conv0, h0, a_prm)
                conv1 = jnp.concatenate([conv0[:, 1:], z2[:, None, :W_A]], axis=1)
                yb, b_s = _hgrn2(blocks(z2), prm["lb"][j], prm["b_norm_w"][j], st["b_s"], b_s, j,
                                 C=DEC_NB, valid=1, nb=DEC_NB, hp=1, tt=DEC_NB, decode=True)
                y1, y2 = y1[None], yb.reshape(1, nseq, -1)
            else:
                buf8 = jnp.concatenate([jnp.zeros((nseq, 8 - (CONV_W - 1), W_A), F32), conv0], axis=1)
                a, b = _rglru_seq(z, buf8, a_prm, 512)
                y1, h1 = _rglru_scan(a, b, z, h0, 512)
                conv1 = z[:, Lg - (CONV_W - 1):, :W_A]
                y2, b_s = _hgrn2(z, prm["lb"][j], prm["b_norm_w"][j], st["b_s"], b_s, j,
                                 C=CHUNK, valid=CHUNK, nb=nseq, hp=2, tt=512, decode=False)
            new["a_conv"].append(conv1)
            new["a_h"].append(h1)
            x = _outproj(x, mod, y1, y2, prm["w_out_even"], l, j, tm)
        else:
            z = _inproj(x, mod, prm["w_in_odd"], l, j, tm_in, width=MLSTM_COLS, w_is_out_by_in=True)
            zd = _inproj(x, mod, prm["w_in_rwkv"], l, j, tm, tn=P_D // 2, w_is_out_by_in=True)
            d_prm = prm["rwkv"][j]
            if decode:
                yc, c_out = _mlstm(blocks(z[0]), prm["c_gate_b"][j], prm["c_norm_w"][j], c_in, c_out, j,
                                   C=DEC_NB, valid=1, nb=DEC_NB, tt=DEC_NB, decode=True)
                feats = [blocks(t) for t in _rwkv_pre_step(zd[0], st["d_shift"][j], d_prm)]
                yd, d_s = _rwkv_decode(feats, prm["d_ln_w"][j], prm["d_ln_b"][j], st["d_s"], d_s, j, nb=DEC_NB)
                y1, y2 = yc.reshape(1, nseq, -1), yd.reshape(1, nseq, -1)
                shift1 = zd[0]
            else:
                y1, c_out = _mlstm(z, prm["c_gate_b"][j], prm["c_norm_w"][j], c_in, c_out, j,
                                   C=CHUNK, valid=CHUNK, nb=nseq, tt=256, decode=False)
                prev8 = jnp.concatenate([jnp.zeros((nseq, 7, P_D), F32), st["d_shift"][j][:, None]], axis=1)
                feats = _rwkv_pre_seq(zd, prev8, d_prm, 128)
                p0 = _pair_blockdiag(jnp.swapaxes(st["d_s"][j], -1, -2))
                y2, p1 = _rwkv(feats, prm["d_ln_w"][j], prm["d_ln_b"][j], p0, C=CHUNK, valid=CHUNK, nb=nseq, hp=2,
                               tt=256)
                shift1 = zd[:, -1]
                new["d_s"].append(jnp.swapaxes(_pair_unblockdiag(p1), -1, -2))
            new["d_shift"].append(shift1)
            x = _outproj(x, mod, y1, y2, prm["w_out_odd"], l, j, tm)
        x = ffn(x, l, 1)
    y = _final_norm(x, prm["final_norm_w"], tm)
    if not decode:
        d_s = jnp.stack(new["d_s"])
    outs = (y, jnp.stack(new["a_conv"]), jnp.stack(new["a_h"]), b_s, c_out[0], c_out[1][:, :, :, 0, :],
            c_out[2][:, :, :, 0, 0], jnp.stack(new["d_shift"]), d_s)
    return outs, ffn_bf16


def _prepare(w_mod, b_mod, ffn_w_gate, ffn_w_up, ffn_w_down, w_in_even, w_out_even, a_conv_w, a_conv_b, a_gate_r_w,
             a_gate_r_b, a_gate_i_w, a_gate_i_b, a_lambda, b_lb_gamma, b_norm_w, w_in_odd, w_out_odd, c_igate_b,
             c_fgate_b, c_norm_w, d_mu, d_w0, d_w2, d_a0, d_a2, d_g2, d_k_k, d_k_a, d_r_k, d_ln_w, d_ln_b,
             final_norm_w):
    n_gate = 2 * H_C
    w_odd_t = jnp.swapaxes(w_in_odd, 1, 2).astype(BF16)
    row = lambda t: t[:, None, :]
    zpad = jnp.zeros((N_ODD, LANES - n_gate), F32)
    half = jnp.zeros((N_ODD, R_W, W_D), F32)
    rglru = [(a_conv_w[j], a_conv_b[j][None], a_gate_r_w[j].astype(BF16), a_gate_r_b[j][None],
              a_gate_i_w[j].astype(BF16), a_gate_i_b[j][None], a_lambda[j][None]) for j in range(N_EVEN)]
    w2p = jnp.concatenate([d_w2, half], axis=1).astype(BF16)
    a2p = jnp.concatenate([half, d_a2], axis=1).astype(BF16)
    g2 = d_g2.astype(BF16)
    rwkv = [(d_mu[j][None], d_w0[j][None], d_a0[j][None], d_k_k[j][None], d_k_a[j][None],
             d_r_k[j].reshape(1, W_D), w2p[j], a2p[j], g2[j]) for j in range(N_ODD)]
    return dict(
        ffn_w_gate=ffn_w_gate, ffn_w_up=ffn_w_up, ffn_w_down=ffn_w_down,
        w_in_even=w_in_even.astype(BF16), w_in_odd=w_odd_t, w_in_rwkv=w_odd_t[:, P_ODD - P_D:],
        w_out_even=w_out_even.astype(BF16).reshape(N_EVEN, 2, D // 2, D),
        w_out_odd=w_out_odd.astype(BF16).reshape(N_ODD, 2, D // 2, D),
        rglru=rglru, rwkv=rwkv,
        lb=row(_lower_bounds(b_lb_gamma)), b_norm_w=row(b_norm_w),
        c_gate_b=row(jnp.concatenate([c_igate_b, c_fgate_b, zpad], axis=-1)), c_norm_w=row(c_norm_w),
        d_ln_w=row(d_ln_w), d_ln_b=row(d_ln_b), final_norm_w=final_norm_w)


def kernel(x_prompt, x_sample, c_prompt, c_sample, state_a_conv, state_a_h, state_b_s, state_c_c, state_c_n, state_c_m, state_d_shift, state_d_s, w_mod, b_mod, ffn_w_gate, ffn_w_up, ffn_w_down, w_in_even, w_out_even, a_conv_w, a_conv_b, a_gate_r_w, a_gate_r_b, a_gate_i_w, a_gate_i_b, a_lambda, b_lb_gamma, b_norm_w, w_in_odd, w_out_odd, c_igate_b, c_fgate_b, c_norm_w, d_mu, d_w0, d_w2, d_a0, d_a2, d_g2, d_k_k, d_k_a, d_r_k, d_ln_w, d_ln_b, final_norm_w):
    prm = _prepare(w_mod, b_mod, ffn_w_gate, ffn_w_up, ffn_w_down, w_in_even, w_out_even, a_conv_w, a_conv_b,
                   a_gate_r_w, a_gate_r_b, a_gate_i_w, a_gate_i_b, a_lambda, b_lb_gamma, b_norm_w, w_in_odd,
                   w_out_odd, c_igate_b, c_fgate_b, c_norm_w, d_mu, d_w0, d_w2, d_a0, d_a2, d_g2, d_k_k, d_k_a,
                   d_r_k, d_ln_w, d_ln_b, final_norm_w)
    bp, lp, _ = x_prompt.shape
    bs = x_sample.shape[0]
    n_rows = -(-(bs + bp) // 8) * 8
    c_all = jnp.concatenate([c_sample, c_prompt, jnp.zeros((n_rows - bs - bp, D), F32)], axis=0)
    mod_all = _modulation(c_all, w_mod, b_mod)
    mod_s = mod_all[:, None]
    mod_p = mod_all[:, bs:bs + bp, None]

    zeros = lambda *s: jnp.zeros(s, F32)
    st_p = dict(a_conv=zeros(N_EVEN, bp, CONV_W - 1, W_A), a_h=zeros(N_EVEN, bp, W_A),
                b_s=zeros(N_EVEN, bp, H_B, DK_B, DV_B), c_c=zeros(N_ODD, bp, H_C, DK_C, DV_C),
                c_n=zeros(N_ODD, bp, H_C, DK_C), c_m=zeros(N_ODD, bp, H_C), d_shift=zeros(N_ODD, bp, P_D),
                d_s=zeros(N_ODD, bp, H_D, N_D, N_D))
    st_s = dict(a_conv=state_a_conv, a_h=state_a_h, b_s=state_b_s, c_c=state_c_c, c_n=state_c_n, c_m=state_c_m,
                d_shift=state_d_shift, d_s=state_d_s)
    out_s, ffn_bf16 = _trunk(x_sample.reshape(1, bs, D), mod_s, st_s, prm, decode=True)
    out_p, _ = _trunk(x_prompt, mod_p, st_p, dict(prm, ffn_bf16=ffn_bf16), decode=False)
    y_s = out_s[0].reshape(bs, 1, D)
    return (out_p[0], y_s) + out_p[1:] + out_s[1:]
```

```python
import functools

import jax
import jax.numpy as jnp
from jax import lax
from jax.experimental import pallas as pl
from jax.experimental.pallas import tpu as pltpu

F32 = jnp.float32
BF16 = jnp.bfloat16

D = 2048
DEPTH = 4
N_EVEN = 2
N_ODD = 2
D_FF = 5632
N_MOD = 9
EPS = 1e-6
CHUNK = 64
NEG_BIG = -1e30

W_A = 1024
NB_A = 8
BS_A = 128
CONV_W = 4
C_RGLRU = 8.0
H_B = 8
DK_B = 128
DV_B = 128
H_C = 4
DK_C = 128
DV_C = 256
N_D = 64
H_D = 16
W_D = 1024
R_W = 64
R_A = 64
R_G = 128
GN_EPS_D = 64e-5
P_EVEN = 6144
P_D = 3328
P_ODD = 6408
MLSTM_COLS = 3584
DEC_NB = 8
LANES = 128
VMEM_LIMIT = 48 * 2**20


def _cp(sem, vmem=VMEM_LIMIT):
    return pltpu.CompilerParams(dimension_semantics=sem, vmem_limit_bytes=vmem)


def _mm(a, b):
    return jnp.dot(a, b, preferred_element_type=F32)


def _dot(a, b):
    return _mm(a.astype(BF16), b.astype(BF16))


_NT = (((1,), (1,)), ((), ()))
_TN = (((0,), (0,)), ((), ()))
_NN = (((1,), (0,)), ((), ()))


def _dg(a, b, dn):
    return lax.dot_general(a, b, dn, preferred_element_type=F32)


def _dot_nt(a, b):
    return _dg(a.astype(BF16), b.astype(BF16), _NT)


def _dot_tn(a, b):
    return _dg(a.astype(BF16), b.astype(BF16), _TN)


def _split3(x):
    hi = x.astype(BF16)
    r1 = x - hi.astype(F32)
    mid = r1.astype(BF16)
    lo = (r1 - mid.astype(F32)).astype(BF16)
    return hi, mid, lo


def _split2(x):
    hi = x.astype(BF16)
    lo = (x - hi.astype(F32)).astype(BF16)
    return hi, lo


def _sel_dot(mask_bf, x):
    hi, mid, lo = _split3(x)
    return _mm(mask_bf, hi) + _mm(mask_bf, mid) + _mm(mask_bf, lo)


def _dot_sel(x, mask_bf):
    hi, mid, lo = _split3(x)
    return _mm(hi, mask_bf) + _mm(mid, mask_bf) + _mm(lo, mask_bf)


def _dot_sel_tn(x, mask_bf):
    hi, mid, lo = _split3(x)
    return _dg(hi, mask_bf, _TN) + _dg(mid, mask_bf, _TN) + _dg(lo, mask_bf, _TN)


def _mmx(a, b, dn=_NN):
    a_hi, a_lo = _split2(a)
    b_hi, b_lo = _split2(b)
    return _dg(a_hi, b_hi, dn) + _dg(a_hi, b_lo, dn) + _dg(a_lo, b_hi, dn)


def _iota(shape, dim):
    return lax.broadcasted_iota(jnp.int32, shape, dim)


def _sigmoid(x):
    return jax.nn.sigmoid(x)


def _softplus(x):
    return jnp.maximum(x, 0.0) + jnp.log1p(jnp.exp(-jnp.abs(x)))


def _log_sigmoid(x):
    return jnp.minimum(x, 0.0) - jnp.log1p(jnp.exp(-jnp.abs(x)))


def _gelu_tanh(x):
    return x * (0.5 * (1.0 + jnp.tanh(0.7978845608028654 * (x + 0.044715 * (x * x * x)))))


def _rms(x):
    return x * lax.rsqrt(jnp.mean(x * x, axis=-1, keepdims=True) + EPS)


def _adaln(x, shift, scale):
    return _rms(x) * (1.0 + scale) + shift


def _log2(n):
    k = n.bit_length() - 1
    assert (1 << k) == n
    return k


def _mod_spec(rows, width, layer, col):
    return pl.BlockSpec((None, None, rows, width), lambda g, i, n: (layer, g, 0, col))


def _mod_body(c_ref, w_ref, b_ref, o_ref):
    c = c_ref[...]
    cs = (c * _sigmoid(c)).astype(BF16)
    o_ref[...] = _mm(cs, w_ref[...].astype(BF16)) + b_ref[...]


def _modulation(c_all, w_mod, b_mod):
    rows = c_all.shape[0]
    tn = 1024
    return pl.pallas_call(
        _mod_body,
        grid=(DEPTH, N_MOD * D // tn),
        in_specs=[pl.BlockSpec((rows, D), lambda l, n: (0, 0)),
                  pl.BlockSpec((None, D, tn), lambda l, n: (l, 0, n)),
                  pl.BlockSpec((None, 1, tn), lambda l, n: (l, 0, n))],
        out_specs=pl.BlockSpec((None, rows, tn), lambda l, n: (l, 0, n)),
        out_shape=jax.ShapeDtypeStruct((DEPTH, rows, N_MOD * D), F32),
        compiler_params=_cp(("arbitrary", "arbitrary")),
        name="modulation",
    )(c_all, w_mod, b_mod.reshape(DEPTH, 1, N_MOD * D))


def _lower_bounds_body(g_ref, o_ref):
    g = g_ref[...]
    e = jnp.exp(g - jnp.max(g, axis=0, keepdims=True))
    sm = e / jnp.sum(e, axis=0, keepdims=True)
    acc = jnp.zeros_like(sm[0:1])
    for j in range(N_EVEN):
        acc = acc + sm[j:j + 1]
        o_ref[j:j + 1, :] = acc - sm[0:1]


def _lower_bounds(gamma):
    return pl.pallas_call(
        _lower_bounds_body,
        out_shape=jax.ShapeDtypeStruct(gamma.shape, F32),
        name="hgrn2_lower_bounds",
    )(gamma)


def _ffn_step(x_ref, sh_ref, sc_ref, gt_ref, wg_ref, wu_ref, wd_ref, o_ref, hf_ref, acc_ref):
    f = pl.program_id(2)

    @pl.when(f == 0)
    def _():
        hf_ref[...] = _adaln(x_ref[...], sh_ref[...], sc_ref[...]).astype(BF16)
        acc_ref[...] = jnp.zeros_like(acc_ref)

    hf = hf_ref[...]
    g = _mm(hf, wg_ref[...])
    u = _mm(hf, wu_ref[...])
    act = (g * _sigmoid(g) * u).astype(BF16)
    acc_ref[...] += _mm(act, wd_ref[...])

    @pl.when(f == pl.num_programs(2) - 1)
    def _():
        o_ref[...] = x_ref[...] + 0.5 * (1.0 + gt_ref[...]) * acc_ref[...]


def _ffn_body(x_ref, sh_ref, sc_ref, gt_ref, wg_ref, wu_ref, wd_ref, o_ref, hf_ref, acc_ref):
    _ffn_step(x_ref, sh_ref, sc_ref, gt_ref, wg_ref, wu_ref, wd_ref, o_ref, hf_ref, acc_ref)


def _ffn_cast_body(x_ref, sh_ref, sc_ref, gt_ref, wg_ref, wu_ref, wd_ref, o_ref, wgo_ref, wuo_ref, wdo_ref,
                   hf_ref, acc_ref):
    wgo_ref[...] = wg_ref[...].astype(BF16)
    wuo_ref[...] = wu_ref[...].astype(BF16)
    wdo_ref[...] = wd_ref[...].astype(BF16)
    _ffn_step(x_ref, sh_ref, sc_ref, gt_ref, wgo_ref, wuo_ref, wdo_ref, o_ref, hf_ref, acc_ref)


FFN_TF = 512


def _ffn_specs(x, mod, layer, which, tm):
    R = 1 if mod.shape[2] == 1 else tm
    kb = 6 * which
    return [pl.BlockSpec((None, tm, D), lambda g, i, f: (g, i, 0)),
            _mod_spec(R, D, layer, kb), _mod_spec(R, D, layer, kb + 1), _mod_spec(R, D, layer, kb + 2)]


def _ffn(x, mod, w_bf16, layer, which, tm):
    G, Lg, _ = x.shape
    tf = FFN_TF
    return pl.pallas_call(
        _ffn_body,
        grid=(G, Lg // tm, D_FF // tf),
        in_specs=_ffn_specs(x, mod, layer, which, tm) + [
            pl.BlockSpec((D, tf), lambda g, i, f: (0, f)), pl.BlockSpec((D, tf), lambda g, i, f: (0, f)),
            pl.BlockSpec((tf, D), lambda g, i, f: (f, 0))],
        out_specs=pl.BlockSpec((None, tm, D), lambda g, i, f: (g, i, 0)),
        out_shape=jax.ShapeDtypeStruct(x.shape, F32),
        scratch_shapes=[pltpu.VMEM((tm, D), BF16), pltpu.VMEM((tm, D), F32)],
        compiler_params=_cp(("arbitrary", "arbitrary", "arbitrary")),
        name="adaln_swiglu_ffn",
    )(x, mod, mod, mod, *w_bf16)


def _ffn_cast(x, mod, wg, wu, wd, layer, which):
    G, tm, _ = x.shape
    assert G == 1
    tf = FFN_TF // 2
    bf = lambda s: jax.ShapeDtypeStruct(s, BF16)
    outs = pl.pallas_call(
        _ffn_cast_body,
        grid=(1, 1, D_FF // tf),
        in_specs=_ffn_specs(x, mod, layer, which, tm) + [
            pl.BlockSpec((None, None, D, tf), lambda g, i, f: (layer, which, 0, f)),
            pl.BlockSpec((None, None, D, tf), lambda g, i, f: (layer, which, 0, f)),
            pl.BlockSpec((None, None, tf, D), lambda g, i, f: (layer, which, f, 0))],
        out_specs=[pl.BlockSpec((None, tm, D), lambda g, i, f: (g, i, 0)),
                   pl.BlockSpec((D, tf), lambda g, i, f: (0, f)), pl.BlockSpec((D, tf), lambda g, i, f: (0, f)),
                   pl.BlockSpec((tf, D), lambda g, i, f: (f, 0))],
        out_shape=[jax.ShapeDtypeStruct(x.shape, F32), bf((D, D_FF)), bf((D, D_FF)), bf((D_FF, D))],
        scratch_shapes=[pltpu.VMEM((tm, D), BF16), pltpu.VMEM((tm, D), F32)],
        compiler_params=_cp(("arbitrary", "arbitrary", "arbitrary")),
        name="adaln_swiglu_ffn_weight_cast",
    )(x, mod, mod, mod, wg, wu, wd)
    return outs[0], tuple(outs[1:])


def _inproj_body(x_ref, sh_ref, sc_ref, w_ref, o_ref, hf_ref, *, w_is_out_by_in):
    @pl.when(pl.program_id(2) == 0)
    def _():
        hf_ref[...] = _adaln(x_ref[...], sh_ref[...], sc_ref[...]).astype(BF16)

    o_ref[...] = _dg(hf_ref[...], w_ref[...], _NT if w_is_out_by_in else _NN)


def _inproj(x, mod, w, layer, j, tm, width=None, tn=512, w_is_out_by_in=False):
    G, Lg, _ = x.shape
    R = 1 if mod.shape[2] == 1 else tm
    n_out = w.shape[1] if w_is_out_by_in else w.shape[2]
    P = n_out if width is None else width
    assert P % tn == 0 and P <= n_out
    w_spec = (pl.BlockSpec((None, tn, D), lambda g, i, n: (j, n, 0)) if w_is_out_by_in
              else pl.BlockSpec((None, D, tn), lambda g, i, n: (j, 0, n)))
    return pl.pallas_call(
        functools.partial(_inproj_body, w_is_out_by_in=w_is_out_by_in),
        grid=(G, Lg // tm, P // tn),
        in_specs=[pl.BlockSpec((None, tm, D), lambda g, i, n: (g, i, 0)),
                  _mod_spec(R, D, layer, 3), _mod_spec(R, D, layer, 4), w_spec],
        out_specs=pl.BlockSpec((None, tm, tn), lambda g, i, n: (g, i, n)),
        out_shape=jax.ShapeDtypeStruct((G, Lg, P), F32),
        scratch_shapes=[pltpu.VMEM((tm, D), BF16)],
        compiler_params=_cp(("arbitrary", "arbitrary", "arbitrary")),
        name="adaln_in_projection",
    )(x, mod, mod, w)


def _outproj_body(x_ref, gt_ref, y1_ref, y2_ref, w1_ref, w2_ref, o_ref):
    y = _dot(y1_ref[...], w1_ref[...]) + _dot(y2_ref[...], w2_ref[...])
    o_ref[...] = x_ref[...] + (1.0 + gt_ref[...]) * y


def _outproj(x, mod, y1, y2, w, layer, j, tm):
    G, Lg, _ = x.shape
    R = 1 if mod.shape[2] == 1 else tm
    tn = D
    nb = D // tn
    half = y1.shape[-1]
    return pl.pallas_call(
        _outproj_body,
        grid=(G, Lg // tm, nb),
        in_specs=[pl.BlockSpec((None, tm, tn), lambda g, i, n: (g, i, n)),
                  pl.BlockSpec((None, None, R, tn), lambda g, i, n: (layer, g, 0, 5 * nb + n)),
                  pl.BlockSpec((None, tm, half), lambda g, i, n: (g, i, 0)),
                  pl.BlockSpec((None, tm, half), lambda g, i, n: (g, i, 0)),
                  pl.BlockSpec((None, None, half, tn), lambda g, i, n: (j, 0, 0, n)),
                  pl.BlockSpec((None, None, half, tn), lambda g, i, n: (j, 1, 0, n))],
        out_specs=pl.BlockSpec((None, tm, tn), lambda g, i, n: (g, i, n)),
        out_shape=jax.ShapeDtypeStruct(x.shape, F32),
        compiler_params=_cp(("arbitrary", "arbitrary", "arbitrary")),
        name="out_projection_residual",
    )(x, mod, y1, y2, w, w)


def _final_body(x_ref, w_ref, o_ref):
    o_ref[...] = _rms(x_ref[...]) * w_ref[...]


def _final_norm(x, w, tm):
    G, Lg, _ = x.shape
    return pl.pallas_call(
        _final_body,
        grid=(G, Lg // tm),
        in_specs=[pl.BlockSpec((None, tm, D), lambda g, i: (g, i, 0)),
                  pl.BlockSpec((1, D), lambda g, i: (0, 0))],
        out_specs=pl.BlockSpec((None, tm, D), lambda g, i: (g, i, 0)),
        out_shape=jax.ShapeDtypeStruct(x.shape, F32),
        compiler_params=_cp(("arbitrary", "arbitrary")),
        name="final_rmsnorm",
    )(x, w.reshape(1, D))


def _rglru_gates(x0, x1, x2, x3, cw, cb, wr, br, wi, bi, lam):
    u = cb + x0 * cw[0:1] + x1 * cw[1:2] + x2 * cw[2:3] + x3 * cw[3:4]
    r_parts, i_parts = [], []
    for n in range(NB_A):
        un = u[:, n * BS_A:(n + 1) * BS_A].astype(BF16)
        r_parts.append(_mm(un, wr[n]))
        i_parts.append(_mm(un, wi[n]))
    r = _sigmoid(jnp.concatenate(r_parts, axis=1) + br)
    ig = _sigmoid(jnp.concatenate(i_parts, axis=1) + bi)
    log_a = -C_RGLRU * r * _softplus(-lam)
    a = jnp.exp(log_a)
    b = jnp.sqrt(-jnp.tanh(log_a) * (a * a + 1.0)) * (ig * u)
    return a, b


def _rglru_seq_body(x_ref, halo_ref, buf_ref, cw_ref, cb_ref, wr_ref, br_ref, wi_ref, bi_ref, lam_ref,
                    a_ref, b_ref, *, tt):
    i = pl.program_id(1)
    halo = jnp.where(i == 0, buf_ref[...], halo_ref[...])
    full = jnp.concatenate([halo, x_ref[...]], axis=0)
    taps = [pltpu.roll(full, CONV_W - 1 - k, axis=0)[8:8 + tt] for k in range(CONV_W - 1)]
    a, b = _rglru_gates(taps[0], taps[1], taps[2], x_ref[...], cw_ref[...], cb_ref[...], wr_ref[...], br_ref[...],
                        wi_ref[...], bi_ref[...], lam_ref[...])
    a_ref[...] = a
    b_ref[...] = b


def _rglru_param_specs(nidx):
    z2 = lambda *_: (0, 0)
    z3 = lambda *_: (0, 0, 0)
    del nidx
    return [pl.BlockSpec((CONV_W, W_A), z2), pl.BlockSpec((1, W_A), z2),
            pl.BlockSpec((NB_A, BS_A, BS_A), z3), pl.BlockSpec((1, W_A), z2),
            pl.BlockSpec((NB_A, BS_A, BS_A), z3), pl.BlockSpec((1, W_A), z2),
            pl.BlockSpec((1, W_A), z2)]


def _rglru_seq(z, buf8, prm, tt):
    B, L, _ = z.shape
    hb = tt // 8
    out = jax.ShapeDtypeStruct((B, L, W_A), F32)
    return pl.pallas_call(
        functools.partial(_rglru_seq_body, tt=tt),
        grid=(B, L // tt),
        in_specs=[pl.BlockSpec((None, tt, W_A), lambda b, i: (b, i, 0)),
                  pl.BlockSpec((None, 8, W_A), lambda b, i: (b, jnp.maximum(i * hb - 1, 0), 0)),
                  pl.BlockSpec((None, 8, W_A), lambda b, i: (b, 0, 0))] + _rglru_param_specs(2),
        out_specs=[pl.BlockSpec((None, tt, W_A), lambda b, i: (b, i, 0))] * 2,
        out_shape=[out, out],
        compiler_params=_cp(("arbitrary", "arbitrary")),
        name="rglru_conv_gates",
    )(z, z, buf8, *prm)


def _rglru_scan_body(a_ref, b_ref, ag_ref, h0_ref, y_ref, hl_ref, h_scr, *, tt):
    @pl.when(pl.program_id(1) == 0)
    def _():
        h_scr[...] = h0_ref[...]

    def step(t, h):
        h = a_ref[t] * h + b_ref[t]
        y_ref[t] = h
        return h

    h = lax.fori_loop(0, tt, step, h_scr[...], unroll=8)
    h_scr[...] = h
    hl_ref[...] = h
    y_ref[...] = y_ref[...] * _gelu_tanh(ag_ref[...])


def _rglru_scan(a, b, z, h0, tt):
    B, L, _ = a.shape
    a4 = a.reshape(B, L, 8, LANES)
    b4 = b.reshape(B, L, 8, LANES)
    z4 = z.reshape(B, L, z.shape[-1] // LANES, LANES)
    spec = pl.BlockSpec((None, tt, 8, LANES), lambda bb, i: (bb, i, 0, 0))
    y, hl = pl.pallas_call(
        functools.partial(_rglru_scan_body, tt=tt),
        grid=(B, L // tt),
        in_specs=[spec, spec,
                  pl.BlockSpec((None, tt, 8, LANES), lambda bb, i: (bb, i, 1, 0)),
                  pl.BlockSpec((None, 8, LANES), lambda bb, i: (bb, 0, 0))],
        out_specs=[spec, pl.BlockSpec((None, 8, LANES), lambda bb, i: (bb, 0, 0))],
        out_shape=[jax.ShapeDtypeStruct((B, L, 8, LANES), F32), jax.ShapeDtypeStruct((B, 8, LANES), F32)],
        scratch_shapes=[pltpu.VMEM((8, LANES), F32)],
        compiler_params=_cp(("arbitrary", "arbitrary")),
        name="rglru_scan_gelu_gate",
    )(a4, b4, z4, h0.reshape(B, 8, LANES))
    return y.reshape(B, L, W_A), hl.reshape(B, W_A)


def _rglru_step_body(x3_ref, ag_ref, x0_ref, x1_ref, x2_ref, h0_ref, cw_ref, cb_ref, wr_ref, br_ref, wi_ref, bi_ref,
                     lam_ref, y_ref, h_ref):
    a, b = _rglru_gates(x0_ref[...], x1_ref[...], x2_ref[...], x3_ref[...], cw_ref[...], cb_ref[...], wr_ref[...],
                        br_ref[...], wi_ref[...], bi_ref[...], lam_ref[...])
    h = a * h0_ref[...] + b
    h_ref[...] = h
    y_ref[...] = h * _gelu_tanh(ag_ref[...])


def _rglru_step(z, buf, h0, prm):
    n = z.shape[0]
    row = lambda c: pl.BlockSpec((n, W_A), lambda i: (0, c))
    out = jax.ShapeDtypeStruct((n, W_A), F32)
    return pl.pallas_call(
        _rglru_step_body,
        grid=(1,),
        in_specs=[row(0), row(1), row(0), row(0), row(0), row(0)] + _rglru_param_specs(1),
        out_specs=[row(0), row(0)],
        out_shape=[out, out],
        compiler_params=_cp(("arbitrary",)),
        name="rglru_decode_step",
    )(z, z, buf[:, 0], buf[:, 1], buf[:, 2], h0, *prm)


def _live_rows(C, first, valid):
    row = _iota((C, 1), 0)
    return (row >= first) & (row < first + valid)


def _hgrn2_chunk(q, fp, v, gate, lb, nw, S, *, C, valid, first=0):
    SB = min(16, C)
    k = (1.0 - lb) * _sigmoid(-fp)
    lf = jnp.log1p(-k)
    if valid < C:
        live = _live_rows(C, first, valid)
        k = jnp.where(live, k, 0.0)
        lf = jnp.where(live, lf, 0.0)
    g_col = _dot_sel_tn(lf, jnp.ones((C, DV_B), BF16))
    if valid == 1:
        kv = _dot_tn(k, v)
        yield
        S_new = jnp.exp(g_col) * S + kv
        o = _dot(q, S_new)
        yield
        return _rms(o) * nw * (gate * _sigmoid(gate)), S_new
    tril = _iota((C, C), 0) >= _iota((C, C), 1)
    b = _sel_dot(jnp.where(tril, 1.0, 0.0).astype(BF16), lf)
    yield
    g_row = b[C - 1:C, :]
    o = _dot(q * jnp.exp(b), S)
    khat = k * jnp.exp(g_row - b)
    S_new = jnp.exp(g_col) * S + _dot_tn(khat, v)

    tril_sb = _iota((SB, SB), 0) >= _iota((SB, SB), 1)
    atts = []
    for i in range(C // SB):
        lo = i * SB
        qi, bi, ki = q[lo:lo + SB], b[lo:lo + SB], k[lo:lo + SB]
        ci = bi - jnp.log(ki)
        att = jnp.where(tril_sb, jnp.sum(qi[:, None, :] * jnp.exp(bi[:, None, :] - ci[None, :, :]), axis=-1), 0.0)
        off = None
        if i > 0:
            ref = b[lo - 1:lo, :]
            qt = qi * jnp.exp(bi - ref)
            kt = k[0:lo] * jnp.exp(ref - b[0:lo])
            off = _dot_nt(qt, kt)
        atts.append((att, off))
    yield
    parts = []
    for i, (att, off) in enumerate(atts):
        lo = i * SB
        oi = _dot(att, v[lo:lo + SB])
        parts.append(oi if off is None else oi + _dot(off, v[0:lo]))
    yield
    o = o + (parts[0] if len(parts) == 1 else jnp.concatenate(parts, axis=0))
    return _rms(o) * nw * (gate * _sigmoid(gate)), S_new


def _merge_rows(outs, C):
    row = _iota((C, 1), 0)
    y = jnp.where(row == 0, outs[0], 0.0)
    for s in range(1, len(outs)):
        y = jnp.where(row == s, outs[s], y)
    return y


def _chained(prev, out_index, operands, specs):
    if prev is None:
        return {}
    operands.append(prev)
    specs.append(pl.BlockSpec(memory_space=pl.ANY))
    return {len(operands) - 1: out_index}


def _hgrn2_body(q_ref, f_ref, v_ref, g_ref, lb_ref, nw_ref, s0_ref, *rest, C, valid, nb, hp, decode, nchunk):
    y_ref, s1_ref = rest[-2:]

    @pl.when(pl.program_id(2) == 0)
    def _():
        s1_ref[...] = s0_ref[...]

    nw = nw_ref[...]
    probs = [(bi, hh) for bi in range(nb) for hh in range(hp)]

    def chunk(c, carry):
        rows = pl.ds(pl.multiple_of(c * C, C), C)
        gens = []
        for bi, hh in probs:
            lanes = slice(hh * LANES, (hh + 1) * LANES)
            src = 0 if decode else bi
            gens.append(_hgrn2_chunk(q_ref[src, rows, lanes], f_ref[src, rows, lanes], v_ref[src, rows, lanes],
                                     g_ref[src, rows, lanes], lb_ref[:, lanes], nw, s1_ref[bi, hh],
                                     C=C, valid=valid, first=bi if decode else 0))
        outs = _interleave(gens)
        for (bi, hh), (y, s_new) in zip(probs, outs):
            s1_ref[bi, hh] = s_new
            if not decode:
                y_ref[bi, rows, hh * LANES:(hh + 1) * LANES] = y
        if decode:
            for hh in range(hp):
                y_ref[0, rows, hh * LANES:(hh + 1) * LANES] = _merge_rows([outs[bi * hp + hh][0] for bi in range(nb)], C)
        return carry

    lax.fori_loop(0, nchunk, chunk, 0)


def _hgrn2(z, lb, nw, s_all, s_prev, j, *, C, valid, nb, hp, tt, decode):
    B, L, _ = z.shape
    zb = 1 if decode else nb
    w = hp * LANES
    col = lambda base: pl.BlockSpec((zb, tt, w), lambda b, h, i: (b, i, base // hp + h))
    st = pl.BlockSpec((None, nb, hp, DK_B, DV_B), lambda b, h, i: (j, b, h, 0, 0))
    operands = [z, z, z, z, lb, nw, s_all]
    specs = [col(16), col(24), col(32), col(40), pl.BlockSpec((1, w), lambda b, h, i: (0, h)),
             pl.BlockSpec((1, LANES), lambda b, h, i: (0, 0)), st]
    aliases = _chained(s_prev, 1, operands, specs)
    return pl.pallas_call(
        functools.partial(_hgrn2_body, C=C, valid=valid, nb=nb, hp=hp, decode=decode, nchunk=tt // C),
        grid=(B // zb, H_B // hp, L // tt),
        in_specs=specs,
        out_specs=[col(0), st],
        out_shape=[jax.ShapeDtypeStruct((B, L, H_B * DV_B), F32), jax.ShapeDtypeStruct(s_all.shape, F32)],
        input_output_aliases=aliases,
        compiler_params=_cp(("arbitrary", "arbitrary", "arbitrary")),
        name="hgrn2_chunked",
    )(*operands)


def _mlstm_step(q, k, v, op, gt, h, nw, Cst, n, m, *, C, first):
    lane = _iota((C, LANES), 1)
    live = _live_rows(C, first, 1)
    i0 = jnp.sum(jnp.where(live & (lane == h), gt, 0.0), keepdims=True)
    lf0 = _log_sigmoid(jnp.sum(jnp.where(live & (lane == h + H_C), gt, 0.0), keepdims=True))
    inter = lf0 + m
    m_t = jnp.maximum(inter, i0)
    w_inter = jnp.exp(inter - m_t)
    wk = jnp.exp(i0 - m_t)
    kh = k * (DK_C ** -0.5)
    s = jnp.sum(q * kh, axis=1, keepdims=True) * wk
    qc = _dot(q, Cst)
    kw = jnp.where(live, kh * wk, 0.0)
    kv = _dot_tn(kw, v)
    yield
    num = w_inter * qc + s * v
    den = w_inter * jnp.sum(q * n, axis=1, keepdims=True) + s
    hh = num / jnp.maximum(jnp.abs(den), jnp.exp(-m_t))
    C_new = w_inter * Cst + kv
    n_new = w_inter * n + jnp.sum(kw, axis=0, keepdims=True)
    y = _rms(hh) * nw * _sigmoid(op)
    return y, C_new, n_new, m_t


def _mlstm_chunk(q, k, v, op, gt, h, nw, Cst, n, m, *, C, valid, first=0):
    if valid == 1:
        return (yield from _mlstm_step(q, k, v, op, gt, h, nw, Cst, n, m, C=C, first=first))
    assert first == 0
    gtT = gt.T
    lane = _iota((C, LANES), 1)
    sub = _iota((LANES, C), 0)
    i_col = jnp.sum(jnp.where(lane == h, gt, 0.0), axis=1, keepdims=True)
    f_col = jnp.sum(jnp.where(lane == h + H_C, gt, 0.0), axis=1, keepdims=True)
    i_row = jnp.sum(jnp.where(sub == h, gtT, 0.0), axis=0, keepdims=True)
    f_row = jnp.sum(jnp.where(sub == h + H_C, gtT, 0.0), axis=0, keepdims=True)
    lf_col = _log_sigmoid(f_col)
    lf_row = _log_sigmoid(f_row)
    if valid < C:
        live_c = _iota((C, 1), 0) < valid
        live_r = _iota((1, C), 1) < valid
        i_col = jnp.where(live_c, i_col, NEG_BIG)
        i_row = jnp.where(live_r, i_row, NEG_BIG)
        lf_col = jnp.where(live_c, lf_col, 0.0)
        lf_row = jnp.where(live_r, lf_row, 0.0)
    tril = _iota((C, C), 0) >= _iota((C, C), 1)
    b_col = jnp.sum(jnp.where(tril, lf_row, 0.0), axis=1, keepdims=True)
    b_row = jnp.sum(jnp.where(_iota((C, C), 0) <= _iota((C, C), 1), lf_col, 0.0), axis=0, keepdims=True)
    dm = jnp.where(tril, b_col - b_row + i_row, NEG_BIG)
    inter = b_col + m
    m_t = jnp.maximum(inter, jnp.max(dm, axis=1, keepdims=True))
    w_inter = jnp.exp(inter - m_t)
    kh = k * (DK_C ** -0.5)
    qk = _dot_nt(q, kh)
    qc = _dot(q, Cst)
    g = b_col[C - 1:C, :]
    m_new = m_t[C - 1:C, :]
    wk = jnp.exp(g - b_col + i_col - m_new)
    f_state = jnp.exp(g + m - m_new)
    kw = kh * wk
    kv = _dot_tn(kw, v)
    yield
    s = qk * jnp.exp(dm - m_t)
    sv = _dot(s, v)
    yield
    num = w_inter * qc + sv
    den = w_inter * jnp.sum(q * n, axis=1, keepdims=True) + jnp.sum(s, axis=1, keepdims=True)
    hh = num / jnp.maximum(jnp.abs(den), jnp.exp(-m_t))
    C_new = f_state * Cst + kv
    n_new = f_state * n + jnp.sum(kw, axis=0, keepdims=True)
    y = _rms(hh) * nw * _sigmoid(op)
    return y, C_new, n_new, m_new


def _mlstm_body(q_ref, k_ref, v_ref, o_ref, g_ref, gb_ref, nw_ref, c0_ref, n0_ref, m0_ref, *rest,
                C, valid, nb, decode, nchunk):
    y_ref, c1_ref, n1_ref, m1_ref = rest[-4:]

    @pl.when(pl.program_id(2) == 0)
    def _():
        c1_ref[...] = c0_ref[...]
        n1_ref[...] = n0_ref[...]
        m1_ref[...] = m0_ref[...]

    h = pl.program_id(1)
    gb = gb_ref[...]
    nw = nw_ref[...]

    def chunk(c, carry):
        rows = pl.ds(pl.multiple_of(c * C, C), C)
        gens = []
        for bi in range(nb):
            src = 0 if decode else bi
            gens.append(_mlstm_chunk(
                q_ref[src, rows, :], k_ref[src, rows, :], v_ref[src, rows, :], o_ref[src, rows, :],
                g_ref[src, rows, :] + gb, h, nw, c1_ref[bi], n1_ref[bi], m1_ref[bi][:, 0:1],
                C=C, valid=valid, first=bi if decode else 0))
        outs = _interleave(gens)
        for bi, (y, c_new, n_new, m_new) in enumerate(outs):
            if not decode:
                y_ref[bi, rows, :] = y
            c1_ref[bi] = c_new
            n1_ref[bi] = n_new
            m1_ref[bi] = jnp.broadcast_to(m_new, (1, LANES))
        if decode:
            y_ref[0, rows, :] = _merge_rows([o[0] for o in outs], C)
        return carry

    lax.fori_loop(0, nchunk, chunk, 0)


def _mlstm(z, gb, nw, states, prev, j, *, C, valid, nb, tt, decode):
    B, L, _ = z.shape
    zb = 1 if decode else nb
    c128 = lambda base: pl.BlockSpec((zb, tt, LANES), lambda b, h, i: (b, i, base + h))
    c256 = lambda base: pl.BlockSpec((zb, tt, DV_C), lambda b, h, i: (b, i, base + h))
    cst = pl.BlockSpec((None, nb, None, DK_C, DV_C), lambda b, h, i: (j, b, h, 0, 0))
    vec = pl.BlockSpec((None, nb, None, 1, LANES), lambda b, h, i: (j, b, h, 0, 0))
    operands = [z, z, z, z, z, gb, nw, *states]
    specs = [c128(0), c128(4), c256(4), c256(8), pl.BlockSpec((zb, tt, LANES), lambda b, h, i: (b, i, 24)),
             pl.BlockSpec((1, LANES), lambda b, h, i: (0, 0)), pl.BlockSpec((1, DV_C), lambda b, h, i: (0, 0)),
             cst, vec, vec]
    aliases = {}
    for k, p in enumerate(prev or ()):
        aliases.update(_chained(p, 1 + k, operands, specs))
    outs = pl.pallas_call(
        functools.partial(_mlstm_body, C=C, valid=valid, nb=nb, decode=decode, nchunk=tt // C),
        grid=(B // zb, H_C, L // tt),
        in_specs=specs,
        out_specs=[c256(0), cst, vec, vec],
        out_shape=[jax.ShapeDtypeStruct((B, L, H_C * DV_C), F32)] + [jax.ShapeDtypeStruct(s.shape, F32) for s in states],
        input_output_aliases=aliases,
        compiler_params=_cp(("arbitrary", "arbitrary", "arbitrary")),
        name="mlstm_chunked",
    )(*operands)
    return outs[0], tuple(outs[1:])


def _head_sum(x, bd):
    return jnp.concatenate([_dot_sel(x[:, n * LANES:(n + 1) * LANES], bd) for n in range(W_D // LANES)], axis=1)


def _rwkv_pre(zd, sh, mu, w0, a0, kkw, ka, rk, w2p, a2p, g2, bd):
    zs = zd + (sh - zd) * mu
    r, k, v = zs[:, 0:W_D], zs[:, W_D:2 * W_D], zs[:, 2 * W_D:3 * W_D]
    wa = zs[:, 3 * W_D:3 * W_D + LANES]
    gl = zs[:, 3 * W_D + LANES:]
    w = -_softplus(-(w0 + _dot(jnp.tanh(wa), w2p))) - 0.5
    logd = -jnp.exp(w)
    a = _sigmoid(a0 + _dot(wa, a2p))
    gate = _dot(_sigmoid(gl), g2)
    kk = k * kkw
    kk = kk / jnp.maximum(jnp.sqrt(_head_sum(kk * kk, bd)), 1e-12)
    kmod = k * (1.0 + (a - 1.0) * ka)
    bonus = _head_sum(r * kmod * rk, bd) * v
    return r, logd, kmod, v, kk, kk * a, gate, bonus


def _head_block_ones():
    return jnp.where((_iota((LANES, LANES), 0) >> 6) == (_iota((LANES, LANES), 1) >> 6), 1.0, 0.0).astype(BF16)


def _rwkv_pre_seq_body(z_ref, halo_ref, prev_ref, mu_ref, w0_ref, a0_ref, kk_ref, ka_ref, rk_ref, w2_ref, a2_ref,
                       g2_ref, *out_refs, tt):
    i = pl.program_id(1)
    halo = jnp.where(i == 0, prev_ref[...], halo_ref[...])
    zd = z_ref[...]
    sh = pltpu.roll(jnp.concatenate([halo, zd], axis=0), 1, axis=0)[8:8 + tt]
    outs = _rwkv_pre(zd, sh, mu_ref[...], w0_ref[...], a0_ref[...], kk_ref[...], ka_ref[...], rk_ref[...],
                     w2_ref[...], a2_ref[...], g2_ref[...], _head_block_ones())
    for ref, val in zip(out_refs, outs):
        ref[...] = val


def _rwkv_pre_step_body(z_ref, sh_ref, mu_ref, w0_ref, a0_ref, kk_ref, ka_ref, rk_ref, w2_ref, a2_ref, g2_ref,
                        *out_refs):
    outs = _rwkv_pre(z_ref[...], sh_ref[...], mu_ref[...], w0_ref[...], a0_ref[...], kk_ref[...], ka_ref[...],
                     rk_ref[...], w2_ref[...], a2_ref[...], g2_ref[...], _head_block_ones())
    for ref, val in zip(out_refs[:6], outs[:6]):
        ref[...] = val.T
    for ref, val in zip(out_refs[6:], outs[6:]):
        ref[...] = val


def _rwkv_pre_param_specs():
    z2 = lambda *_: (0, 0)
    return ([pl.BlockSpec((1, P_D), z2)] + [pl.BlockSpec((1, W_D), z2)] * 5
            + [pl.BlockSpec((LANES, W_D), z2)] * 3)


def _rwkv_pre_seq(z, prev8, prm, tt):
    B, L, _ = z.shape
    hb = tt // 8
    out = jax.ShapeDtypeStruct((B, L, W_D), F32)
    return pl.pallas_call(
        functools.partial(_rwkv_pre_seq_body, tt=tt),
        grid=(B, L // tt),
        in_specs=[pl.BlockSpec((None, tt, P_D), lambda b, i: (b, i, 0)),
                  pl.BlockSpec((None, 8, P_D), lambda b, i: (b, jnp.maximum(i * hb - 1, 0), 0)),
                  pl.BlockSpec((None, 8, P_D), lambda b, i: (b, 0, 0))] + _rwkv_pre_param_specs(),
        out_specs=[pl.BlockSpec((None, tt, W_D), lambda b, i: (b, i, 0))] * 8,
        out_shape=[out] * 8,
        compiler_params=_cp(("arbitrary", "arbitrary")),
        name="rwkv7_token_shift_features",
    )(z, z, prev8, *prm)


def _rwkv_pre_step(z, shifted, prm):
    n = z.shape[0]
    fm, tm = jax.ShapeDtypeStruct((W_D, n), F32), jax.ShapeDtypeStruct((n, W_D), F32)
    whole = lambda s: pl.BlockSpec(s.shape, lambda i: (0, 0))
    outs = pl.pallas_call(
        _rwkv_pre_step_body,
        grid=(1,),
        in_specs=[pl.BlockSpec((n, P_D), lambda i: (0, 0)), pl.BlockSpec((n, P_D), lambda i: (0, 0))]
        + _rwkv_pre_param_specs(),
        out_specs=[whole(fm)] * 6 + [whole(tm)] * 2,
        out_shape=[fm] * 6 + [tm] * 2,
        compiler_params=_cp(("arbitrary",)),
        name="rwkv7_decode_features",
    )(z, shifted, *prm)
    return outs[:6], outs[6], outs[7]


def _unit_lower_inverse(N, C):
    n = N.shape[0]
    ri, ci = _iota((n, n), 0), _iota((n, n), 1)
    base = min(8, C)
    kb = _log2(base)
    X = jnp.where((ri >> kb) == (ci >> kb), N, 0.0)
    T = jnp.where(ri == ci, 1.0, 0.0) + X
    for _ in range(kb - 1):
        X = _dot(X, X)
        yield
        T = T + _dot(T, X)
        yield
    size = base
    while size < C:
        ks = _log2(size)
        sel = ((ri >> (ks + 1)) == (ci >> (ks + 1))) & (((ri >> ks) & 1) == 1) & (((ci >> ks) & 1) == 0)
        TN_ = _dot(T, jnp.where(sel, N, 0.0))
        yield
        T = T + _dot(TN_, T)
        yield
        size *= 2
    return T


def _interleave(gens):
    results = [None] * len(gens)
    live = list(range(len(gens)))
    while live:
        for i in list(live):
            try:
                next(gens[i])
            except StopIteration as stop:
                results[i] = stop.value
                live.remove(i)
    return results


def _rwkv_groupnorm_gate(y, gate, bonus, lnw, lnb):
    bd = _head_block_ones()
    mean = _dot_sel(y, bd) * (1.0 / N_D)
    yield
    dy = y - mean
    var = _dot_sel(dy * dy, bd) * (1.0 / N_D)
    yield
    return (dy * lax.rsqrt(var + GN_EPS_D) * lnw + lnb + bonus) * gate


def _rwkv_chunk(r, ld, k, v, kk, be, gate, bonus, lnw, lnb, P, *, C, valid, first=0):
    if valid < C:
        live = _live_rows(C, first, valid)
        ld = jnp.where(live, ld, 0.0)
        be = jnp.where(live, be, 0.0)
        k = jnp.where(live, k, 0.0)
        v = jnp.where(live, v, 0.0)
    tril = _iota((C, C), 0) >= _iota((C, C), 1)
    c = _sel_dot(jnp.where(tril, 1.0, 0.0).astype(BF16), ld)
    g_col = _dot_sel_tn(ld, jnp.ones((C, LANES), BF16))
    yield
    ec, enc = jnp.exp(c), jnp.exp(-c)
    lane0 = _iota((C, LANES), 1) < N_D

    def stack(x):
        return jnp.concatenate([jnp.where(lane0, x, 0.0), jnp.where(lane0, 0.0, x)], axis=0)

    n2 = 2 * C
    AR = jnp.concatenate([stack(-kk * jnp.exp(c - ld)), stack(r * ec)], axis=0).astype(BF16)
    BK = jnp.concatenate([stack(be * enc), stack(k * enc)], axis=0).astype(BF16)
    V2 = stack(v).astype(BF16)
    kc = _log2(C)
    ri, ci = _iota((2 * n2, 2 * n2), 0), _iota((2 * n2, 2 * n2), 1)
    tpos, spos = ri & (C - 1), ci & (C - 1)
    keep = (((ri >> kc) & 1) == ((ci >> kc) & 1)) & ((spos < tpos) | ((ri >= n2) & (spos == tpos)))
    G = jnp.where(keep, _dg(AR, BK, _NT), 0.0)
    LP = _mm(AR, P.astype(BF16))
    yield
    GV = _mm(G[:, n2:].astype(BF16), V2)
    T = yield from _unit_lower_inverse(G[0:n2, 0:n2], C)
    U = _dot(T, LP[0:n2] + GV[0:n2])
    yield
    Y2 = LP[n2:] + GV[n2:] + _dot(G[n2:, 0:n2], U)
    P_new = jnp.exp(g_col) * (P + _dg(BK, jnp.concatenate([U.astype(BF16), V2], axis=0), _TN))
    yield
    y = Y2[0:C] + Y2[C:n2]
    out = yield from _rwkv_groupnorm_gate(y, gate, bonus, lnw, lnb)
    return out, P_new


def _rwkv_body(r_ref, ld_ref, k_ref, v_ref, kk_ref, be_ref, gt_ref, bo_ref, lnw_ref, lnb_ref, p0_ref,
               y_ref, p1_ref, *, C, valid, nb, hp, nchunk):
    @pl.when(pl.program_id(2) == 0)
    def _():
        p1_ref[...] = p0_ref[...]

    probs = [(bi, pp) for bi in range(nb) for pp in range(hp)]
    in_refs = (r_ref, ld_ref, k_ref, v_ref, kk_ref, be_ref, gt_ref, bo_ref)

    def chunk(c, carry):
        rows = pl.ds(pl.multiple_of(c * C, C), C)
        gens = []
        for bi, pp in probs:
            lanes = slice(pp * LANES, (pp + 1) * LANES)
            args = [ref[bi, rows, lanes] for ref in in_refs] + [lnw_ref[:, lanes], lnb_ref[:, lanes], p1_ref[bi, pp]]
            gens.append(_rwkv_chunk(*args, C=C, valid=valid))
        for (bi, pp), (y, p_new) in zip(probs, _interleave(gens)):
            p1_ref[bi, pp] = p_new
            y_ref[bi, rows, pp * LANES:(pp + 1) * LANES] = y
        return carry

    lax.fori_loop(0, nchunk, chunk, 0)


def _rwkv_decode_body(r_ref, ld_ref, k_ref, v_ref, kk_ref, be_ref, s0_ref, *rest):
    y_ref, s1_ref = rest[-2:]
    S = s0_ref[...]
    sa = -jnp.sum(S * kk_ref[...][None], axis=1)
    S = S * jnp.exp(ld_ref[...])[None] + sa[:, None, :] * be_ref[...][None] + v_ref[...][:, None, :] * k_ref[...][None]
    s1_ref[...] = S
    y_ref[...] = jnp.sum(S * r_ref[...][None], axis=1)


def _rwkv_decode(feats_fm, s_all, s_prev, j):
    n = feats_fm[0].shape[1]
    row = pl.BlockSpec((N_D, n), lambda h: (h, 0))
    st = pl.BlockSpec((None, None, N_D, N_D, n), lambda h: (j, h, 0, 0, 0))
    operands = [*feats_fm, s_all]
    specs = [row] * 6 + [st]
    aliases = _chained(s_prev, 1, operands, specs)
    return pl.pallas_call(
        _rwkv_decode_body,
        grid=(H_D,),
        in_specs=specs,
        out_specs=[row, st],
        out_shape=[jax.ShapeDtypeStruct((W_D, n), F32), jax.ShapeDtypeStruct(s_all.shape, F32)],
        input_output_aliases=aliases,
        compiler_params=_cp(("arbitrary",)),
        name="rwkv7_decode_step",
    )(*operands)


def _rwkv_decode_out_body(y_ref, gt_ref, bo_ref, lnw_ref, lnb_ref, o_ref):
    y = y_ref[...].T
    bd = _head_block_ones()
    dy = y - _head_sum(y, bd) * (1.0 / N_D)
    var = _head_sum(dy * dy, bd) * (1.0 / N_D)
    o_ref[...] = (dy * lax.rsqrt(var + GN_EPS_D) * lnw_ref[...] + lnb_ref[...] + bo_ref[...]) * gt_ref[...]


def _rwkv_decode_out(y_fm, gate, bonus, lnw, lnb):
    n = y_fm.shape[1]
    return pl.pallas_call(
        _rwkv_decode_out_body,
        out_shape=jax.ShapeDtypeStruct((n, W_D), F32),
        name="rwkv7_decode_groupnorm_gate",
    )(y_fm, gate, bonus, lnw, lnb)


def _rwkv(feats, lnw, lnb, p0, *, C, valid, nb, hp, tt):
    B, L, _ = feats[0].shape
    w = hp * LANES
    col = pl.BlockSpec((nb, tt, w), lambda b, p, i: (b, i, p))
    vec = pl.BlockSpec((1, w), lambda b, p, i: (0, p))
    st = pl.BlockSpec((nb, hp, LANES, LANES), lambda b, p, i: (b, p, 0, 0))
    return pl.pallas_call(
        functools.partial(_rwkv_body, C=C, valid=valid, nb=nb, hp=hp, nchunk=tt // C),
        grid=(B // nb, H_D // 2 // hp, L // tt),
        in_specs=[col] * 8 + [vec, vec, st],
        out_specs=[col, st],
        out_shape=[jax.ShapeDtypeStruct((B, L, W_D), F32), jax.ShapeDtypeStruct(p0.shape, F32)],
        compiler_params=_cp(("arbitrary", "arbitrary", "arbitrary")),
        name="rwkv7_chunked",
    )(*feats, lnw, lnb, p0)


def _pair_blockdiag(s):
    B = s.shape[0]
    st = s.reshape(B, H_D // 2, 2, N_D, N_D)
    zero = jnp.zeros_like(st[:, :, 0])
    top = jnp.concatenate([st[:, :, 0], zero], axis=-1)
    bot = jnp.concatenate([zero, st[:, :, 1]], axis=-1)
    return jnp.concatenate([top, bot], axis=-2)


def _pair_unblockdiag(p):
    B = p.shape[0]
    st = jnp.stack([p[:, :, :N_D, :N_D], p[:, :, N_D:, N_D:]], axis=2)
    return st.reshape(B, H_D, N_D, N_D)


def _trunk(x, mod, st, prm, *, decode):
    G, Lg, _ = x.shape
    tm = min(128 if decode else 512, Lg)
    tm_in = tm if decode or Lg % 1024 else 1024
    assert decode or Lg % 512 == 0
    nseq = Lg if decode else G
    new = {k: [] for k in ("a_conv", "a_h", "d_shift", "d_s")}
    b_s = None
    c_in = (st["c_c"], st["c_n"][:, :, :, None, :],
            jnp.broadcast_to(st["c_m"][..., None, None], st["c_m"].shape + (1, LANES)))
    c_out = d_s = None
    d_s_in = jnp.transpose(st["d_s"], (0, 2, 3, 4, 1)) if decode else None
    blocks = lambda t: t.reshape(nseq // DEC_NB, DEC_NB, -1)
    ffn_bf16 = {} if decode else prm["ffn_bf16"]

    def ffn(x, l, which):
        if decode:
            assert Lg == tm
            x, ffn_bf16[l, which] = _ffn_cast(x, mod, prm["ffn_w_gate"], prm["ffn_w_up"], prm["ffn_w_down"], l, which)
            return x
        return _ffn(x, mod, ffn_bf16[l, which], l, which, tm)

    for l in range(DEPTH):
        j = l // 2
        x = ffn(x, l, 0)
        if l % 2 == 0:
            z = _inproj(x, mod, prm["w_in_even"], l, j, tm_in)
            a_prm = prm["rglru"][j]
            conv0, h0 = st["a_conv"][j], st["a_h"][j]
            if decode:
                z2 = z[0]
                y1, h1 = _rglru_step(z2, conv0, h0, a_prm)
                conv1 = jnp.concatenate([conv0[:, 1:], z2[:, None, :W_A]], axis=1)
                yb, b_s = _hgrn2(blocks(z2), prm["lb"][j], prm["b_norm_w"][j], st["b_s"], b_s, j,
                                 C=DEC_NB, valid=1, nb=DEC_NB, hp=1, tt=DEC_NB, decode=True)
                y1, y2 = y1[None], yb.reshape(1, nseq, -1)
            else:
                buf8 = jnp.concatenate([jnp.zeros((nseq, 8 - (CONV_W - 1), W_A), F32), conv0], axis=1)
                a, b = _rglru_seq(z, buf8, a_prm, 512)
                y1, h1 = _rglru_scan(a, b, z, h0, 512)
                conv1 = z[:, Lg - (CONV_W - 1):, :W_A]
                y2, b_s = _hgrn2(z, prm["lb"][j], prm["b_norm_w"][j], st["b_s"], b_s, j,
                                 C=CHUNK, valid=CHUNK, nb=nseq, hp=2, tt=512, decode=False)
            new["a_conv"].append(conv1)
            new["a_h"].append(h1)
            x = _outproj(x, mod, y1, y2, prm["w_out_even"], l, j, tm)
        else:
            z = _inproj(x, mod, prm["w_in_odd"], l, j, tm_in, width=MLSTM_COLS, w_is_out_by_in=True)
            zd = _inproj(x, mod, prm["w_in_rwkv"], l, j, tm, tn=P_D // 2, w_is_out_by_in=True)
            d_prm = prm["rwkv"][j]
            if decode:
                yc, c_out = _mlstm(blocks(z[0]), prm["c_gate_b"][j], prm["c_norm_w"][j], c_in, c_out, j,
                                   C=DEC_NB, valid=1, nb=DEC_NB, tt=DEC_NB, decode=True)
                feats_fm, gate, bonus = _rwkv_pre_step(zd[0], st["d_shift"][j], d_prm)
                y_fm, d_s = _rwkv_decode(feats_fm, d_s_in, d_s, j)
                yd = _rwkv_decode_out(y_fm, gate, bonus, prm["d_ln_w"][j], prm["d_ln_b"][j])
                y1, y2 = yc.reshape(1, nseq, -1), yd[None]
                shift1 = zd[0]
            else:
                y1, c_out = _mlstm(z, prm["c_gate_b"][j], prm["c_norm_w"][j], c_in, c_out, j,
                                   C=CHUNK, valid=CHUNK, nb=nseq, tt=256, decode=False)
                prev8 = jnp.concatenate([jnp.zeros((nseq, 7, P_D), F32), st["d_shift"][j][:, None]], axis=1)
                feats = _rwkv_pre_seq(zd, prev8, d_prm, 128)
                p0 = _pair_blockdiag(jnp.swapaxes(st["d_s"][j], -1, -2))
                y2, p1 = _rwkv(feats, prm["d_ln_w"][j], prm["d_ln_b"][j], p0, C=CHUNK, valid=CHUNK, nb=nseq, hp=2,
                               tt=256)
                shift1 = zd[:, -1]
                new["d_s"].append(jnp.swapaxes(_pair_unblockdiag(p1), -1, -2))
            new["d_shift"].append(shift1)
            x = _outproj(x, mod, y1, y2, prm["w_out_odd"], l, j, tm)
        x = ffn(x, l, 1)
    y = _final_norm(x, prm["final_norm_w"], tm)
    d_s = jnp.transpose(d_s, (0, 4, 1, 2, 3)) if decode else jnp.stack(new["d_s"])
    outs = (y, jnp.stack(new["a_conv"]), jnp.stack(new["a_h"]), b_s, c_out[0], c_out[1][:, :, :, 0, :],
            c_out[2][:, :, :, 0, 0], jnp.stack(new["d_shift"]), d_s)
    return outs, ffn_bf16


def _prepare(w_mod, b_mod, ffn_w_gate, ffn_w_up, ffn_w_down, w_in_even, w_out_even, a_conv_w, a_conv_b, a_gate_r_w,
             a_gate_r_b, a_gate_i_w, a_gate_i_b, a_lambda, b_lb_gamma, b_norm_w, w_in_odd, w_out_odd, c_igate_b,
             c_fgate_b, c_norm_w, d_mu, d_w0, d_w2, d_a0, d_a2, d_g2, d_k_k, d_k_a, d_r_k, d_ln_w, d_ln_b,
             final_norm_w):
    n_gate = 2 * H_C
    w_odd_t = jnp.swapaxes(w_in_odd, 1, 2).astype(BF16)
    row = lambda t: t[:, None, :]
    zpad = jnp.zeros((N_ODD, LANES - n_gate), F32)
    half = jnp.zeros((N_ODD, R_W, W_D), F32)
    rglru = [(a_conv_w[j], a_conv_b[j][None], a_gate_r_w[j].astype(BF16), a_gate_r_b[j][None],
              a_gate_i_w[j].astype(BF16), a_gate_i_b[j][None], a_lambda[j][None]) for j in range(N_EVEN)]
    w2p = jnp.concatenate([d_w2, half], axis=1).astype(BF16)
    a2p = jnp.concatenate([half, d_a2], axis=1).astype(BF16)
    g2 = d_g2.astype(BF16)
    rwkv = [(d_mu[j][None], d_w0[j][None], d_a0[j][None], d_k_k[j][None], d_k_a[j][None],
             d_r_k[j].reshape(1, W_D), w2p[j], a2p[j], g2[j]) for j in range(N_ODD)]
    return dict(
        ffn_w_gate=ffn_w_gate, ffn_w_up=ffn_w_up, ffn_w_down=ffn_w_down,
        w_in_even=w_in_even.astype(BF16), w_in_odd=w_odd_t, w_in_rwkv=w_odd_t[:, P_ODD - P_D:],
        w_out_even=w_out_even.astype(BF16).reshape(N_EVEN, 2, D // 2, D),
        w_out_odd=w_out_odd.astype(BF16).reshape(N_ODD, 2, D // 2, D),
        rglru=rglru, rwkv=rwkv,
        lb=row(_lower_bounds(b_lb_gamma)), b_norm_w=row(b_norm_w),
        c_gate_b=row(jnp.concatenate([c_igate_b, c_fgate_b, zpad], axis=-1)), c_norm_w=row(c_norm_w),
        d_ln_w=row(d_ln_w), d_ln_b=row(d_ln_b), final_norm_w=final_norm_w)


def kernel(x_prompt, x_sample, c_prompt, c_sample, state_a_conv, state_a_h, state_b_s, state_c_c, state_c_n, state_c_m, state_d_shift, state_d_s, w_mod, b_mod, ffn_w_gate, ffn_w_up, ffn_w_down, w_in_even, w_out_even, a_conv_w, a_conv_b, a_gate_r_w, a_gate_r_b, a_gate_i_w, a_gate_i_b, a_lambda, b_lb_gamma, b_norm_w, w_in_odd, w_out_odd, c_igate_b, c_fgate_b, c_norm_w, d_mu, d_w0, d_w2, d_a0, d_a2, d_g2, d_k_k, d_k_a, d_r_k, d_ln_w, d_ln_b, final_norm_w):
    prm = _prepare(w_mod, b_mod, ffn_w_gate, ffn_w_up, ffn_w_down, w_in_even, w_out_even, a_conv_w, a_conv_b,
                   a_gate_r_w, a_gate_r_b, a_gate_i_w, a_gate_i_b, a_lambda, b_lb_gamma, b_norm_w, w_in_odd,
                   w_out_odd, c_igate_b, c_fgate_b, c_norm_w, d_mu, d_w0, d_w2, d_a0, d_a2, d_g2, d_k_k, d_k_a,
                   d_r_k, d_ln_w, d_ln_b, final_norm_w)
    bp, lp, _ = x_prompt.shape
    bs = x_sample.shape[0]
    n_rows = -(-(bs + bp) // 8) * 8
    c_all = jnp.concatenate([c_sample, c_prompt, jnp.zeros((n_rows - bs - bp, D), F32)], axis=0)
    mod_all = _modulation(c_all, w_mod, b_mod)
    mod_s = mod_all[:, None]
    mod_p = mod_all[:, bs:bs + bp, None]

    zeros = lambda *s: jnp.zeros(s, F32)
    st_p = dict(a_conv=zeros(N_EVEN, bp, CONV_W - 1, W_A), a_h=zeros(N_EVEN, bp, W_A),
                b_s=zeros(N_EVEN, bp, H_B, DK_B, DV_B), c_c=zeros(N_ODD, bp, H_C, DK_C, DV_C),
                c_n=zeros(N_ODD, bp, H_C, DK_C), c_m=zeros(N_ODD, bp, H_C), d_shift=zeros(N_ODD, bp, P_D),
                d_s=zeros(N_ODD, bp, H_D, N_D, N_D))
    st_s = dict(a_conv=state_a_conv, a_h=state_a_h, b_s=state_b_s, c_c=state_c_c, c_n=state_c_n, c_m=state_c_m,
                d_shift=state_d_shift, d_s=state_d_s)
    out_s, ffn_bf16 = _trunk(x_sample.reshape(1, bs, D), mod_s, st_s, prm, decode=True)
    out_p, _ = _trunk(x_prompt, mod_p, st_p, dict(prm, ffn_bf16=ffn_bf16), decode=False)
    y_s = out_s[0].reshape(bs, 1, D)
    return (out_p[0], y_s) + out_p[1:] + out_s[1:]
```

```python
import functools

import jax
import jax.numpy as jnp
from jax import lax
from jax.experimental import pallas as pl
from jax.experimental.pallas import tpu as pltpu

F32 = jnp.float32
BF16 = jnp.bfloat16

D = 2048
DEPTH = 4
N_EVEN = 2
N_ODD = 2
D_FF = 5632
N_MOD = 9
EPS = 1e-6
CHUNK = 64
NEG_BIG = -1e30

W_A = 1024
NB_A = 8
BS_A = 128
CONV_W = 4
C_RGLRU = 8.0
H_B = 8
DK_B = 128
DV_B = 128
H_C = 4
DK_C = 128
DV_C = 256
N_D = 64
H_D = 16
W_D = 1024
R_W = 64
R_A = 64
R_G = 128
GN_EPS_D = 64e-5
P_EVEN = 6144
P_D = 3328
P_ODD = 6408
MLSTM_COLS = 3584
DEC_NB = 8
LANES = 128
VMEM_LIMIT = 48 * 2**20


def _cp(sem, vmem=VMEM_LIMIT):
    return pltpu.CompilerParams(dimension_semantics=sem, vmem_limit_bytes=vmem)


def _mm(a, b):
    return jnp.dot(a, b, preferred_element_type=F32)


def _dot(a, b):
    return _mm(a.astype(BF16), b.astype(BF16))


_NT = (((1,), (1,)), ((), ()))
_TN = (((0,), (0,)), ((), ()))
_NN = (((1,), (0,)), ((), ()))


def _dg(a, b, dn):
    return lax.dot_general(a, b, dn, preferred_element_type=F32)


def _dot_nt(a, b):
    return _dg(a.astype(BF16), b.astype(BF16), _NT)


def _dot_tn(a, b):
    return _dg(a.astype(BF16), b.astype(BF16), _TN)


def _split3(x):
    hi = x.astype(BF16)
    r1 = x - hi.astype(F32)
    mid = r1.astype(BF16)
    lo = (r1 - mid.astype(F32)).astype(BF16)
    return hi, mid, lo


def _split2(x):
    hi = x.astype(BF16)
    lo = (x - hi.astype(F32)).astype(BF16)
    return hi, lo


def _pieces(x, n):
    return _split3(x) if n == 3 else _split2(x)


def _sel_dot(mask_bf, x, n=3):
    return functools.reduce(jnp.add, [_mm(mask_bf, p) for p in _pieces(x, n)])


def _dot_sel(x, mask_bf, n=3):
    return functools.reduce(jnp.add, [_mm(p, mask_bf) for p in _pieces(x, n)])


def _dot_sel_tn(x, mask_bf, n=3):
    return functools.reduce(jnp.add, [_dg(p, mask_bf, _TN) for p in _pieces(x, n)])


def _mmx(a, b, dn=_NN):
    a_hi, a_lo = _split2(a)
    b_hi, b_lo = _split2(b)
    return _dg(a_hi, b_hi, dn) + _dg(a_hi, b_lo, dn) + _dg(a_lo, b_hi, dn)


def _iota(shape, dim):
    return lax.broadcasted_iota(jnp.int32, shape, dim)


def _sigmoid(x):
    return jax.nn.sigmoid(x)


def _softplus(x):
    return jnp.maximum(x, 0.0) + jnp.log1p(jnp.exp(-jnp.abs(x)))


def _log_sigmoid(x):
    return jnp.minimum(x, 0.0) - jnp.log1p(jnp.exp(-jnp.abs(x)))


def _gelu_tanh(x):
    return x * (0.5 * (1.0 + jnp.tanh(0.7978845608028654 * (x + 0.044715 * (x * x * x)))))


def _rms(x):
    return x * lax.rsqrt(jnp.mean(x * x, axis=-1, keepdims=True) + EPS)


def _adaln(x, shift, scale):
    return _rms(x) * (1.0 + scale) + shift


def _log2(n):
    k = n.bit_length() - 1
    assert (1 << k) == n
    return k


def _mod_spec(rows, width, layer, col):
    return pl.BlockSpec((None, None, rows, width), lambda g, i, n: (layer, g, 0, col))


def _mod_body(c_ref, w_ref, b_ref, o_ref):
    c = c_ref[...]
    cs = (c * _sigmoid(c)).astype(BF16)
    o_ref[...] = _mm(cs, w_ref[...].astype(BF16)) + b_ref[...]


def _modulation(c_all, w_mod, b_mod):
    rows = c_all.shape[0]
    tn = 1024
    return pl.pallas_call(
        _mod_body,
        grid=(DEPTH, N_MOD * D // tn),
        in_specs=[pl.BlockSpec((rows, D), lambda l, n: (0, 0)),
                  pl.BlockSpec((None, D, tn), lambda l, n: (l, 0, n)),
                  pl.BlockSpec((None, 1, tn), lambda l, n: (l, 0, n))],
        out_specs=pl.BlockSpec((None, rows, tn), lambda l, n: (l, 0, n)),
        out_shape=jax.ShapeDtypeStruct((DEPTH, rows, N_MOD * D), F32),
        compiler_params=_cp(("arbitrary", "arbitrary")),
        name="modulation",
    )(c_all, w_mod, b_mod.reshape(DEPTH, 1, N_MOD * D))


def _lower_bounds_body(g_ref, o_ref):
    g = g_ref[...]
    e = jnp.exp(g - jnp.max(g, axis=0, keepdims=True))
    sm = e / jnp.sum(e, axis=0, keepdims=True)
    acc = jnp.zeros_like(sm[0:1])
    for j in range(N_EVEN):
        acc = acc + sm[j:j + 1]
        o_ref[j:j + 1, :] = acc - sm[0:1]


def _lower_bounds(gamma):
    return pl.pallas_call(
        _lower_bounds_body,
        out_shape=jax.ShapeDtypeStruct(gamma.shape, F32),
        name="hgrn2_lower_bounds",
    )(gamma)


def _ffn_step(x_ref, sh_ref, sc_ref, gt_ref, wg_ref, wu_ref, wd_ref, o_ref, hf_ref, acc_ref):
    f = pl.program_id(2)

    @pl.when(f == 0)
    def _():
        hf_ref[...] = _adaln(x_ref[...], sh_ref[...], sc_ref[...]).astype(BF16)
        acc_ref[...] = jnp.zeros_like(acc_ref)

    hf = hf_ref[...]
    g = _mm(hf, wg_ref[...])
    u = _mm(hf, wu_ref[...])
    act = (g * _sigmoid(g) * u).astype(BF16)
    acc_ref[...] += _mm(act, wd_ref[...])

    @pl.when(f == pl.num_programs(2) - 1)
    def _():
        o_ref[...] = x_ref[...] + 0.5 * (1.0 + gt_ref[...]) * acc_ref[...]


def _ffn_body(x_ref, sh_ref, sc_ref, gt_ref, wg_ref, wu_ref, wd_ref, o_ref, hf_ref, acc_ref):
    _ffn_step(x_ref, sh_ref, sc_ref, gt_ref, wg_ref, wu_ref, wd_ref, o_ref, hf_ref, acc_ref)


def _ffn_cast_body(x_ref, sh_ref, sc_ref, gt_ref, wg_ref, wu_ref, wd_ref, o_ref, wgo_ref, wuo_ref, wdo_ref,
                   hf_ref, acc_ref):
    wgo_ref[...] = wg_ref[...].astype(BF16)
    wuo_ref[...] = wu_ref[...].astype(BF16)
    wdo_ref[...] = wd_ref[...].astype(BF16)
    _ffn_step(x_ref, sh_ref, sc_ref, gt_ref, wgo_ref, wuo_ref, wdo_ref, o_ref, hf_ref, acc_ref)


FFN_TF = 512


def _ffn_specs(x, mod, layer, which, tm):
    R = 1 if mod.shape[2] == 1 else tm
    kb = 6 * which
    return [pl.BlockSpec((None, tm, D), lambda g, i, f: (g, i, 0)),
            _mod_spec(R, D, layer, kb), _mod_spec(R, D, layer, kb + 1), _mod_spec(R, D, layer, kb + 2)]


def _ffn(x, mod, w_bf16, layer, which, tm):
    G, Lg, _ = x.shape
    tf = FFN_TF
    return pl.pallas_call(
        _ffn_body,
        grid=(G, Lg // tm, D_FF // tf),
        in_specs=_ffn_specs(x, mod, layer, which, tm) + [
            pl.BlockSpec((D, tf), lambda g, i, f: (0, f)), pl.BlockSpec((D, tf), lambda g, i, f: (0, f)),
            pl.BlockSpec((tf, D), lambda g, i, f: (f, 0))],
        out_specs=pl.BlockSpec((None, tm, D), lambda g, i, f: (g, i, 0)),
        out_shape=jax.ShapeDtypeStruct(x.shape, F32),
        scratch_shapes=[pltpu.VMEM((tm, D), BF16), pltpu.VMEM((tm, D), F32)],
        compiler_params=_cp(("arbitrary", "arbitrary", "arbitrary")),
        name="adaln_swiglu_ffn",
    )(x, mod, mod, mod, *w_bf16)


def _ffn_cast(x, mod, wg, wu, wd, layer, which):
    G, tm, _ = x.shape
    assert G == 1
    tf = FFN_TF // 2
    bf = lambda s: jax.ShapeDtypeStruct(s, BF16)
    outs = pl.pallas_call(
        _ffn_cast_body,
        grid=(1, 1, D_FF // tf),
        in_specs=_ffn_specs(x, mod, layer, which, tm) + [
            pl.BlockSpec((None, None, D, tf), lambda g, i, f: (layer, which, 0, f)),
            pl.BlockSpec((None, None, D, tf), lambda g, i, f: (layer, which, 0, f)),
            pl.BlockSpec((None, None, tf, D), lambda g, i, f: (layer, which, f, 0))],
        out_specs=[pl.BlockSpec((None, tm, D), lambda g, i, f: (g, i, 0)),
                   pl.BlockSpec((D, tf), lambda g, i, f: (0, f)), pl.BlockSpec((D, tf), lambda g, i, f: (0, f)),
                   pl.BlockSpec((tf, D), lambda g, i, f: (f, 0))],
        out_shape=[jax.ShapeDtypeStruct(x.shape, F32), bf((D, D_FF)), bf((D, D_FF)), bf((D_FF, D))],
        scratch_shapes=[pltpu.VMEM((tm, D), BF16), pltpu.VMEM((tm, D), F32)],
        compiler_params=_cp(("arbitrary", "arbitrary", "arbitrary")),
        name="adaln_swiglu_ffn_weight_cast",
    )(x, mod, mod, mod, wg, wu, wd)
    return outs[0], tuple(outs[1:])


def _inproj_body(x_ref, sh_ref, sc_ref, w_ref, o_ref, hf_ref, *, w_is_out_by_in):
    @pl.when(pl.program_id(2) == 0)
    def _():
        hf_ref[...] = _adaln(x_ref[...], sh_ref[...], sc_ref[...]).astype(BF16)

    o_ref[...] = _dg(hf_ref[...], w_ref[...], _NT if w_is_out_by_in else _NN)


def _inproj(x, mod, w, layer, j, tm, width=None, tn=512, w_is_out_by_in=False):
    G, Lg, _ = x.shape
    R = 1 if mod.shape[2] == 1 else tm
    n_out = w.shape[1] if w_is_out_by_in else w.shape[2]
    P = n_out if width is None else width
    assert P % tn == 0 and P <= n_out
    w_spec = (pl.BlockSpec((None, tn, D), lambda g, i, n: (j, n, 0)) if w_is_out_by_in
              else pl.BlockSpec((None, D, tn), lambda g, i, n: (j, 0, n)))
    return pl.pallas_call(
        functools.partial(_inproj_body, w_is_out_by_in=w_is_out_by_in),
        grid=(G, Lg // tm, P // tn),
        in_specs=[pl.BlockSpec((None, tm, D), lambda g, i, n: (g, i, 0)),
                  _mod_spec(R, D, layer, 3), _mod_spec(R, D, layer, 4), w_spec],
        out_specs=pl.BlockSpec((None, tm, tn), lambda g, i, n: (g, i, n)),
        out_shape=jax.ShapeDtypeStruct((G, Lg, P), F32),
        scratch_shapes=[pltpu.VMEM((tm, D), BF16)],
        compiler_params=_cp(("arbitrary", "arbitrary", "arbitrary")),
        name="adaln_in_projection",
    )(x, mod, mod, w)


def _outproj_body(x_ref, gt_ref, y1_ref, y2_ref, w1_ref, w2_ref, o_ref):
    y = _dot(y1_ref[...], w1_ref[...]) + _dot(y2_ref[...], w2_ref[...])
    o_ref[...] = x_ref[...] + (1.0 + gt_ref[...]) * y


def _outproj(x, mod, y1, y2, w, layer, j, tm):
    G, Lg, _ = x.shape
    R = 1 if mod.shape[2] == 1 else tm
    tn = D
    nb = D // tn
    half = y1.shape[-1]
    return pl.pallas_call(
        _outproj_body,
        grid=(G, Lg // tm, nb),
        in_specs=[pl.BlockSpec((None, tm, tn), lambda g, i, n: (g, i, n)),
                  pl.BlockSpec((None, None, R, tn), lambda g, i, n: (layer, g, 0, 5 * nb + n)),
                  pl.BlockSpec((None, tm, half), lambda g, i, n: (g, i, 0)),
                  pl.BlockSpec((None, tm, half), lambda g, i, n: (g, i, 0)),
                  pl.BlockSpec((None, None, half, tn), lambda g, i, n: (j, 0, 0, n)),
                  pl.BlockSpec((None, None, half, tn), lambda g, i, n: (j, 1, 0, n))],
        out_specs=pl.BlockSpec((None, tm, tn), lambda g, i, n: (g, i, n)),
        out_shape=jax.ShapeDtypeStruct(x.shape, F32),
        compiler_params=_cp(("arbitrary", "arbitrary", "arbitrary")),
        name="out_projection_residual",
    )(x, mod, y1, y2, w, w)


def _final_body(x_ref, w_ref, o_ref):
    o_ref[...] = _rms(x_ref[...]) * w_ref[...]


def _final_norm(x, w, tm):
    G, Lg, _ = x.shape
    return pl.pallas_call(
        _final_body,
        grid=(G, Lg // tm),
        in_specs=[pl.BlockSpec((None, tm, D), lambda g, i: (g, i, 0)),
                  pl.BlockSpec((1, D), lambda g, i: (0, 0))],
        out_specs=pl.BlockSpec((None, tm, D), lambda g, i: (g, i, 0)),
        out_shape=jax.ShapeDtypeStruct(x.shape, F32),
        compiler_params=_cp(("arbitrary", "arbitrary")),
        name="final_rmsnorm",
    )(x, w.reshape(1, D))


def _rglru_gates(x0, x1, x2, x3, cw, cb, wr, br, wi, bi, lam):
    u = cb + x0 * cw[0:1] + x1 * cw[1:2] + x2 * cw[2:3] + x3 * cw[3:4]
    r_parts, i_parts = [], []
    for n in range(NB_A):
        un = u[:, n * BS_A:(n + 1) * BS_A].astype(BF16)
        r_parts.append(_mm(un, wr[n]))
        i_parts.append(_mm(un, wi[n]))
    r = _sigmoid(jnp.concatenate(r_parts, axis=1) + br)
    ig = _sigmoid(jnp.concatenate(i_parts, axis=1) + bi)
    log_a = -C_RGLRU * r * _softplus(-lam)
    a = jnp.exp(log_a)
    b = jnp.sqrt(-jnp.tanh(log_a) * (a * a + 1.0)) * (ig * u)
    return a, b


def _rglru_seq_body(x_ref, halo_ref, buf_ref, cw_ref, cb_ref, wr_ref, br_ref, wi_ref, bi_ref, lam_ref,
                    a_ref, b_ref, *, tt):
    i = pl.program_id(1)
    halo = jnp.where(i == 0, buf_ref[...], halo_ref[...])
    full = jnp.concatenate([halo, x_ref[...]], axis=0)
    taps = [pltpu.roll(full, CONV_W - 1 - k, axis=0)[8:8 + tt] for k in range(CONV_W - 1)]
    a, b = _rglru_gates(taps[0], taps[1], taps[2], x_ref[...], cw_ref[...], cb_ref[...], wr_ref[...], br_ref[...],
                        wi_ref[...], bi_ref[...], lam_ref[...])
    a_ref[...] = a
    b_ref[...] = b


def _rglru_param_specs(nidx):
    z2 = lambda *_: (0, 0)
    z3 = lambda *_: (0, 0, 0)
    del nidx
    return [pl.BlockSpec((CONV_W, W_A), z2), pl.BlockSpec((1, W_A), z2),
            pl.BlockSpec((NB_A, BS_A, BS_A), z3), pl.BlockSpec((1, W_A), z2),
            pl.BlockSpec((NB_A, BS_A, BS_A), z3), pl.BlockSpec((1, W_A), z2),
            pl.BlockSpec((1, W_A), z2)]


def _rglru_seq(z, buf8, prm, tt):
    B, L, _ = z.shape
    hb = tt // 8
    out = jax.ShapeDtypeStruct((B, L, W_A), F32)
    return pl.pallas_call(
        functools.partial(_rglru_seq_body, tt=tt),
        grid=(B, L // tt),
        in_specs=[pl.BlockSpec((None, tt, W_A), lambda b, i: (b, i, 0)),
                  pl.BlockSpec((None, 8, W_A), lambda b, i: (b, jnp.maximum(i * hb - 1, 0), 0)),
                  pl.BlockSpec((None, 8, W_A), lambda b, i: (b, 0, 0))] + _rglru_param_specs(2),
        out_specs=[pl.BlockSpec((None, tt, W_A), lambda b, i: (b, i, 0))] * 2,
        out_shape=[out, out],
        compiler_params=_cp(("arbitrary", "arbitrary")),
        name="rglru_conv_gates",
    )(z, z, buf8, *prm)


def _rglru_scan_body(a_ref, b_ref, ag_ref, h0_ref, y_ref, hl_ref, h_scr, *, tt):
    @pl.when(pl.program_id(1) == 0)
    def _():
        h_scr[...] = h0_ref[...]

    def step(t, h):
        h = a_ref[t] * h + b_ref[t]
        y_ref[t] = h
        return h

    h = lax.fori_loop(0, tt, step, h_scr[...], unroll=8)
    h_scr[...] = h
    hl_ref[...] = h
    y_ref[...] = y_ref[...] * _gelu_tanh(ag_ref[...])


def _rglru_scan(a, b, z, h0, tt):
    B, L, _ = a.shape
    a4 = a.reshape(B, L, 8, LANES)
    b4 = b.reshape(B, L, 8, LANES)
    z4 = z.reshape(B, L, z.shape[-1] // LANES, LANES)
    spec = pl.BlockSpec((None, tt, 8, LANES), lambda bb, i: (bb, i, 0, 0))
    y, hl = pl.pallas_call(
        functools.partial(_rglru_scan_body, tt=tt),
        grid=(B, L // tt),
        in_specs=[spec, spec,
                  pl.BlockSpec((None, tt, 8, LANES), lambda bb, i: (bb, i, 1, 0)),
                  pl.BlockSpec((None, 8, LANES), lambda bb, i: (bb, 0, 0))],
        out_specs=[spec, pl.BlockSpec((None, 8, LANES), lambda bb, i: (bb, 0, 0))],
        out_shape=[jax.ShapeDtypeStruct((B, L, 8, LANES), F32), jax.ShapeDtypeStruct((B, 8, LANES), F32)],
        scratch_shapes=[pltpu.VMEM((8, LANES), F32)],
        compiler_params=_cp(("arbitrary", "arbitrary")),
        name="rglru_scan_gelu_gate",
    )(a4, b4, z4, h0.reshape(B, 8, LANES))
    return y.reshape(B, L, W_A), hl.reshape(B, W_A)


def _rglru_step_body(x3_ref, ag_ref, x0_ref, x1_ref, x2_ref, h0_ref, cw_ref, cb_ref, wr_ref, br_ref, wi_ref, bi_ref,
                     lam_ref, y_ref, h_ref):
    a, b = _rglru_gates(x0_ref[...], x1_ref[...], x2_ref[...], x3_ref[...], cw_ref[...], cb_ref[...], wr_ref[...],
                        br_ref[...], wi_ref[...], bi_ref[...], lam_ref[...])
    h = a * h0_ref[...] + b
    h_ref[...] = h
    y_ref[...] = h * _gelu_tanh(ag_ref[...])


def _rglru_step(z, buf, h0, prm):
    n = z.shape[0]
    row = lambda c: pl.BlockSpec((n, W_A), lambda i: (0, c))
    out = jax.ShapeDtypeStruct((n, W_A), F32)
    return pl.pallas_call(
        _rglru_step_body,
        grid=(1,),
        in_specs=[row(0), row(1), row(0), row(0), row(0), row(0)] + _rglru_param_specs(1),
        out_specs=[row(0), row(0)],
        out_shape=[out, out],
        compiler_params=_cp(("arbitrary",)),
        name="rglru_decode_step",
    )(z, z, buf[:, 0], buf[:, 1], buf[:, 2], h0, *prm)


def _live_rows(C, first, valid):
    row = _iota((C, 1), 0)
    return (row >= first) & (row < first + valid)


def _hgrn2_chunk(q, fp, v, gate, lb, nw, S, *, C, valid, first=0):
    SB = min(16, C)
    k = (1.0 - lb) * _sigmoid(-fp)
    lf = jnp.log1p(-k)
    if valid < C:
        live = _live_rows(C, first, valid)
        k = jnp.where(live, k, 0.0)
        lf = jnp.where(live, lf, 0.0)
    g_col = _dot_sel_tn(lf, jnp.ones((C, DV_B), BF16))
    if valid == 1:
        kv = _dot_tn(k, v)
        yield
        S_new = jnp.exp(g_col) * S + kv
        o = _dot(q, S_new)
        yield
        return _rms(o) * nw * (gate * _sigmoid(gate)), S_new
    tril = _iota((C, C), 0) >= _iota((C, C), 1)
    b = _sel_dot(jnp.where(tril, 1.0, 0.0).astype(BF16), lf)
    yield
    g_row = b[C - 1:C, :]
    o = _dot(q * jnp.exp(b), S)
    khat = k * jnp.exp(g_row - b)
    S_new = jnp.exp(g_col) * S + _dot_tn(khat, v)

    tril_sb = _iota((SB, SB), 0) >= _iota((SB, SB), 1)
    atts = []
    for i in range(C // SB):
        lo = i * SB
        qi, bi, ki = q[lo:lo + SB], b[lo:lo + SB], k[lo:lo + SB]
        ci = bi - jnp.log(ki)
        att = jnp.where(tril_sb, jnp.sum(qi[:, None, :] * jnp.exp(bi[:, None, :] - ci[None, :, :]), axis=-1), 0.0)
        off = None
        if i > 0:
            ref = b[lo - 1:lo, :]
            qt = qi * jnp.exp(bi - ref)
            kt = k[0:lo] * jnp.exp(ref - b[0:lo])
            off = _dot_nt(qt, kt)
        atts.append((att, off))
    yield
    parts = []
    for i, (att, off) in enumerate(atts):
        lo = i * SB
        oi = _dot(att, v[lo:lo + SB])
        parts.append(oi if off is None else oi + _dot(off, v[0:lo]))
    yield
    o = o + (parts[0] if len(parts) == 1 else jnp.concatenate(parts, axis=0))
    return _rms(o) * nw * (gate * _sigmoid(gate)), S_new


def _merge_rows(outs, C):
    row = _iota((C, 1), 0)
    y = jnp.where(row == 0, outs[0], 0.0)
    for s in range(1, len(outs)):
        y = jnp.where(row == s, outs[s], y)
    return y


def _chained(prev, out_index, operands, specs):
    if prev is None:
        return {}
    operands.append(prev)
    specs.append(pl.BlockSpec(memory_space=pl.ANY))
    return {len(operands) - 1: out_index}


def _hgrn2_body(q_ref, f_ref, v_ref, g_ref, lb_ref, nw_ref, s0_ref, *rest, C, valid, nb, hp, decode, nchunk):
    y_ref, s1_ref = rest[-2:]

    @pl.when(pl.program_id(2) == 0)
    def _():
        s1_ref[...] = s0_ref[...]

    nw = nw_ref[...]
    probs = [(bi, hh) for bi in range(nb) for hh in range(hp)]

    def chunk(c, carry):
        rows = pl.ds(pl.multiple_of(c * C, C), C)
        gens = []
        for bi, hh in probs:
            lanes = slice(hh * LANES, (hh + 1) * LANES)
            src = 0 if decode else bi
            gens.append(_hgrn2_chunk(q_ref[src, rows, lanes], f_ref[src, rows, lanes], v_ref[src, rows, lanes],
                                     g_ref[src, rows, lanes], lb_ref[:, lanes], nw, s1_ref[bi, hh],
                                     C=C, valid=valid, first=bi if decode else 0))
        outs = _interleave(gens)
        for (bi, hh), (y, s_new) in zip(probs, outs):
            s1_ref[bi, hh] = s_new
            if not decode:
                y_ref[bi, rows, hh * LANES:(hh + 1) * LANES] = y
        if decode:
            for hh in range(hp):
                y_ref[0, rows, hh * LANES:(hh + 1) * LANES] = _merge_rows([outs[bi * hp + hh][0] for bi in range(nb)], C)
        return carry

    lax.fori_loop(0, nchunk, chunk, 0)


def _hgrn2(z, lb, nw, s_all, s_prev, j, *, C, valid, nb, hp, tt, decode):
    B, L, _ = z.shape
    zb = 1 if decode else nb
    w = hp * LANES
    col = lambda base: pl.BlockSpec((zb, tt, w), lambda b, h, i: (b, i, base // hp + h))
    st = pl.BlockSpec((None, nb, hp, DK_B, DV_B), lambda b, h, i: (j, b, h, 0, 0))
    operands = [z, z, z, z, lb, nw, s_all]
    specs = [col(16), col(24), col(32), col(40), pl.BlockSpec((1, w), lambda b, h, i: (0, h)),
             pl.BlockSpec((1, LANES), lambda b, h, i: (0, 0)), st]
    aliases = _chained(s_prev, 1, operands, specs)
    return pl.pallas_call(
        functools.partial(_hgrn2_body, C=C, valid=valid, nb=nb, hp=hp, decode=decode, nchunk=tt // C),
        grid=(B // zb, H_B // hp, L // tt),
        in_specs=specs,
        out_specs=[col(0), st],
        out_shape=[jax.ShapeDtypeStruct((B, L, H_B * DV_B), F32), jax.ShapeDtypeStruct(s_all.shape, F32)],
        input_output_aliases=aliases,
        compiler_params=_cp(("arbitrary", "arbitrary", "arbitrary")),
        name="hgrn2_chunked",
    )(*operands)


def _mlstm_step(q, k, v, op, gt, h, nw, Cst, n, m, *, C, first):
    lane = _iota((C, LANES), 1)
    live = _live_rows(C, first, 1)
    i0 = jnp.sum(jnp.where(live & (lane == h), gt, 0.0), keepdims=True)
    lf0 = _log_sigmoid(jnp.sum(jnp.where(live & (lane == h + H_C), gt, 0.0), keepdims=True))
    inter = lf0 + m
    m_t = jnp.maximum(inter, i0)
    w_inter = jnp.exp(inter - m_t)
    wk = jnp.exp(i0 - m_t)
    kh = k * (DK_C ** -0.5)
    s = jnp.sum(q * kh, axis=1, keepdims=True) * wk
    qc = _dot(q, Cst)
    kw = jnp.where(live, kh * wk, 0.0)
    kv = _dot_tn(kw, v)
    yield
    num = w_inter * qc + s * v
    den = w_inter * jnp.sum(q * n, axis=1, keepdims=True) + s
    hh = num / jnp.maximum(jnp.abs(den), jnp.exp(-m_t))
    C_new = w_inter * Cst + kv
    n_new = w_inter * n + jnp.sum(kw, axis=0, keepdims=True)
    y = _rms(hh) * nw * _sigmoid(op)
    return y, C_new, n_new, m_t


def _mlstm_chunk(q, k, v, op, gt, h, nw, Cst, n, m, *, C, valid, first=0):
    if valid == 1:
        return (yield from _mlstm_step(q, k, v, op, gt, h, nw, Cst, n, m, C=C, first=first))
    assert first == 0
    gtT = gt.T
    lane = _iota((C, LANES), 1)
    sub = _iota((LANES, C), 0)
    i_col = jnp.sum(jnp.where(lane == h, gt, 0.0), axis=1, keepdims=True)
    f_col = jnp.sum(jnp.where(lane == h + H_C, gt, 0.0), axis=1, keepdims=True)
    i_row = jnp.sum(jnp.where(sub == h, gtT, 0.0), axis=0, keepdims=True)
    f_row = jnp.sum(jnp.where(sub == h + H_C, gtT, 0.0), axis=0, keepdims=True)
    lf_col = _log_sigmoid(f_col)
    lf_row = _log_sigmoid(f_row)
    if valid < C:
        live_c = _iota((C, 1), 0) < valid
        live_r = _iota((1, C), 1) < valid
        i_col = jnp.where(live_c, i_col, NEG_BIG)
        i_row = jnp.where(live_r, i_row, NEG_BIG)
        lf_col = jnp.where(live_c, lf_col, 0.0)
        lf_row = jnp.where(live_r, lf_row, 0.0)
    tril = _iota((C, C), 0) >= _iota((C, C), 1)
    b_col = jnp.sum(jnp.where(tril, lf_row, 0.0), axis=1, keepdims=True)
    b_row = jnp.sum(jnp.where(_iota((C, C), 0) <= _iota((C, C), 1), lf_col, 0.0), axis=0, keepdims=True)
    dm = jnp.where(tril, b_col - b_row + i_row, NEG_BIG)
    inter = b_col + m
    m_t = jnp.maximum(inter, jnp.max(dm, axis=1, keepdims=True))
    w_inter = jnp.exp(inter - m_t)
    kh = k * (DK_C ** -0.5)
    qk = _dot_nt(q, kh)
    qc = _dot(q, Cst)
    g = b_col[C - 1:C, :]
    m_new = m_t[C - 1:C, :]
    wk = jnp.exp(g - b_col + i_col - m_new)
    f_state = jnp.exp(g + m - m_new)
    kw = kh * wk
    kv = _dot_tn(kw, v)
    yield
    s = qk * jnp.exp(dm - m_t)
    sv = _dot(s, v)
    yield
    num = w_inter * qc + sv
    den = w_inter * jnp.sum(q * n, axis=1, keepdims=True) + jnp.sum(s, axis=1, keepdims=True)
    hh = num / jnp.maximum(jnp.abs(den), jnp.exp(-m_t))
    C_new = f_state * Cst + kv
    n_new = f_state * n + jnp.sum(kw, axis=0, keepdims=True)
    y = _rms(hh) * nw * _sigmoid(op)
    return y, C_new, n_new, m_new


def _mlstm_body(q_ref, k_ref, v_ref, o_ref, g_ref, gb_ref, nw_ref, c0_ref, n0_ref, m0_ref, *rest,
                C, valid, nb, hp, decode, nchunk):
    y_ref, c1_ref, n1_ref, m1_ref = rest[-4:]

    @pl.when(pl.program_id(2) == 0)
    def _():
        c1_ref[...] = c0_ref[...]
        n1_ref[...] = n0_ref[...]
        m1_ref[...] = m0_ref[...]

    gb = gb_ref[...]
    nw = nw_ref[...]
    probs = [(bi, hh) for bi in range(nb) for hh in range(hp)]

    def chunk(c, carry):
        rows = pl.ds(pl.multiple_of(c * C, C), C)
        gens = []
        for bi, hh in probs:
            src = 0 if decode else bi
            kl = slice(hh * DK_C, (hh + 1) * DK_C)
            vl = slice(hh * DV_C, (hh + 1) * DV_C)
            gens.append(_mlstm_chunk(
                q_ref[src, rows, kl], k_ref[src, rows, kl], v_ref[src, rows, vl], o_ref[src, rows, vl],
                g_ref[src, rows, :] + gb, pl.program_id(1) * hp + hh, nw, c1_ref[bi, hh], n1_ref[bi, hh],
                m1_ref[bi, hh][:, 0:1], C=C, valid=valid, first=bi if decode else 0))
        outs = _interleave(gens)
        for (bi, hh), (y, c_new, n_new, m_new) in zip(probs, outs):
            if not decode:
                y_ref[bi, rows, hh * DV_C:(hh + 1) * DV_C] = y
            c1_ref[bi, hh] = c_new
            n1_ref[bi, hh] = n_new
            m1_ref[bi, hh] = jnp.broadcast_to(m_new, (1, LANES))
        if decode:
            for hh in range(hp):
                y_ref[0, rows, hh * DV_C:(hh + 1) * DV_C] = _merge_rows([outs[bi * hp + hh][0] for bi in range(nb)], C)
        return carry

    lax.fori_loop(0, nchunk, chunk, 0)


def _mlstm(z, gb, nw, states, prev, j, *, C, valid, nb, hp, tt, decode):
    B, L, _ = z.shape
    zb = 1 if decode else nb
    c128 = lambda base: pl.BlockSpec((zb, tt, hp * DK_C), lambda b, h, i: (b, i, base // hp + h))
    c256 = lambda base: pl.BlockSpec((zb, tt, hp * DV_C), lambda b, h, i: (b, i, base // hp + h))
    cst = pl.BlockSpec((None, nb, hp, DK_C, DV_C), lambda b, h, i: (j, b, h, 0, 0))
    vec = pl.BlockSpec((None, nb, hp, 1, LANES), lambda b, h, i: (j, b, h, 0, 0))
    operands = [z, z, z, z, z, gb, nw, *states]
    specs = [c128(0), c128(4), c256(4), c256(8), pl.BlockSpec((zb, tt, LANES), lambda b, h, i: (b, i, 24)),
             pl.BlockSpec((1, LANES), lambda b, h, i: (0, 0)), pl.BlockSpec((1, DV_C), lambda b, h, i: (0, 0)),
             cst, vec, vec]
    aliases = {}
    for k, p in enumerate(prev or ()):
        aliases.update(_chained(p, 1 + k, operands, specs))
    outs = pl.pallas_call(
        functools.partial(_mlstm_body, C=C, valid=valid, nb=nb, hp=hp, decode=decode, nchunk=tt // C),
        grid=(B // zb, H_C // hp, L // tt),
        in_specs=specs,
        out_specs=[c256(0), cst, vec, vec],
        out_shape=[jax.ShapeDtypeStruct((B, L, H_C * DV_C), F32)] + [jax.ShapeDtypeStruct(s.shape, F32) for s in states],
        input_output_aliases=aliases,
        compiler_params=_cp(("arbitrary", "arbitrary", "arbitrary")),
        name="mlstm_chunked",
    )(*operands)
    return outs[0], tuple(outs[1:])


def _head_sum(x, bd):
    return jnp.concatenate([_dot_sel(x[:, n * LANES:(n + 1) * LANES], bd) for n in range(W_D // LANES)], axis=1)


def _rwkv_pre(zd, sh, mu, w0, a0, kkw, ka, rk, w2p, a2p, g2, bd):
    zs = zd + (sh - zd) * mu
    r, k, v = zs[:, 0:W_D], zs[:, W_D:2 * W_D], zs[:, 2 * W_D:3 * W_D]
    wa = zs[:, 3 * W_D:3 * W_D + LANES]
    gl = zs[:, 3 * W_D + LANES:]
    w = -_softplus(-(w0 + _dot(jnp.tanh(wa), w2p))) - 0.5
    logd = -jnp.exp(w)
    a = _sigmoid(a0 + _dot(wa, a2p))
    gate = _dot(_sigmoid(gl), g2)
    kk = k * kkw
    kk = kk / jnp.maximum(jnp.sqrt(_head_sum(kk * kk, bd)), 1e-12)
    kmod = k * (1.0 + (a - 1.0) * ka)
    bonus = _head_sum(r * kmod * rk, bd) * v
    return r, logd, kmod, v, kk, kk * a, gate, bonus


def _head_block_ones():
    return jnp.where((_iota((LANES, LANES), 0) >> 6) == (_iota((LANES, LANES), 1) >> 6), 1.0, 0.0).astype(BF16)


def _rwkv_pre_seq_body(z_ref, halo_ref, prev_ref, mu_ref, w0_ref, a0_ref, kk_ref, ka_ref, rk_ref, w2_ref, a2_ref,
                       g2_ref, *out_refs, tt):
    i = pl.program_id(1)
    halo = jnp.where(i == 0, prev_ref[...], halo_ref[...])
    zd = z_ref[...]
    sh = pltpu.roll(jnp.concatenate([halo, zd], axis=0), 1, axis=0)[8:8 + tt]
    outs = _rwkv_pre(zd, sh, mu_ref[...], w0_ref[...], a0_ref[...], kk_ref[...], ka_ref[...], rk_ref[...],
                     w2_ref[...], a2_ref[...], g2_ref[...], _head_block_ones())
    for ref, val in zip(out_refs, outs):
        ref[...] = val


def _rwkv_pre_step_body(z_ref, sh_ref, mu_ref, w0_ref, a0_ref, kk_ref, ka_ref, rk_ref, w2_ref, a2_ref, g2_ref,
                        *out_refs):
    outs = _rwkv_pre(z_ref[...], sh_ref[...], mu_ref[...], w0_ref[...], a0_ref[...], kk_ref[...], ka_ref[...],
                     rk_ref[...], w2_ref[...], a2_ref[...], g2_ref[...], _head_block_ones())
    for ref, val in zip(out_refs[:6], outs[:6]):
        ref[...] = val.T
    for ref, val in zip(out_refs[6:], outs[6:]):
        ref[...] = val


def _rwkv_pre_param_specs():
    z2 = lambda *_: (0, 0)
    return ([pl.BlockSpec((1, P_D), z2)] + [pl.BlockSpec((1, W_D), z2)] * 5
            + [pl.BlockSpec((LANES, W_D), z2)] * 3)


def _rwkv_pre_seq(z, prev8, prm, tt):
    B, L, _ = z.shape
    hb = tt // 8
    out = jax.ShapeDtypeStruct((B, L, W_D), F32)
    return pl.pallas_call(
        functools.partial(_rwkv_pre_seq_body, tt=tt),
        grid=(B, L // tt),
        in_specs=[pl.BlockSpec((None, tt, P_D), lambda b, i: (b, i, 0)),
                  pl.BlockSpec((None, 8, P_D), lambda b, i: (b, jnp.maximum(i * hb - 1, 0), 0)),
                  pl.BlockSpec((None, 8, P_D), lambda b, i: (b, 0, 0))] + _rwkv_pre_param_specs(),
        out_specs=[pl.BlockSpec((None, tt, W_D), lambda b, i: (b, i, 0))] * 8,
        out_shape=[out] * 8,
        compiler_params=_cp(("arbitrary", "arbitrary")),
        name="rwkv7_token_shift_features",
    )(z, z, prev8, *prm)


def _rwkv_pre_step(z, shifted, prm):
    n = z.shape[0]
    fm, tm = jax.ShapeDtypeStruct((W_D, n), F32), jax.ShapeDtypeStruct((n, W_D), F32)
    whole = lambda s: pl.BlockSpec(s.shape, lambda i: (0, 0))
    outs = pl.pallas_call(
        _rwkv_pre_step_body,
        grid=(1,),
        in_specs=[pl.BlockSpec((n, P_D), lambda i: (0, 0)), pl.BlockSpec((n, P_D), lambda i: (0, 0))]
        + _rwkv_pre_param_specs(),
        out_specs=[whole(fm)] * 6 + [whole(tm)] * 2,
        out_shape=[fm] * 6 + [tm] * 2,
        compiler_params=_cp(("arbitrary",)),
        name="rwkv7_decode_features",
    )(z, shifted, *prm)
    return outs[:6], outs[6], outs[7]


def _unit_lower_inverse(N, C):
    n = N.shape[0]
    ri, ci = _iota((n, n), 0), _iota((n, n), 1)
    base = min(8, C)
    kb = _log2(base)
    X = jnp.where((ri >> kb) == (ci >> kb), N, 0.0)
    T = jnp.where(ri == ci, 1.0, 0.0) + X
    for _ in range(kb - 1):
        X = _dot(X, X)
        yield
        T = T + _dot(T, X)
        yield
    size = base
    while size < C:
        ks = _log2(size)
        sel = ((ri >> (ks + 1)) == (ci >> (ks + 1))) & (((ri >> ks) & 1) == 1) & (((ci >> ks) & 1) == 0)
        TN_ = _dot(T, jnp.where(sel, N, 0.0))
        yield
        T = T + _dot(TN_, T)
        yield
        size *= 2
    return T


def _interleave(gens):
    results = [None] * len(gens)
    live = list(range(len(gens)))
    while live:
        for i in list(live):
            try:
                next(gens[i])
            except StopIteration as stop:
                results[i] = stop.value
                live.remove(i)
    return results


def _rwkv_groupnorm_gate(y, gate, bonus, lnw, lnb):
    bd = _head_block_ones()
    mean = _dot_sel(y, bd, 2) * (1.0 / N_D)
    yield
    dy = y - mean
    var = _dot(dy * dy, bd) * (1.0 / N_D)
    yield
    return (dy * lax.rsqrt(var + GN_EPS_D) * lnw + lnb + bonus) * gate


def _rwkv_chunk(r, ld, k, v, kk, be, gate, bonus, lnw, lnb, P, *, C, valid, first=0):
    if valid < C:
        live = _live_rows(C, first, valid)
        ld = jnp.where(live, ld, 0.0)
        be = jnp.where(live, be, 0.0)
        k = jnp.where(live, k, 0.0)
        v = jnp.where(live, v, 0.0)
    tril = _iota((C, C), 0) >= _iota((C, C), 1)
    c = _sel_dot(jnp.where(tril, 1.0, 0.0).astype(BF16), ld, 2)
    g_col = _dot_sel_tn(ld, jnp.ones((C, LANES), BF16), 2)
    yield
    ec, enc = jnp.exp(c), jnp.exp(-c)
    lane0 = _iota((C, LANES), 1) < N_D

    def stack(x):
        return jnp.concatenate([jnp.where(lane0, x, 0.0), jnp.where(lane0, 0.0, x)], axis=0)

    n2 = 2 * C
    AR = jnp.concatenate([stack(-kk * jnp.exp(c - ld)), stack(r * ec)], axis=0).astype(BF16)
    BK = jnp.concatenate([stack(be * enc), stack(k * enc)], axis=0).astype(BF16)
    V2 = stack(v).astype(BF16)
    kc = _log2(C)
    ri, ci = _iota((2 * n2, 2 * n2), 0), _iota((2 * n2, 2 * n2), 1)
    tpos, spos = ri & (C - 1), ci & (C - 1)
    keep = (((ri >> kc) & 1) == ((ci >> kc) & 1)) & ((spos < tpos) | ((ri >= n2) & (spos == tpos)))
    G = jnp.where(keep, _dg(AR, BK, _NT), 0.0)
    LP = _mm(AR, P.astype(BF16))
    yield
    GV = _mm(G[:, n2:].astype(BF16), V2)
    T = yield from _unit_lower_inverse(G[0:n2, 0:n2], C)
    U = _dot(T, LP[0:n2] + GV[0:n2])
    yield
    Y2 = LP[n2:] + GV[n2:] + _dot(G[n2:, 0:n2], U)
    P_new = jnp.exp(g_col) * (P + _dg(BK, jnp.concatenate([U.astype(BF16), V2], axis=0), _TN))
    yield
    y = Y2[0:C] + Y2[C:n2]
    out = yield from _rwkv_groupnorm_gate(y, gate, bonus, lnw, lnb)
    return out, P_new


def _rwkv_body(r_ref, ld_ref, k_ref, v_ref, kk_ref, be_ref, gt_ref, bo_ref, lnw_ref, lnb_ref, p0_ref,
               y_ref, p1_ref, *, C, valid, nb, hp, nchunk):
    @pl.when(pl.program_id(2) == 0)
    def _():
        p1_ref[...] = p0_ref[...]

    probs = [(bi, pp) for bi in range(nb) for pp in range(hp)]
    in_refs = (r_ref, ld_ref, k_ref, v_ref, kk_ref, be_ref, gt_ref, bo_ref)

    def chunk(c, carry):
        rows = pl.ds(pl.multiple_of(c * C, C), C)
        gens = []
        for bi, pp in probs:
            lanes = slice(pp * LANES, (pp + 1) * LANES)
            args = [ref[bi, rows, lanes] for ref in in_refs] + [lnw_ref[:, lanes], lnb_ref[:, lanes], p1_ref[bi, pp]]
            gens.append(_rwkv_chunk(*args, C=C, valid=valid))
        for (bi, pp), (y, p_new) in zip(probs, _interleave(gens)):
            p1_ref[bi, pp] = p_new
            y_ref[bi, rows, pp * LANES:(pp + 1) * LANES] = y
        return carry

    lax.fori_loop(0, nchunk, chunk, 0)


def _rwkv_decode_body(r_ref, ld_ref, k_ref, v_ref, kk_ref, be_ref, s0_ref, *rest):
    y_ref, s1_ref = rest[-2:]
    S = s0_ref[...]
    sa = -jnp.sum(S * kk_ref[...][None], axis=1)
    S = S * jnp.exp(ld_ref[...])[None] + sa[:, None, :] * be_ref[...][None] + v_ref[...][:, None, :] * k_ref[...][None]
    s1_ref[...] = S
    y_ref[...] = jnp.sum(S * r_ref[...][None], axis=1)


def _rwkv_decode(feats_fm, s_all, s_prev, j):
    n = feats_fm[0].shape[1]
    row = pl.BlockSpec((N_D, n), lambda h: (h, 0))
    st = pl.BlockSpec((None, None, N_D, N_D, n), lambda h: (j, h, 0, 0, 0))
    operands = [*feats_fm, s_all]
    specs = [row] * 6 + [st]
    aliases = _chained(s_prev, 1, operands, specs)
    return pl.pallas_call(
        _rwkv_decode_body,
        grid=(H_D,),
        in_specs=specs,
        out_specs=[row, st],
        out_shape=[jax.ShapeDtypeStruct((W_D, n), F32), jax.ShapeDtypeStruct(s_all.shape, F32)],
        input_output_aliases=aliases,
        compiler_params=_cp(("arbitrary",)),
        name="rwkv7_decode_step",
    )(*operands)


def _rwkv_decode_out_body(y_ref, gt_ref, bo_ref, lnw_ref, lnb_ref, o_ref):
    y = y_ref[...].T
    bd = _head_block_ones()
    dy = y - _head_sum(y, bd) * (1.0 / N_D)
    var = _head_sum(dy * dy, bd) * (1.0 / N_D)
    o_ref[...] = (dy * lax.rsqrt(var + GN_EPS_D) * lnw_ref[...] + lnb_ref[...] + bo_ref[...]) * gt_ref[...]


def _rwkv_decode_out(y_fm, gate, bonus, lnw, lnb):
    n = y_fm.shape[1]
    return pl.pallas_call(
        _rwkv_decode_out_body,
        out_shape=jax.ShapeDtypeStruct((n, W_D), F32),
        name="rwkv7_decode_groupnorm_gate",
    )(y_fm, gate, bonus, lnw, lnb)


def _rwkv(feats, lnw, lnb, p0, *, C, valid, nb, hp, tt):
    B, L, _ = feats[0].shape
    w = hp * LANES
    col = pl.BlockSpec((nb, tt, w), lambda b, p, i: (b, i, p))
    vec = pl.BlockSpec((1, w), lambda b, p, i: (0, p))
    st = pl.BlockSpec((nb, hp, LANES, LANES), lambda b, p, i: (b, p, 0, 0))
    return pl.pallas_call(
        functools.partial(_rwkv_body, C=C, valid=valid, nb=nb, hp=hp, nchunk=tt // C),
        grid=(B // nb, H_D // 2 // hp, L // tt),
        in_specs=[col] * 8 + [vec, vec, st],
        out_specs=[col, st],
        out_shape=[jax.ShapeDtypeStruct((B, L, W_D), F32), jax.ShapeDtypeStruct(p0.shape, F32)],
        compiler_params=_cp(("arbitrary", "arbitrary", "arbitrary")),
        name="rwkv7_chunked",
    )(*feats, lnw, lnb, p0)


def _pair_blockdiag(s):
    B = s.shape[0]
    st = s.reshape(B, H_D // 2, 2, N_D, N_D)
    zero = jnp.zeros_like(st[:, :, 0])
    top = jnp.concatenate([st[:, :, 0], zero], axis=-1)
    bot = jnp.concatenate([zero, st[:, :, 1]], axis=-1)
    return jnp.concatenate([top, bot], axis=-2)


def _pair_unblockdiag(p):
    B = p.shape[0]
    st = jnp.stack([p[:, :, :N_D, :N_D], p[:, :, N_D:, N_D:]], axis=2)
    return st.reshape(B, H_D, N_D, N_D)


def _trunk(x, mod, st, prm, *, decode):
    G, Lg, _ = x.shape
    tm = min(128 if decode else 512, Lg)
    tm_in = tm if decode or Lg % 1024 else 1024
    assert decode or Lg % 512 == 0
    nseq = Lg if decode else G
    new = {k: [] for k in ("a_conv", "a_h", "d_shift", "d_s")}
    b_s = None
    c_in = (st["c_c"], st["c_n"][:, :, :, None, :],
            jnp.broadcast_to(st["c_m"][..., None, None], st["c_m"].shape + (1, LANES)))
    c_out = d_s = None
    d_s_in = jnp.transpose(st["d_s"], (0, 2, 3, 4, 1)) if decode else None
    blocks = lambda t: t.reshape(nseq // DEC_NB, DEC_NB, -1)
    ffn_bf16 = {} if decode else prm["ffn_bf16"]

    def ffn(x, l, which):
        if decode:
            assert Lg == tm
            x, ffn_bf16[l, which] = _ffn_cast(x, mod, prm["ffn_w_gate"], prm["ffn_w_up"], prm["ffn_w_down"], l, which)
            return x
        return _ffn(x, mod, ffn_bf16[l, which], l, which, tm)

    for l in range(DEPTH):
        j = l // 2
        x = ffn(x, l, 0)
        if l % 2 == 0:
            z = _inproj(x, mod, prm["w_in_even"], l, j, tm_in, tn=1024)
            a_prm = prm["rglru"][j]
            conv0, h0 = st["a_conv"][j], st["a_h"][j]
            if decode:
                z2 = z[0]
                y1, h1 = _rglru_step(z2, conv0, h0, a_prm)
                conv1 = jnp.concatenate([conv0[:, 1:], z2[:, None, :W_A]], axis=1)
                yb, b_s = _hgrn2(blocks(z2), prm["lb"][j], prm["b_norm_w"][j], st["b_s"], b_s, j,
                                 C=DEC_NB, valid=1, nb=DEC_NB, hp=1, tt=DEC_NB, decode=True)
                y1, y2 = y1[None], yb.reshape(1, nseq, -1)
            else:
                buf8 = jnp.concatenate([jnp.zeros((nseq, 8 - (CONV_W - 1), W_A), F32), conv0], axis=1)
                a, b = _rglru_seq(z, buf8, a_prm, 512)
                y1, h1 = _rglru_scan(a, b, z, h0, 512)
                conv1 = z[:, Lg - (CONV_W - 1):, :W_A]
                y2, b_s = _hgrn2(z, prm["lb"][j], prm["b_norm_w"][j], st["b_s"], b_s, j,
                                 C=CHUNK, valid=CHUNK, nb=nseq, hp=2, tt=512, decode=False)
            new["a_conv"].append(conv1)
            new["a_h"].append(h1)
            x = _outproj(x, mod, y1, y2, prm["w_out_even"], l, j, tm)
        else:
            z = _inproj(x, mod, prm["w_in_odd"], l, j, tm_in, width=MLSTM_COLS, w_is_out_by_in=True)
            zd = _inproj(x, mod, prm["w_in_rwkv"], l, j, tm, tn=P_D // 2, w_is_out_by_in=True)
            d_prm = prm["rwkv"][j]
            if decode:
                yc, c_out = _mlstm(blocks(z[0]), prm["c_gate_b"][j], prm["c_norm_w"][j], c_in, c_out, j,
                                   C=DEC_NB, valid=1, nb=DEC_NB, hp=1, tt=DEC_NB, decode=True)
                feats_fm, gate, bonus = _rwkv_pre_step(zd[0], st["d_shift"][j], d_prm)
                y_fm, d_s = _rwkv_decode(feats_fm, d_s_in, d_s, j)
                yd = _rwkv_decode_out(y_fm, gate, bonus, prm["d_ln_w"][j], prm["d_ln_b"][j])
                y1, y2 = yc.reshape(1, nseq, -1), yd[None]
                shift1 = zd[0]
            else:
                y1, c_out = _mlstm(z, prm["c_gate_b"][j], prm["c_norm_w"][j], c_in, c_out, j,
                                   C=CHUNK, valid=CHUNK, nb=nseq, hp=2, tt=256, decode=False)
                prev8 = jnp.concatenate([jnp.zeros((nseq, 7, P_D), F32), st["d_shift"][j][:, None]], axis=1)
                feats = _rwkv_pre_seq(zd, prev8, d_prm, 128)
                p0 = _pair_blockdiag(jnp.swapaxes(st["d_s"][j], -1, -2))
                y2, p1 = _rwkv(feats, prm["d_ln_w"][j], prm["d_ln_b"][j], p0, C=CHUNK, valid=CHUNK, nb=nseq, hp=2,
                               tt=256)
                shift1 = zd[:, -1]
                new["d_s"].append(jnp.swapaxes(_pair_unblockdiag(p1), -1, -2))
            new["d_shift"].append(shift1)
            x = _outproj(x, mod, y1, y2, prm["w_out_odd"], l, j, tm)
        x = ffn(x, l, 1)
    y = _final_norm(x, prm["final_norm_w"], tm)
    d_s = jnp.transpose(d_s, (0, 4, 1, 2, 3)) if decode else jnp.stack(new["d_s"])
    outs = (y, jnp.stack(new["a_conv"]), jnp.stack(new["a_h"]), b_s, c_out[0], c_out[1][:, :, :, 0, :],
            c_out[2][:, :, :, 0, 0], jnp.stack(new["d_shift"]), d_s)
    return outs, ffn_bf16


def _prepare(w_mod, b_mod, ffn_w_gate, ffn_w_up, ffn_w_down, w_in_even, w_out_even, a_conv_w, a_conv_b, a_gate_r_w,
             a_gate_r_b, a_gate_i_w, a_gate_i_b, a_lambda, b_lb_gamma, b_norm_w, w_in_odd, w_out_odd, c_igate_b,
             c_fgate_b, c_norm_w, d_mu, d_w0, d_w2, d_a0, d_a2, d_g2, d_k_k, d_k_a, d_r_k, d_ln_w, d_ln_b,
             final_norm_w):
    n_gate = 2 * H_C
    w_odd_t = jnp.swapaxes(w_in_odd, 1, 2).astype(BF16)
    row = lambda t: t[:, None, :]
    zpad = jnp.zeros((N_ODD, LANES - n_gate), F32)
    half = jnp.zeros((N_ODD, R_W, W_D), F32)
    rglru = [(a_conv_w[j], a_conv_b[j][None], a_gate_r_w[j].astype(BF16), a_gate_r_b[j][None],
              a_gate_i_w[j].astype(BF16), a_gate_i_b[j][None], a_lambda[j][None]) for j in range(N_EVEN)]
    w2p = jnp.concatenate([d_w2, half], axis=1).astype(BF16)
    a2p = jnp.concatenate([half, d_a2], axis=1).astype(BF16)
    g2 = d_g2.astype(BF16)
    rwkv = [(d_mu[j][None], d_w0[j][None], d_a0[j][None], d_k_k[j][None], d_k_a[j][None],
             d_r_k[j].reshape(1, W_D), w2p[j], a2p[j], g2[j]) for j in range(N_ODD)]
    return dict(
        ffn_w_gate=ffn_w_gate, ffn_w_up=ffn_w_up, ffn_w_down=ffn_w_down,
        w_in_even=w_in_even.astype(BF16), w_in_odd=w_odd_t, w_in_rwkv=w_odd_t[:, P_ODD - P_D:],
        w_out_even=w_out_even.astype(BF16).reshape(N_EVEN, 2, D // 2, D),
        w_out_odd=w_out_odd.astype(BF16).reshape(N_ODD, 2, D // 2, D),
        rglru=rglru, rwkv=rwkv,
        lb=row(_lower_bounds(b_lb_gamma)), b_norm_w=row(b_norm_w),
        c_gate_b=row(jnp.concatenate([c_igate_b, c_fgate_b, zpad], axis=-1)), c_norm_w=row(c_norm_w),
        d_ln_w=row(d_ln_w), d_ln_b=row(d_ln_b), final_norm_w=final_norm_w)


def kernel(x_prompt, x_sample, c_prompt, c_sample, state_a_conv, state_a_h, state_b_s, state_c_c, state_c_n, state_c_m, state_d_shift, state_d_s, w_mod, b_mod, ffn_w_gate, ffn_w_up, ffn_w_down, w_in_even, w_out_even, a_conv_w, a_conv_b, a_gate_r_w, a_gate_r_b, a_gate_i_w, a_gate_i_b, a_lambda, b_lb_gamma, b_norm_w, w_in_odd, w_out_odd, c_igate_b, c_fgate_b, c_norm_w, d_mu, d_w0, d_w2, d_a0, d_a2, d_g2, d_k_k, d_k_a, d_r_k, d_ln_w, d_ln_b, final_norm_w):
    prm = _prepare(w_mod, b_mod, ffn_w_gate, ffn_w_up, ffn_w_down, w_in_even, w_out_even, a_conv_w, a_conv_b,
                   a_gate_r_w, a_gate_r_b, a_gate_i_w, a_gate_i_b, a_lambda, b_lb_gamma, b_norm_w, w_in_odd,
                   w_out_odd, c_igate_b, c_fgate_b, c_norm_w, d_mu, d_w0, d_w2, d_a0, d_a2, d_g2, d_k_k, d_k_a,
                   d_r_k, d_ln_w, d_ln_b, final_norm_w)
    bp, lp, _ = x_prompt.shape
    bs = x_sample.shape[0]
    n_rows = -(-(bs + bp) // 8) * 8
    c_all = jnp.concatenate([c_sample, c_prompt, jnp.zeros((n_rows - bs - bp, D), F32)], axis=0)
    mod_all = _modulation(c_all, w_mod, b_mod)
    mod_s = mod_all[:, None]
    mod_p = mod_all[:, bs:bs + bp, None]

    zeros = lambda *s: jnp.zeros(s, F32)
    st_p = dict(a_conv=zeros(N_EVEN, bp, CONV_W - 1, W_A), a_h=zeros(N_EVEN, bp, W_A),
                b_s=zeros(N_EVEN, bp, H_B, DK_B, DV_B), c_c=zeros(N_ODD, bp, H_C, DK_C, DV_C),
                c_n=zeros(N_ODD, bp, H_C, DK_C), c_m=zeros(N_ODD, bp, H_C), d_shift=zeros(N_ODD, bp, P_D),
                d_s=zeros(N_ODD, bp, H_D, N_D, N_D))
    st_s = dict(a_conv=state_a_conv, a_h=state_a_h, b_s=state_b_s, c_c=state_c_c, c_n=state_c_n, c_m=state_c_m,
                d_shift=state_d_shift, d_s=state_d_s)
    out_s, ffn_bf16 = _trunk(x_sample.reshape(1, bs, D), mod_s, st_s, prm, decode=True)
    out_p, _ = _trunk(x_prompt, mod_p, st_p, dict(prm, ffn_bf16=ffn_bf16), decode=False)
    y_s = out_s[0].reshape(bs, 1, D)
    return (out_p[0], y_s) + out_p[1:] + out_s[1:]
```

```python
import functools

import jax
import jax.numpy as jnp
from jax import lax
from jax.experimental import pallas as pl
from jax.experimental.pallas import tpu as pltpu

F32 = jnp.float32
BF16 = jnp.bfloat16

D = 2048
DEPTH = 4
N_EVEN = 2
N_ODD = 2
D_FF = 5632
N_MOD = 9
EPS = 1e-6
CHUNK = 64
NEG_BIG = -1e30

W_A = 1024
NB_A = 8
BS_A = 128
CONV_W = 4
C_RGLRU = 8.0
H_B = 8
DK_B = 128
DV_B = 128
H_C = 4
DK_C = 128
DV_C = 256
N_D = 64
H_D = 16
W_D = 1024
R_W = 64
R_A = 64
R_G = 128
GN_EPS_D = 64e-5
P_EVEN = 6144
P_D = 3328
P_ODD = 6408
MLSTM_COLS = 3584
DEC_NB = 8
LANES = 128
VMEM_LIMIT = 48 * 2**20


def _cp(sem, vmem=VMEM_LIMIT):
    return pltpu.CompilerParams(dimension_semantics=sem, vmem_limit_bytes=vmem)


def _mm(a, b):
    return jnp.dot(a, b, preferred_element_type=F32)


def _dot(a, b):
    return _mm(a.astype(BF16), b.astype(BF16))


_NT = (((1,), (1,)), ((), ()))
_TN = (((0,), (0,)), ((), ()))
_NN = (((1,), (0,)), ((), ()))


def _dg(a, b, dn):
    return lax.dot_general(a, b, dn, preferred_element_type=F32)


def _dot_nt(a, b):
    return _dg(a.astype(BF16), b.astype(BF16), _NT)


def _dot_tn(a, b):
    return _dg(a.astype(BF16), b.astype(BF16), _TN)


def _split3(x):
    hi = x.astype(BF16)
    r1 = x - hi.astype(F32)
    mid = r1.astype(BF16)
    lo = (r1 - mid.astype(F32)).astype(BF16)
    return hi, mid, lo


def _split2(x):
    hi = x.astype(BF16)
    lo = (x - hi.astype(F32)).astype(BF16)
    return hi, lo


def _pieces(x, n):
    return _split3(x) if n == 3 else _split2(x)


def _sel_dot(mask_bf, x, n=3):
    return functools.reduce(jnp.add, [_mm(mask_bf, p) for p in _pieces(x, n)])


def _dot_sel(x, mask_bf, n=3):
    return functools.reduce(jnp.add, [_mm(p, mask_bf) for p in _pieces(x, n)])


def _dot_sel_tn(x, mask_bf, n=3):
    return functools.reduce(jnp.add, [_dg(p, mask_bf, _TN) for p in _pieces(x, n)])


def _mmx(a, b, dn=_NN):
    a_hi, a_lo = _split2(a)
    b_hi, b_lo = _split2(b)
    return _dg(a_hi, b_hi, dn) + _dg(a_hi, b_lo, dn) + _dg(a_lo, b_hi, dn)


def _iota(shape, dim):
    return lax.broadcasted_iota(jnp.int32, shape, dim)


def _sigmoid(x):
    return jax.nn.sigmoid(x)


def _softplus(x):
    return jnp.maximum(x, 0.0) + jnp.log1p(jnp.exp(-jnp.abs(x)))


def _log_sigmoid(x):
    return jnp.minimum(x, 0.0) - jnp.log1p(jnp.exp(-jnp.abs(x)))


def _gelu_tanh(x):
    return x * (0.5 * (1.0 + jnp.tanh(0.7978845608028654 * (x + 0.044715 * (x * x * x)))))


def _rms(x):
    return x * lax.rsqrt(jnp.mean(x * x, axis=-1, keepdims=True) + EPS)


def _adaln(x, shift, scale):
    return _rms(x) * (1.0 + scale) + shift


def _log2(n):
    k = n.bit_length() - 1
    assert (1 << k) == n
    return k


def _mod_spec(rows, width, layer, col):
    return pl.BlockSpec((None, None, rows, width), lambda g, i, n: (layer, g, 0, col))


def _mod_body(c_ref, w_ref, b_ref, o_ref):
    c = c_ref[...]
    cs = (c * _sigmoid(c)).astype(BF16)
    o_ref[...] = _mm(cs, w_ref[...].astype(BF16)) + b_ref[...]


def _modulation(c_all, w_mod, b_mod):
    rows = c_all.shape[0]
    tn = 1024
    return pl.pallas_call(
        _mod_body,
        grid=(DEPTH, N_MOD * D // tn),
        in_specs=[pl.BlockSpec((rows, D), lambda l, n: (0, 0)),
                  pl.BlockSpec((None, D, tn), lambda l, n: (l, 0, n)),
                  pl.BlockSpec((None, 1, tn), lambda l, n: (l, 0, n))],
        out_specs=pl.BlockSpec((None, rows, tn), lambda l, n: (l, 0, n)),
        out_shape=jax.ShapeDtypeStruct((DEPTH, rows, N_MOD * D), F32),
        compiler_params=_cp(("arbitrary", "arbitrary")),
        name="modulation",
    )(c_all, w_mod, b_mod.reshape(DEPTH, 1, N_MOD * D))


def _lower_bounds_body(g_ref, o_ref):
    g = g_ref[...]
    e = jnp.exp(g - jnp.max(g, axis=0, keepdims=True))
    sm = e / jnp.sum(e, axis=0, keepdims=True)
    acc = jnp.zeros_like(sm[0:1])
    for j in range(N_EVEN):
        acc = acc + sm[j:j + 1]
        o_ref[j:j + 1, :] = acc - sm[0:1]


def _lower_bounds(gamma):
    return pl.pallas_call(
        _lower_bounds_body,
        out_shape=jax.ShapeDtypeStruct(gamma.shape, F32),
        name="hgrn2_lower_bounds",
    )(gamma)


def _ffn_step(x_ref, sh_ref, sc_ref, gt_ref, wg_ref, wu_ref, wd_ref, o_ref, hf_ref, acc_ref):
    f = pl.program_id(2)

    @pl.when(f == 0)
    def _():
        hf_ref[...] = _adaln(x_ref[...], sh_ref[...], sc_ref[...]).astype(BF16)
        acc_ref[...] = jnp.zeros_like(acc_ref)

    hf = hf_ref[...]
    g = _mm(hf, wg_ref[...])
    u = _mm(hf, wu_ref[...])
    act = (g * _sigmoid(g) * u).astype(BF16)
    acc_ref[...] += _mm(act, wd_ref[...])

    @pl.when(f == pl.num_programs(2) - 1)
    def _():
        o_ref[...] = x_ref[...] + 0.5 * (1.0 + gt_ref[...]) * acc_ref[...]


def _ffn_body(x_ref, sh_ref, sc_ref, gt_ref, wg_ref, wu_ref, wd_ref, o_ref, hf_ref, acc_ref):
    _ffn_step(x_ref, sh_ref, sc_ref, gt_ref, wg_ref, wu_ref, wd_ref, o_ref, hf_ref, acc_ref)


def _ffn_cast_body(x_ref, sh_ref, sc_ref, gt_ref, wg_ref, wu_ref, wd_ref, o_ref, wgo_ref, wuo_ref, wdo_ref,
                   hf_ref, acc_ref):
    wgo_ref[...] = wg_ref[...].astype(BF16)
    wuo_ref[...] = wu_ref[...].astype(BF16)
    wdo_ref[...] = wd_ref[...].astype(BF16)
    _ffn_step(x_ref, sh_ref, sc_ref, gt_ref, wgo_ref, wuo_ref, wdo_ref, o_ref, hf_ref, acc_ref)


FFN_TF = 512


def _ffn_specs(x, mod, layer, which, tm):
    R = 1 if mod.shape[2] == 1 else tm
    kb = 6 * which
    return [pl.BlockSpec((None, tm, D), lambda g, i, f: (g, i, 0)),
            _mod_spec(R, D, layer, kb), _mod_spec(R, D, layer, kb + 1), _mod_spec(R, D, layer, kb + 2)]


def _ffn(x, mod, w_bf16, layer, which, tm):
    G, Lg, _ = x.shape
    tf = FFN_TF
    return pl.pallas_call(
        _ffn_body,
        grid=(G, Lg // tm, D_FF // tf),
        in_specs=_ffn_specs(x, mod, layer, which, tm) + [
            pl.BlockSpec((D, tf), lambda g, i, f: (0, f)), pl.BlockSpec((D, tf), lambda g, i, f: (0, f)),
            pl.BlockSpec((tf, D), lambda g, i, f: (f, 0))],
        out_specs=pl.BlockSpec((None, tm, D), lambda g, i, f: (g, i, 0)),
        out_shape=jax.ShapeDtypeStruct(x.shape, F32),
        scratch_shapes=[pltpu.VMEM((tm, D), BF16), pltpu.VMEM((tm, D), F32)],
        compiler_params=_cp(("arbitrary", "arbitrary", "arbitrary")),
        name="adaln_swiglu_ffn",
    )(x, mod, mod, mod, *w_bf16)


def _ffn_cast(x, mod, wg, wu, wd, layer, which):
    G, tm, _ = x.shape
    assert G == 1
    tf = FFN_TF // 2
    bf = lambda s: jax.ShapeDtypeStruct(s, BF16)
    outs = pl.pallas_call(
        _ffn_cast_body,
        grid=(1, 1, D_FF // tf),
        in_specs=_ffn_specs(x, mod, layer, which, tm) + [
            pl.BlockSpec((None, None, D, tf), lambda g, i, f: (layer, which, 0, f)),
            pl.BlockSpec((None, None, D, tf), lambda g, i, f: (layer, which, 0, f)),
            pl.BlockSpec((None, None, tf, D), lambda g, i, f: (layer, which, f, 0))],
        out_specs=[pl.BlockSpec((None, tm, D), lambda g, i, f: (g, i, 0)),
                   pl.BlockSpec((D, tf), lambda g, i, f: (0, f)), pl.BlockSpec((D, tf), lambda g, i, f: (0, f)),
                   pl.BlockSpec((tf, D), lambda g, i, f: (f, 0))],
        out_shape=[jax.ShapeDtypeStruct(x.shape, F32), bf((D, D_FF)), bf((D, D_FF)), bf((D_FF, D))],
        scratch_shapes=[pltpu.VMEM((tm, D), BF16), pltpu.VMEM((tm, D), F32)],
        compiler_params=_cp(("arbitrary", "arbitrary", "arbitrary")),
        name="adaln_swiglu_ffn_weight_cast",
    )(x, mod, mod, mod, wg, wu, wd)
    return outs[0], tuple(outs[1:])


def _inproj_body(x_ref, sh_ref, sc_ref, w_ref, o_ref, hf_ref, *, w_is_out_by_in):
    @pl.when(pl.program_id(2) == 0)
    def _():
        hf_ref[...] = _adaln(x_ref[...], sh_ref[...], sc_ref[...]).astype(BF16)

    o_ref[...] = _dg(hf_ref[...], w_ref[...], _NT if w_is_out_by_in else _NN)


def _inproj(x, mod, w, layer, j, tm, width=None, tn=512, w_is_out_by_in=False):
    G, Lg, _ = x.shape
    R = 1 if mod.shape[2] == 1 else tm
    n_out = w.shape[1] if w_is_out_by_in else w.shape[2]
    P = n_out if width is None else width
    assert P % tn == 0 and P <= n_out
    w_spec = (pl.BlockSpec((None, tn, D), lambda g, i, n: (j, n, 0)) if w_is_out_by_in
              else pl.BlockSpec((None, D, tn), lambda g, i, n: (j, 0, n)))
    return pl.pallas_call(
        functools.partial(_inproj_body, w_is_out_by_in=w_is_out_by_in),
        grid=(G, Lg // tm, P // tn),
        in_specs=[pl.BlockSpec((None, tm, D), lambda g, i, n: (g, i, 0)),
                  _mod_spec(R, D, layer, 3), _mod_spec(R, D, layer, 4), w_spec],
        out_specs=pl.BlockSpec((None, tm, tn), lambda g, i, n: (g, i, n)),
        out_shape=jax.ShapeDtypeStruct((G, Lg, P), F32),
        scratch_shapes=[pltpu.VMEM((tm, D), BF16)],
        compiler_params=_cp(("arbitrary", "arbitrary", "arbitrary")),
        name="adaln_in_projection",
    )(x, mod, mod, w)


def _outproj_body(x_ref, gt_ref, y1_ref, y2_ref, w1_ref, w2_ref, o_ref):
    y = _dot(y1_ref[...], w1_ref[...]) + _dot(y2_ref[...], w2_ref[...])
    o_ref[...] = x_ref[...] + (1.0 + gt_ref[...]) * y


def _outproj(x, mod, y1, y2, w, layer, j, tm):
    G, Lg, _ = x.shape
    R = 1 if mod.shape[2] == 1 else tm
    tn = D
    nb = D // tn
    half = y1.shape[-1]
    return pl.pallas_call(
        _outproj_body,
        grid=(G, Lg // tm, nb),
        in_specs=[pl.BlockSpec((None, tm, tn), lambda g, i, n: (g, i, n)),
                  pl.BlockSpec((None, None, R, tn), lambda g, i, n: (layer, g, 0, 5 * nb + n)),
                  pl.BlockSpec((None, tm, half), lambda g, i, n: (g, i, 0)),
                  pl.BlockSpec((None, tm, half), lambda g, i, n: (g, i, 0)),
                  pl.BlockSpec((None, None, half, tn), lambda g, i, n: (j, 0, 0, n)),
                  pl.BlockSpec((None, None, half, tn), lambda g, i, n: (j, 1, 0, n))],
        out_specs=pl.BlockSpec((None, tm, tn), lambda g, i, n: (g, i, n)),
        out_shape=jax.ShapeDtypeStruct(x.shape, F32),
        compiler_params=_cp(("arbitrary", "arbitrary", "arbitrary")),
        name="out_projection_residual",
    )(x, mod, y1, y2, w, w)


def _final_body(x_ref, w_ref, o_ref):
    o_ref[...] = _rms(x_ref[...]) * w_ref[...]


def _final_norm(x, w, tm):
    G, Lg, _ = x.shape
    return pl.pallas_call(
        _final_body,
        grid=(G, Lg // tm),
        in_specs=[pl.BlockSpec((None, tm, D), lambda g, i: (g, i, 0)),
                  pl.BlockSpec((1, D), lambda g, i: (0, 0))],
        out_specs=pl.BlockSpec((None, tm, D), lambda g, i: (g, i, 0)),
        out_shape=jax.ShapeDtypeStruct(x.shape, F32),
        compiler_params=_cp(("arbitrary", "arbitrary")),
        name="final_rmsnorm",
    )(x, w.reshape(1, D))


def _rglru_gates(x0, x1, x2, x3, cw, cb, wr, br, wi, bi, lam):
    u = cb + x0 * cw[0:1] + x1 * cw[1:2] + x2 * cw[2:3] + x3 * cw[3:4]
    r_parts, i_parts = [], []
    for n in range(NB_A):
        un = u[:, n * BS_A:(n + 1) * BS_A].astype(BF16)
        r_parts.append(_mm(un, wr[n]))
        i_parts.append(_mm(un, wi[n]))
    r = _sigmoid(jnp.concatenate(r_parts, axis=1) + br)
    ig = _sigmoid(jnp.concatenate(i_parts, axis=1) + bi)
    log_a = -C_RGLRU * r * _softplus(-lam)
    a = jnp.exp(log_a)
    b = jnp.sqrt(-jnp.tanh(log_a) * (a * a + 1.0)) * (ig * u)
    return a, b


def _rglru_seq_body(x_ref, halo_ref, buf_ref, cw_ref, cb_ref, wr_ref, br_ref, wi_ref, bi_ref, lam_ref,
                    a_ref, b_ref, *, tt):
    i = pl.program_id(1)
    halo = jnp.where(i == 0, buf_ref[...], halo_ref[...])
    full = jnp.concatenate([halo, x_ref[...]], axis=0)
    taps = [pltpu.roll(full, CONV_W - 1 - k, axis=0)[8:8 + tt] for k in range(CONV_W - 1)]
    a, b = _rglru_gates(taps[0], taps[1], taps[2], x_ref[...], cw_ref[...], cb_ref[...], wr_ref[...], br_ref[...],
                        wi_ref[...], bi_ref[...], lam_ref[...])
    a_ref[...] = a
    b_ref[...] = b


def _rglru_param_specs(nidx):
    z2 = lambda *_: (0, 0)
    z3 = lambda *_: (0, 0, 0)
    del nidx
    return [pl.BlockSpec((CONV_W, W_A), z2), pl.BlockSpec((1, W_A), z2),
            pl.BlockSpec((NB_A, BS_A, BS_A), z3), pl.BlockSpec((1, W_A), z2),
            pl.BlockSpec((NB_A, BS_A, BS_A), z3), pl.BlockSpec((1, W_A), z2),
            pl.BlockSpec((1, W_A), z2)]


def _rglru_seq(z, buf8, prm, tt):
    B, L, _ = z.shape
    hb = tt // 8
    out = jax.ShapeDtypeStruct((B, L, W_A), F32)
    return pl.pallas_call(
        functools.partial(_rglru_seq_body, tt=tt),
        grid=(B, L // tt),
        in_specs=[pl.BlockSpec((None, tt, W_A), lambda b, i: (b, i, 0)),
                  pl.BlockSpec((None, 8, W_A), lambda b, i: (b, jnp.maximum(i * hb - 1, 0), 0)),
                  pl.BlockSpec((None, 8, W_A), lambda b, i: (b, 0, 0))] + _rglru_param_specs(2),
        out_specs=[pl.BlockSpec((None, tt, W_A), lambda b, i: (b, i, 0))] * 2,
        out_shape=[out, out],
        compiler_params=_cp(("arbitrary", "arbitrary")),
        name="rglru_conv_gates",
    )(z, z, buf8, *prm)


def _rglru_scan_body(a_ref, b_ref, ag_ref, h0_ref, y_ref, hl_ref, h_scr, *, tt):
    @pl.when(pl.program_id(1) == 0)
    def _():
        h_scr[...] = h0_ref[...]

    def step(t, h):
        h = a_ref[t] * h + b_ref[t]
        y_ref[t] = h
        return h

    h = lax.fori_loop(0, tt, step, h_scr[...], unroll=8)
    h_scr[...] = h
    hl_ref[...] = h
    y_ref[...] = y_ref[...] * _gelu_tanh(ag_ref[...])


def _rglru_scan(a, b, z, h0, tt):
    B, L, _ = a.shape
    a4 = a.reshape(B, L, 8, LANES)
    b4 = b.reshape(B, L, 8, LANES)
    z4 = z.reshape(B, L, z.shape[-1] // LANES, LANES)
    spec = pl.BlockSpec((None, tt, 8, LANES), lambda bb, i: (bb, i, 0, 0))
    y, hl = pl.pallas_call(
        functools.partial(_rglru_scan_body, tt=tt),
        grid=(B, L // tt),
        in_specs=[spec, spec,
                  pl.BlockSpec((None, tt, 8, LANES), lambda bb, i: (bb, i, 1, 0)),
                  pl.BlockSpec((None, 8, LANES), lambda bb, i: (bb, 0, 0))],
        out_specs=[spec, pl.BlockSpec((None, 8, LANES), lambda bb, i: (bb, 0, 0))],
        out_shape=[jax.ShapeDtypeStruct((B, L, 8, LANES), F32), jax.ShapeDtypeStruct((B, 8, LANES), F32)],
        scratch_shapes=[pltpu.VMEM((8, LANES), F32)],
        compiler_params=_cp(("arbitrary", "arbitrary")),
        name="rglru_scan_gelu_gate",
    )(a4, b4, z4, h0.reshape(B, 8, LANES))
    return y.reshape(B, L, W_A), hl.reshape(B, W_A)


def _rglru_step_body(x3_ref, ag_ref, x0_ref, x1_ref, x2_ref, h0_ref, cw_ref, cb_ref, wr_ref, br_ref, wi_ref, bi_ref,
                     lam_ref, y_ref, h_ref):
    a, b = _rglru_gates(x0_ref[...], x1_ref[...], x2_ref[...], x3_ref[...], cw_ref[...], cb_ref[...], wr_ref[...],
                        br_ref[...], wi_ref[...], bi_ref[...], lam_ref[...])
    h = a * h0_ref[...] + b
    h_ref[...] = h
    y_ref[...] = h * _gelu_tanh(ag_ref[...])


def _rglru_step(z, buf, h0, prm):
    n = z.shape[0]
    row = lambda c: pl.BlockSpec((n, W_A), lambda i: (0, c))
    out = jax.ShapeDtypeStruct((n, W_A), F32)
    return pl.pallas_call(
        _rglru_step_body,
        grid=(1,),
        in_specs=[row(0), row(1), row(0), row(0), row(0), row(0)] + _rglru_param_specs(1),
        out_specs=[row(0), row(0)],
        out_shape=[out, out],
        compiler_params=_cp(("arbitrary",)),
        name="rglru_decode_step",
    )(z, z, buf[:, 0], buf[:, 1], buf[:, 2], h0, *prm)


def _live_rows(C, first, valid):
    row = _iota((C, 1), 0)
    return (row >= first) & (row < first + valid)


def _hgrn2_chunk(q, fp, v, gate, lb, nw, S, *, C, valid, first=0):
    SB = min(8, C)
    k = (1.0 - lb) * _sigmoid(-fp)
    lf = jnp.log1p(-k)
    if valid < C:
        live = _live_rows(C, first, valid)
        k = jnp.where(live, k, 0.0)
        lf = jnp.where(live, lf, 0.0)
    g_col = _dot_sel_tn(lf, jnp.ones((C, DV_B), BF16))
    if valid == 1:
        kv = _dot_tn(k, v)
        yield
        S_new = jnp.exp(g_col) * S + kv
        o = _dot(q, S_new)
        yield
        return _rms(o) * nw * (gate * _sigmoid(gate)), S_new
    tril = _iota((C, C), 0) >= _iota((C, C), 1)
    b = _sel_dot(jnp.where(tril, 1.0, 0.0).astype(BF16), lf)
    yield
    g_row = b[C - 1:C, :]
    o = _dot(q * jnp.exp(b), S)
    khat = k * jnp.exp(g_row - b)
    S_new = jnp.exp(g_col) * S + _dot_tn(khat, v)

    tril_sb = _iota((SB, SB), 0) >= _iota((SB, SB), 1)
    atts = []
    for i in range(C // SB):
        lo = i * SB
        qi, bi, ki = q[lo:lo + SB], b[lo:lo + SB], k[lo:lo + SB]
        ci = bi - jnp.log(ki)
        att = jnp.where(tril_sb, jnp.sum(qi[:, None, :] * jnp.exp(bi[:, None, :] - ci[None, :, :]), axis=-1), 0.0)
        off = None
        if i > 0:
            ref = b[lo - 1:lo, :]
            qt = qi * jnp.exp(bi - ref)
            kt = k[0:lo] * jnp.exp(ref - b[0:lo])
            off = _dot_nt(qt, kt)
        atts.append((att, off))
    yield
    parts = []
    for i, (att, off) in enumerate(atts):
        lo = i * SB
        oi = _dot(att, v[lo:lo + SB])
        parts.append(oi if off is None else oi + _dot(off, v[0:lo]))
    yield
    o = o + (parts[0] if len(parts) == 1 else jnp.concatenate(parts, axis=0))
    return _rms(o) * nw * (gate * _sigmoid(gate)), S_new


def _merge_rows(outs, C):
    row = _iota((C, 1), 0)
    y = jnp.where(row == 0, outs[0], 0.0)
    for s in range(1, len(outs)):
        y = jnp.where(row == s, outs[s], y)
    return y


def _chained(prev, out_index, operands, specs):
    if prev is None:
        return {}
    operands.append(prev)
    specs.append(pl.BlockSpec(memory_space=pl.ANY))
    return {len(operands) - 1: out_index}


def _hgrn2_body(q_ref, f_ref, v_ref, g_ref, lb_ref, nw_ref, s0_ref, *rest, C, valid, nb, hp, decode, nchunk):
    y_ref, s1_ref = rest[-2:]

    @pl.when(pl.program_id(2) == 0)
    def _():
        s1_ref[...] = s0_ref[...]

    nw = nw_ref[...]
    probs = [(bi, hh) for bi in range(nb) for hh in range(hp)]

    def chunk(c, carry):
        rows = pl.ds(pl.multiple_of(c * C, C), C)
        gens = []
        for bi, hh in probs:
            lanes = slice(hh * LANES, (hh + 1) * LANES)
            src = 0 if decode else bi
            gens.append(_hgrn2_chunk(q_ref[src, rows, lanes], f_ref[src, rows, lanes], v_ref[src, rows, lanes],
                                     g_ref[src, rows, lanes], lb_ref[:, lanes], nw, s1_ref[bi, hh],
                                     C=C, valid=valid, first=bi if decode else 0))
        outs = _interleave(gens)
        for (bi, hh), (y, s_new) in zip(probs, outs):
            s1_ref[bi, hh] = s_new
            if not decode:
                y_ref[bi, rows, hh * LANES:(hh + 1) * LANES] = y
        if decode:
            for hh in range(hp):
                y_ref[0, rows, hh * LANES:(hh + 1) * LANES] = _merge_rows([outs[bi * hp + hh][0] for bi in range(nb)], C)
        return carry

    lax.fori_loop(0, nchunk, chunk, 0)


def _hgrn2(z, lb, nw, s_all, s_prev, j, *, C, valid, nb, hp, tt, decode):
    B, L, _ = z.shape
    zb = 1 if decode else nb
    w = hp * LANES
    col = lambda base: pl.BlockSpec((zb, tt, w), lambda b, h, i: (b, i, base // hp + h))
    st = pl.BlockSpec((None, nb, hp, DK_B, DV_B), lambda b, h, i: (j, b, h, 0, 0))
    operands = [z, z, z, z, lb, nw, s_all]
    specs = [col(16), col(24), col(32), col(40), pl.BlockSpec((1, w), lambda b, h, i: (0, h)),
             pl.BlockSpec((1, LANES), lambda b, h, i: (0, 0)), st]
    aliases = _chained(s_prev, 1, operands, specs)
    return pl.pallas_call(
        functools.partial(_hgrn2_body, C=C, valid=valid, nb=nb, hp=hp, decode=decode, nchunk=tt // C),
        grid=(B // zb, H_B // hp, L // tt),
        in_specs=specs,
        out_specs=[col(0), st],
        out_shape=[jax.ShapeDtypeStruct((B, L, H_B * DV_B), F32), jax.ShapeDtypeStruct(s_all.shape, F32)],
        input_output_aliases=aliases,
        compiler_params=_cp(("arbitrary", "arbitrary", "arbitrary")),
        name="hgrn2_chunked",
    )(*operands)


def _mlstm_step(q, k, v, op, gt, h, nw, Cst, n, m, *, C, first):
    lane = _iota((C, LANES), 1)
    live = _live_rows(C, first, 1)
    i0 = jnp.sum(jnp.where(live & (lane == h), gt, 0.0), keepdims=True)
    lf0 = _log_sigmoid(jnp.sum(jnp.where(live & (lane == h + H_C), gt, 0.0), keepdims=True))
    inter = lf0 + m
    m_t = jnp.maximum(inter, i0)
    w_inter = jnp.exp(inter - m_t)
    wk = jnp.exp(i0 - m_t)
    kh = k * (DK_C ** -0.5)
    s = jnp.sum(q * kh, axis=1, keepdims=True) * wk
    qc = _dot(q, Cst)
    kw = jnp.where(live, kh * wk, 0.0)
    kv = _dot_tn(kw, v)
    yield
    num = w_inter * qc + s * v
    den = w_inter * jnp.sum(q * n, axis=1, keepdims=True) + s
    hh = num / jnp.maximum(jnp.abs(den), jnp.exp(-m_t))
    C_new = w_inter * Cst + kv
    n_new = w_inter * n + jnp.sum(kw, axis=0, keepdims=True)
    y = _rms(hh) * nw * _sigmoid(op)
    return y, C_new, n_new, m_t


def _mlstm_chunk(q, k, v, op, gt, h, nw, Cst, n, m, *, C, valid, first=0):
    if valid == 1:
        return (yield from _mlstm_step(q, k, v, op, gt, h, nw, Cst, n, m, C=C, first=first))
    assert first == 0
    gtT = gt.T
    lane = _iota((C, LANES), 1)
    sub = _iota((LANES, C), 0)
    i_col = jnp.sum(jnp.where(lane == h, gt, 0.0), axis=1, keepdims=True)
    f_col = jnp.sum(jnp.where(lane == h + H_C, gt, 0.0), axis=1, keepdims=True)
    i_row = jnp.sum(jnp.where(sub == h, gtT, 0.0), axis=0, keepdims=True)
    f_row = jnp.sum(jnp.where(sub == h + H_C, gtT, 0.0), axis=0, keepdims=True)
    lf_col = _log_sigmoid(f_col)
    lf_row = _log_sigmoid(f_row)
    if valid < C:
        live_c = _iota((C, 1), 0) < valid
        live_r = _iota((1, C), 1) < valid
        i_col = jnp.where(live_c, i_col, NEG_BIG)
        i_row = jnp.where(live_r, i_row, NEG_BIG)
        lf_col = jnp.where(live_c, lf_col, 0.0)
        lf_row = jnp.where(live_r, lf_row, 0.0)
    tril = _iota((C, C), 0) >= _iota((C, C), 1)
    b_col = jnp.sum(jnp.where(tril, lf_row, 0.0), axis=1, keepdims=True)
    b_row = jnp.sum(jnp.where(_iota((C, C), 0) <= _iota((C, C), 1), lf_col, 0.0), axis=0, keepdims=True)
    dm = jnp.where(tril, b_col - b_row + i_row, NEG_BIG)
    inter = b_col + m
    m_t = jnp.maximum(inter, jnp.max(dm, axis=1, keepdims=True))
    w_inter = jnp.exp(inter - m_t)
    kh = k * (DK_C ** -0.5)
    qk = _dot_nt(q, kh)
    qc = _dot(q, Cst)
    g = b_col[C - 1:C, :]
    m_new = m_t[C - 1:C, :]
    wk = jnp.exp(g - b_col + i_col - m_new)
    f_state = jnp.exp(g + m - m_new)
    kw = kh * wk
    kv = _dot_tn(kw, v)
    yield
    s = qk * jnp.exp(dm - m_t)
    sv = _dot(s, v)
    yield
    num = w_inter * qc + sv
    den = w_inter * jnp.sum(q * n, axis=1, keepdims=True) + jnp.sum(s, axis=1, keepdims=True)
    hh = num / jnp.maximum(jnp.abs(den), jnp.exp(-m_t))
    C_new = f_state * Cst + kv
    n_new = f_state * n + jnp.sum(kw, axis=0, keepdims=True)
    y = _rms(hh) * nw * _sigmoid(op)
    return y, C_new, n_new, m_new


def _mlstm_body(q_ref, k_ref, v_ref, o_ref, g_ref, gb_ref, nw_ref, c0_ref, n0_ref, m0_ref, *rest,
                C, valid, nb, hp, decode, nchunk):
    y_ref, c1_ref, n1_ref, m1_ref = rest[-4:]

    @pl.when(pl.program_id(2) == 0)
    def _():
        c1_ref[...] = c0_ref[...]
        n1_ref[...] = n0_ref[...]
        m1_ref[...] = m0_ref[...]

    gb = gb_ref[...]
    nw = nw_ref[...]
    probs = [(bi, hh) for bi in range(nb) for hh in range(hp)]

    def chunk(c, carry):
        rows = pl.ds(pl.multiple_of(c * C, C), C)
        gens = []
        for bi, hh in probs:
            src = 0 if decode else bi
            kl = slice(hh * DK_C, (hh + 1) * DK_C)
            vl = slice(hh * DV_C, (hh + 1) * DV_C)
            gens.append(_mlstm_chunk(
                q_ref[src, rows, kl], k_ref[src, rows, kl], v_ref[src, rows, vl], o_ref[src, rows, vl],
                g_ref[src, rows, :] + gb, pl.program_id(1) * hp + hh, nw, c1_ref[bi, hh], n1_ref[bi, hh],
                m1_ref[bi, hh][:, 0:1], C=C, valid=valid, first=bi if decode else 0))
        outs = _interleave(gens)
        for (bi, hh), (y, c_new, n_new, m_new) in zip(probs, outs):
            if not decode:
                y_ref[bi, rows, hh * DV_C:(hh + 1) * DV_C] = y
            c1_ref[bi, hh] = c_new
            n1_ref[bi, hh] = n_new
            m1_ref[bi, hh] = jnp.broadcast_to(m_new, (1, LANES))
        if decode:
            for hh in range(hp):
                y_ref[0, rows, hh * DV_C:(hh + 1) * DV_C] = _merge_rows([outs[bi * hp + hh][0] for bi in range(nb)], C)
        return carry

    lax.fori_loop(0, nchunk, chunk, 0)


def _mlstm(z, gb, nw, states, prev, j, *, C, valid, nb, hp, tt, decode):
    B, L, _ = z.shape
    zb = 1 if decode else nb
    c128 = lambda base: pl.BlockSpec((zb, tt, hp * DK_C), lambda b, h, i: (b, i, base // hp + h))
    c256 = lambda base: pl.BlockSpec((zb, tt, hp * DV_C), lambda b, h, i: (b, i, base // hp + h))
    cst = pl.BlockSpec((None, nb, hp, DK_C, DV_C), lambda b, h, i: (j, b, h, 0, 0))
    vec = pl.BlockSpec((None, nb, hp, 1, LANES), lambda b, h, i: (j, b, h, 0, 0))
    operands = [z, z, z, z, z, gb, nw, *states]
    specs = [c128(0), c128(4), c256(4), c256(8), pl.BlockSpec((zb, tt, LANES), lambda b, h, i: (b, i, 24)),
             pl.BlockSpec((1, LANES), lambda b, h, i: (0, 0)), pl.BlockSpec((1, DV_C), lambda b, h, i: (0, 0)),
             cst, vec, vec]
    aliases = {}
    for k, p in enumerate(prev or ()):
        aliases.update(_chained(p, 1 + k, operands, specs))
    outs = pl.pallas_call(
        functools.partial(_mlstm_body, C=C, valid=valid, nb=nb, hp=hp, decode=decode, nchunk=tt // C),
        grid=(B // zb, H_C // hp, L // tt),
        in_specs=specs,
        out_specs=[c256(0), cst, vec, vec],
        out_shape=[jax.ShapeDtypeStruct((B, L, H_C * DV_C), F32)] + [jax.ShapeDtypeStruct(s.shape, F32) for s in states],
        input_output_aliases=aliases,
        compiler_params=_cp(("arbitrary", "arbitrary", "arbitrary")),
        name="mlstm_chunked",
    )(*operands)
    return outs[0], tuple(outs[1:])


def _head_sum(x, bd):
    return jnp.concatenate([_dot_sel(x[:, n * LANES:(n + 1) * LANES], bd) for n in range(W_D // LANES)], axis=1)


def _rwkv_pre(zd, sh, mu, w0, a0, kkw, ka, rk, w2p, a2p, g2, bd):
    zs = zd + (sh - zd) * mu
    r, k, v = zs[:, 0:W_D], zs[:, W_D:2 * W_D], zs[:, 2 * W_D:3 * W_D]
    wa = zs[:, 3 * W_D:3 * W_D + LANES]
    gl = zs[:, 3 * W_D + LANES:]
    w = -_softplus(-(w0 + _dot(jnp.tanh(wa), w2p))) - 0.5
    logd = -jnp.exp(w)
    a = _sigmoid(a0 + _dot(wa, a2p))
    gate = _dot(_sigmoid(gl), g2)
    kk = k * kkw
    kk = kk / jnp.maximum(jnp.sqrt(_head_sum(kk * kk, bd)), 1e-12)
    kmod = k * (1.0 + (a - 1.0) * ka)
    bonus = _head_sum(r * kmod * rk, bd) * v
    return r, logd, kmod, v, kk, kk * a, gate, bonus


def _head_block_ones():
    return jnp.where((_iota((LANES, LANES), 0) >> 6) == (_iota((LANES, LANES), 1) >> 6), 1.0, 0.0).astype(BF16)


def _rwkv_pre_seq_body(z_ref, halo_ref, prev_ref, mu_ref, w0_ref, a0_ref, kk_ref, ka_ref, rk_ref, w2_ref, a2_ref,
                       g2_ref, *out_refs, tt):
    i = pl.program_id(1)
    halo = jnp.where(i == 0, prev_ref[...], halo_ref[...])
    zd = z_ref[...]
    sh = pltpu.roll(jnp.concatenate([halo, zd], axis=0), 1, axis=0)[8:8 + tt]
    outs = _rwkv_pre(zd, sh, mu_ref[...], w0_ref[...], a0_ref[...], kk_ref[...], ka_ref[...], rk_ref[...],
                     w2_ref[...], a2_ref[...], g2_ref[...], _head_block_ones())
    for ref, val in zip(out_refs, outs):
        ref[...] = val


def _rwkv_pre_step_body(z_ref, sh_ref, mu_ref, w0_ref, a0_ref, kk_ref, ka_ref, rk_ref, w2_ref, a2_ref, g2_ref,
                        *out_refs):
    outs = _rwkv_pre(z_ref[...], sh_ref[...], mu_ref[...], w0_ref[...], a0_ref[...], kk_ref[...], ka_ref[...],
                     rk_ref[...], w2_ref[...], a2_ref[...], g2_ref[...], _head_block_ones())
    for ref, val in zip(out_refs[:6], outs[:6]):
        ref[...] = val.T
    for ref, val in zip(out_refs[6:], outs[6:]):
        ref[...] = val


def _rwkv_pre_param_specs():
    z2 = lambda *_: (0, 0)
    return ([pl.BlockSpec((1, P_D), z2)] + [pl.BlockSpec((1, W_D), z2)] * 5
            + [pl.BlockSpec((LANES, W_D), z2)] * 3)


def _rwkv_pre_seq(z, prev8, prm, tt):
    B, L, _ = z.shape
    hb = tt // 8
    out = jax.ShapeDtypeStruct((B, L, W_D), F32)
    return pl.pallas_call(
        functools.partial(_rwkv_pre_seq_body, tt=tt),
        grid=(B, L // tt),
        in_specs=[pl.BlockSpec((None, tt, P_D), lambda b, i: (b, i, 0)),
                  pl.BlockSpec((None, 8, P_D), lambda b, i: (b, jnp.maximum(i * hb - 1, 0), 0)),
                  pl.BlockSpec((None, 8, P_D), lambda b, i: (b, 0, 0))] + _rwkv_pre_param_specs(),
        out_specs=[pl.BlockSpec((None, tt, W_D), lambda b, i: (b, i, 0))] * 8,
        out_shape=[out] * 8,
        compiler_params=_cp(("arbitrary", "arbitrary")),
        name="rwkv7_token_shift_features",
    )(z, z, prev8, *prm)


def _rwkv_pre_step(z, shifted, prm):
    n = z.shape[0]
    fm, tm = jax.ShapeDtypeStruct((W_D, n), F32), jax.ShapeDtypeStruct((n, W_D), F32)
    whole = lambda s: pl.BlockSpec(s.shape, lambda i: (0, 0))
    outs = pl.pallas_call(
        _rwkv_pre_step_body,
        grid=(1,),
        in_specs=[pl.BlockSpec((n, P_D), lambda i: (0, 0)), pl.BlockSpec((n, P_D), lambda i: (0, 0))]
        + _rwkv_pre_param_specs(),
        out_specs=[whole(fm)] * 6 + [whole(tm)] * 2,
        out_shape=[fm] * 6 + [tm] * 2,
        compiler_params=_cp(("arbitrary",)),
        name="rwkv7_decode_features",
    )(z, shifted, *prm)
    return outs[:6], outs[6], outs[7]


def _unit_lower_inverse(N, C):
    n = N.shape[0]
    ri, ci = _iota((n, n), 0), _iota((n, n), 1)
    base = min(8, C)
    kb = _log2(base)
    X = jnp.where((ri >> kb) == (ci >> kb), N, 0.0)
    T = jnp.where(ri == ci, 1.0, 0.0) + X
    for _ in range(kb - 1):
        X = _dot(X, X)
        yield
        T = T + _dot(T, X)
        yield
    size = base
    while size < C:
        ks = _log2(size)
        sel = ((ri >> (ks + 1)) == (ci >> (ks + 1))) & (((ri >> ks) & 1) == 1) & (((ci >> ks) & 1) == 0)
        TN_ = _dot(T, jnp.where(sel, N, 0.0))
        yield
        T = T + _dot(TN_, T)
        yield
        size *= 2
    return T


def _interleave(gens):
    results = [None] * len(gens)
    live = list(range(len(gens)))
    while live:
        for i in list(live):
            try:
                next(gens[i])
            except StopIteration as stop:
                results[i] = stop.value
                live.remove(i)
    return results


def _rwkv_groupnorm_gate(y, gate, bonus, lnw, lnb):
    bd = _head_block_ones()
    mean = _dot_sel(y, bd, 2) * (1.0 / N_D)
    yield
    dy = y - mean
    var = _dot(dy * dy, bd) * (1.0 / N_D)
    yield
    return (dy * lax.rsqrt(var + GN_EPS_D) * lnw + lnb + bonus) * gate


def _rwkv_chunk(r, ld, k, v, kk, be, gate, bonus, lnw, lnb, P, *, C, valid, first=0):
    if valid < C:
        live = _live_rows(C, first, valid)
        ld = jnp.where(live, ld, 0.0)
        be = jnp.where(live, be, 0.0)
        k = jnp.where(live, k, 0.0)
        v = jnp.where(live, v, 0.0)
    tril = _iota((C, C), 0) >= _iota((C, C), 1)
    c = _sel_dot(jnp.where(tril, 1.0, 0.0).astype(BF16), ld, 2)
    g_col = _dot_sel_tn(ld, jnp.ones((C, LANES), BF16), 2)
    yield
    ec, enc = jnp.exp(c), jnp.exp(-c)
    lane0 = _iota((C, LANES), 1) < N_D

    def stack(x):
        return jnp.concatenate([jnp.where(lane0, x, 0.0), jnp.where(lane0, 0.0, x)], axis=0)

    n2 = 2 * C
    AR = jnp.concatenate([stack(-kk * jnp.exp(c - ld)), stack(r * ec)], axis=0).astype(BF16)
    BK = jnp.concatenate([stack(be * enc), stack(k * enc)], axis=0).astype(BF16)
    V2 = stack(v).astype(BF16)
    kc = _log2(C)
    ri, ci = _iota((2 * n2, 2 * n2), 0), _iota((2 * n2, 2 * n2), 1)
    tpos, spos = ri & (C - 1), ci & (C - 1)
    keep = (((ri >> kc) & 1) == ((ci >> kc) & 1)) & ((spos < tpos) | ((ri >= n2) & (spos == tpos)))
    G = jnp.where(keep, _dg(AR, BK, _NT), 0.0)
    LP = _mm(AR, P.astype(BF16))
    yield
    GV = _mm(G[:, n2:].astype(BF16), V2)
    T = yield from _unit_lower_inverse(G[0:n2, 0:n2], C)
    U = _dot(T, LP[0:n2] + GV[0:n2])
    yield
    Y2 = LP[n2:] + GV[n2:] + _dot(G[n2:, 0:n2], U)
    P_new = jnp.exp(g_col) * (P + _dg(BK, jnp.concatenate([U.astype(BF16), V2], axis=0), _TN))
    yield
    y = Y2[0:C] + Y2[C:n2]
    out = yield from _rwkv_groupnorm_gate(y, gate, bonus, lnw, lnb)
    return out, P_new


def _rwkv_body(r_ref, ld_ref, k_ref, v_ref, kk_ref, be_ref, gt_ref, bo_ref, lnw_ref, lnb_ref, p0_ref,
               y_ref, p1_ref, *, C, valid, nb, hp, nchunk):
    @pl.when(pl.program_id(2) == 0)
    def _():
        p1_ref[...] = p0_ref[...]

    probs = [(bi, pp) for bi in range(nb) for pp in range(hp)]
    in_refs = (r_ref, ld_ref, k_ref, v_ref, kk_ref, be_ref, gt_ref, bo_ref)

    def chunk(c, carry):
        rows = pl.ds(pl.multiple_of(c * C, C), C)
        gens = []
        for bi, pp in probs:
            lanes = slice(pp * LANES, (pp + 1) * LANES)
            args = [ref[bi, rows, lanes] for ref in in_refs] + [lnw_ref[:, lanes], lnb_ref[:, lanes], p1_ref[bi, pp]]
            gens.append(_rwkv_chunk(*args, C=C, valid=valid))
        for (bi, pp), (y, p_new) in zip(probs, _interleave(gens)):
            p1_ref[bi, pp] = p_new
            y_ref[bi, rows, pp * LANES:(pp + 1) * LANES] = y
        return carry

    lax.fori_loop(0, nchunk, chunk, 0)


def _rwkv_decode_body(r_ref, ld_ref, k_ref, v_ref, kk_ref, be_ref, s0_ref, *rest):
    y_ref, s1_ref = rest[-2:]
    S = s0_ref[...]
    sa = -jnp.sum(S * kk_ref[...][None], axis=1)
    S = S * jnp.exp(ld_ref[...])[None] + sa[:, None, :] * be_ref[...][None] + v_ref[...][:, None, :] * k_ref[...][None]
    s1_ref[...] = S
    y_ref[...] = jnp.sum(S * r_ref[...][None], axis=1)


def _rwkv_decode(feats_fm, s_all, s_prev, j):
    n = feats_fm[0].shape[1]
    row = pl.BlockSpec((N_D, n), lambda h: (h, 0))
    st = pl.BlockSpec((None, None, N_D, N_D, n), lambda h: (j, h, 0, 0, 0))
    operands = [*feats_fm, s_all]
    specs = [row] * 6 + [st]
    aliases = _chained(s_prev, 1, operands, specs)
    return pl.pallas_call(
        _rwkv_decode_body,
        grid=(H_D,),
        in_specs=specs,
        out_specs=[row, st],
        out_shape=[jax.ShapeDtypeStruct((W_D, n), F32), jax.ShapeDtypeStruct(s_all.shape, F32)],
        input_output_aliases=aliases,
        compiler_params=_cp(("arbitrary",)),
        name="rwkv7_decode_step",
    )(*operands)


def _rwkv_decode_out_body(y_ref, gt_ref, bo_ref, lnw_ref, lnb_ref, o_ref):
    y = y_ref[...].T
    bd = _head_block_ones()
    dy = y - _head_sum(y, bd) * (1.0 / N_D)
    var = _head_sum(dy * dy, bd) * (1.0 / N_D)
    o_ref[...] = (dy * lax.rsqrt(var + GN_EPS_D) * lnw_ref[...] + lnb_ref[...] + bo_ref[...]) * gt_ref[...]


def _rwkv_decode_out(y_fm, gate, bonus, lnw, lnb):
    n = y_fm.shape[1]
    return pl.pallas_call(
        _rwkv_decode_out_body,
        out_shape=jax.ShapeDtypeStruct((n, W_D), F32),
        name="rwkv7_decode_groupnorm_gate",
    )(y_fm, gate, bonus, lnw, lnb)


def _rwkv(feats, lnw, lnb, p0, *, C, valid, nb, hp, tt):
    B, L, _ = feats[0].shape
    w = hp * LANES
    col = pl.BlockSpec((nb, tt, w), lambda b, p, i: (b, i, p))
    vec = pl.BlockSpec((1, w), lambda b, p, i: (0, p))
    st = pl.BlockSpec((nb, hp, LANES, LANES), lambda b, p, i: (b, p, 0, 0))
    return pl.pallas_call(
        functools.partial(_rwkv_body, C=C, valid=valid, nb=nb, hp=hp, nchunk=tt // C),
        grid=(B // nb, H_D // 2 // hp, L // tt),
        in_specs=[col] * 8 + [vec, vec, st],
        out_specs=[col, st],
        out_shape=[jax.ShapeDtypeStruct((B, L, W_D), F32), jax.ShapeDtypeStruct(p0.shape, F32)],
        compiler_params=_cp(("arbitrary", "arbitrary", "arbitrary")),
        name="rwkv7_chunked",
    )(*feats, lnw, lnb, p0)


def _pair_blockdiag(s):
    B = s.shape[0]
    st = s.reshape(B, H_D // 2, 2, N_D, N_D)
    zero = jnp.zeros_like(st[:, :, 0])
    top = jnp.concatenate([st[:, :, 0], zero], axis=-1)
    bot = jnp.concatenate([zero, st[:, :, 1]], axis=-1)
    return jnp.concatenate([top, bot], axis=-2)


def _pair_unblockdiag(p):
    B = p.shape[0]
    st = jnp.stack([p[:, :, :N_D, :N_D], p[:, :, N_D:, N_D:]], axis=2)
    return st.reshape(B, H_D, N_D, N_D)


def _trunk(x, mod, st, prm, *, decode):
    G, Lg, _ = x.shape
    tm = min(128 if decode else 512, Lg)
    tm_in = tm if decode or Lg % 1024 else 1024
    assert decode or Lg % 512 == 0
    nseq = Lg if decode else G
    new = {k: [] for k in ("a_conv", "a_h", "d_shift", "d_s")}
    b_s = None
    c_in = (st["c_c"], st["c_n"][:, :, :, None, :],
            jnp.broadcast_to(st["c_m"][..., None, None], st["c_m"].shape + (1, LANES)))
    c_out = d_s = None
    d_s_in = jnp.transpose(st["d_s"], (0, 2, 3, 4, 1)) if decode else None
    blocks = lambda t: t.reshape(nseq // DEC_NB, DEC_NB, -1)
    ffn_bf16 = {} if decode else prm["ffn_bf16"]

    def ffn(x, l, which):
        if decode:
            assert Lg == tm
            x, ffn_bf16[l, which] = _ffn_cast(x, mod, prm["ffn_w_gate"], prm["ffn_w_up"], prm["ffn_w_down"], l, which)
            return x
        return _ffn(x, mod, ffn_bf16[l, which], l, which, tm)

    for l in range(DEPTH):
        j = l // 2
        x = ffn(x, l, 0)
        if l % 2 == 0:
            z = _inproj(x, mod, prm["w_in_even"], l, j, tm_in, tn=1024)
            a_prm = prm["rglru"][j]
            conv0, h0 = st["a_conv"][j], st["a_h"][j]
            if decode:
                z2 = z[0]
                y1, h1 = _rglru_step(z2, conv0, h0, a_prm)
                conv1 = jnp.concatenate([conv0[:, 1:], z2[:, None, :W_A]], axis=1)
                yb, b_s = _hgrn2(blocks(z2), prm["lb"][j], prm["b_norm_w"][j], st["b_s"], b_s, j,
                                 C=DEC_NB, valid=1, nb=DEC_NB, hp=1, tt=DEC_NB, decode=True)
                y1, y2 = y1[None], yb.reshape(1, nseq, -1)
            else:
                buf8 = jnp.concatenate([jnp.zeros((nseq, 8 - (CONV_W - 1), W_A), F32), conv0], axis=1)
                a, b = _rglru_seq(z, buf8, a_prm, 512)
                y1, h1 = _rglru_scan(a, b, z, h0, 512)
                conv1 = z[:, Lg - (CONV_W - 1):, :W_A]
                y2, b_s = _hgrn2(z, prm["lb"][j], prm["b_norm_w"][j], st["b_s"], b_s, j,
                                 C=CHUNK, valid=CHUNK, nb=nseq, hp=2, tt=512, decode=False)
            new["a_conv"].append(conv1)
            new["a_h"].append(h1)
            x = _outproj(x, mod, y1, y2, prm["w_out_even"], l, j, tm)
        else:
            z = _inproj(x, mod, prm["w_in_odd"], l, j, tm_in, width=MLSTM_COLS, tn=MLSTM_COLS // 4, w_is_out_by_in=True)
            zd = _inproj(x, mod, prm["w_in_rwkv"], l, j, tm, tn=P_D // 2, w_is_out_by_in=True)
            d_prm = prm["rwkv"][j]
            if decode:
                yc, c_out = _mlstm(blocks(z[0]), prm["c_gate_b"][j], prm["c_norm_w"][j], c_in, c_out, j,
                                   C=DEC_NB, valid=1, nb=DEC_NB, hp=1, tt=DEC_NB, decode=True)
                feats_fm, gate, bonus = _rwkv_pre_step(zd[0], st["d_shift"][j], d_prm)
                y_fm, d_s = _rwkv_decode(feats_fm, d_s_in, d_s, j)
                yd = _rwkv_decode_out(y_fm, gate, bonus, prm["d_ln_w"][j], prm["d_ln_b"][j])
                y1, y2 = yc.reshape(1, nseq, -1), yd[None]
                shift1 = zd[0]
            else:
                y1, c_out = _mlstm(z, prm["c_gate_b"][j], prm["c_norm_w"][j], c_in, c_out, j,
                                   C=CHUNK, valid=CHUNK, nb=nseq, hp=2, tt=256, decode=False)
                prev8 = jnp.concatenate([jnp.zeros((nseq, 7, P_D), F32), st["d_shift"][j][:, None]], axis=1)
                feats = _rwkv_pre_seq(zd, prev8, d_prm, 128)
                p0 = _pair_blockdiag(jnp.swapaxes(st["d_s"][j], -1, -2))
                y2, p1 = _rwkv(feats, prm["d_ln_w"][j], prm["d_ln_b"][j], p0, C=CHUNK, valid=CHUNK, nb=nseq, hp=2,
                               tt=256)
                shift1 = zd[:, -1]
                new["d_s"].append(jnp.swapaxes(_pair_unblockdiag(p1), -1, -2))
            new["d_shift"].append(shift1)
            x = _outproj(x, mod, y1, y2, prm["w_out_odd"], l, j, tm)
        x = ffn(x, l, 1)
    y = _final_norm(x, prm["final_norm_w"], tm)
    d_s = jnp.transpose(d_s, (0, 4, 1, 2, 3)) if decode else jnp.stack(new["d_s"])
    outs = (y, jnp.stack(new["a_conv"]), jnp.stack(new["a_h"]), b_s, c_out[0], c_out[1][:, :, :, 0, :],
            c_out[2][:, :, :, 0, 0], jnp.stack(new["d_shift"]), d_s)
    return outs, ffn_bf16


def _prepare(w_mod, b_mod, ffn_w_gate, ffn_w_up, ffn_w_down, w_in_even, w_out_even, a_conv_w, a_conv_b, a_gate_r_w,
             a_gate_r_b, a_gate_i_w, a_gate_i_b, a_lambda, b_lb_gamma, b_norm_w, w_in_odd, w_out_odd, c_igate_b,
             c_fgate_b, c_norm_w, d_mu, d_w0, d_w2, d_a0, d_a2, d_g2, d_k_k, d_k_a, d_r_k, d_ln_w, d_ln_b,
             final_norm_w):
    n_gate = 2 * H_C
    w_odd_t = jnp.swapaxes(w_in_odd, 1, 2).astype(BF16)
    row = lambda t: t[:, None, :]
    zpad = jnp.zeros((N_ODD, LANES - n_gate), F32)
    half = jnp.zeros((N_ODD, R_W, W_D), F32)
    rglru = [(a_conv_w[j], a_conv_b[j][None], a_gate_r_w[j].astype(BF16), a_gate_r_b[j][None],
              a_gate_i_w[j].astype(BF16), a_gate_i_b[j][None], a_lambda[j][None]) for j in range(N_EVEN)]
    w2p = jnp.concatenate([d_w2, half], axis=1).astype(BF16)
    a2p = jnp.concatenate([half, d_a2], axis=1).astype(BF16)
    g2 = d_g2.astype(BF16)
    rwkv = [(d_mu[j][None], d_w0[j][None], d_a0[j][None], d_k_k[j][None], d_k_a[j][None],
             d_r_k[j].reshape(1, W_D), w2p[j], a2p[j], g2[j]) for j in range(N_ODD)]
    return dict(
        ffn_w_gate=ffn_w_gate, ffn_w_up=ffn_w_up, ffn_w_down=ffn_w_down,
        w_in_even=w_in_even.astype(BF16), w_in_odd=w_odd_t, w_in_rwkv=w_odd_t[:, P_ODD - P_D:],
        w_out_even=w_out_even.astype(BF16).reshape(N_EVEN, 2, D // 2, D),
        w_out_odd=w_out_odd.astype(BF16).reshape(N_ODD, 2, D // 2, D),
        rglru=rglru, rwkv=rwkv,
        lb=row(_lower_bounds(b_lb_gamma)), b_norm_w=row(b_norm_w),
        c_gate_b=row(jnp.concatenate([c_igate_b, c_fgate_b, zpad], axis=-1)), c_norm_w=row(c_norm_w),
        d_ln_w=row(d_ln_w), d_ln_b=row(d_ln_b), final_norm_w=final_norm_w)


def kernel(x_prompt, x_sample, c_prompt, c_sample, state_a_conv, state_a_h, state_b_s, state_c_c, state_c_n, state_c_m, state_d_shift, state_d_s, w_mod, b_mod, ffn_w_gate, ffn_w_up, ffn_w_down, w_in_even, w_out_even, a_conv_w, a_conv_b, a_gate_r_w, a_gate_r_b, a_gate_i_w, a_gate_i_b, a_lambda, b_lb_gamma, b_norm_w, w_in_odd, w_out_odd, c_igate_b, c_fgate_b, c_norm_w, d_mu, d_w0, d_w2, d_a0, d_a2, d_g2, d_k_k, d_k_a, d_r_k, d_ln_w, d_ln_b, final_norm_w):
    prm = _prepare(w_mod, b_mod, ffn_w_gate, ffn_w_up, ffn_w_down, w_in_even, w_out_even, a_conv_w, a_conv_b,
                   a_gate_r_w, a_gate_r_b, a_gate_i_w, a_gate_i_b, a_lambda, b_lb_gamma, b_norm_w, w_in_odd,
                   w_out_odd, c_igate_b, c_fgate_b, c_norm_w, d_mu, d_w0, d_w2, d_a0, d_a2, d_g2, d_k_k, d_k_a,
                   d_r_k, d_ln_w, d_ln_b, final_norm_w)
    bp, lp, _ = x_prompt.shape
    bs = x_sample.shape[0]
    n_rows = -(-(bs + bp) // 8) * 8
    c_all = jnp.concatenate([c_sample, c_prompt, jnp.zeros((n_rows - bs - bp, D), F32)], axis=0)
    mod_all = _modulation(c_all, w_mod, b_mod)
    mod_s = mod_all[:, None]
    mod_p = mod_all[:, bs:bs + bp, None]

    zeros = lambda *s: jnp.zeros(s, F32)
    st_p = dict(a_conv=zeros(N_EVEN, bp, CONV_W - 1, W_A), a_h=zeros(N_EVEN, bp, W_A),
                b_s=zeros(N_EVEN, bp, H_B, DK_B, DV_B), c_c=zeros(N_ODD, bp, H_C, DK_C, DV_C),
                c_n=zeros(N_ODD, bp, H_C, DK_C), c_m=zeros(N_ODD, bp, H_C), d_shift=zeros(N_ODD, bp, P_D),
                d_s=zeros(N_ODD, bp, H_D, N_D, N_D))
    st_s = dict(a_conv=state_a_conv, a_h=state_a_h, b_s=state_b_s, c_c=state_c_c, c_n=state_c_n, c_m=state_c_m,
                d_shift=state_d_shift, d_s=state_d_s)
    out_s, ffn_bf16 = _trunk(x_sample.reshape(1, bs, D), mod_s, st_s, prm, decode=True)
    out_p, _ = _trunk(x_prompt, mod_p, st_p, dict(prm, ffn_bf16=ffn_bf16), decode=False)
    y_s = out_s[0].reshape(bs, 1, D)
    return (out_p[0], y_s) + out_p[1:] + out_s[1:]
```

```python
import functools

import jax
import jax.numpy as jnp
from jax import lax
from jax.experimental import pallas as pl
from jax.experimental.pallas import tpu as pltpu

F32 = jnp.float32
BF16 = jnp.bfloat16

D = 2048
DEPTH = 4
N_EVEN = 2
N_ODD = 2
D_FF = 5632
N_MOD = 9
EPS = 1e-6
CHUNK = 64
NEG_BIG = -1e30

W_A = 1024
NB_A = 8
BS_A = 128
CONV_W = 4
C_RGLRU = 8.0
H_B = 8
DK_B = 128
DV_B = 128
H_C = 4
DK_C = 128
DV_C = 256
N_D = 64
H_D = 16
W_D = 1024
R_W = 64
R_A = 64
R_G = 128
GN_EPS_D = 64e-5
P_EVEN = 6144
P_D = 3328
P_ODD = 6408
MLSTM_COLS = 3584
DEC_NB = 8
LANES = 128
VMEM_LIMIT = 48 * 2**20


def _cp(sem, vmem=VMEM_LIMIT):
    return pltpu.CompilerParams(dimension_semantics=sem, vmem_limit_bytes=vmem)


def _mm(a, b):
    return jnp.dot(a, b, preferred_element_type=F32)


def _dot(a, b):
    return _mm(a.astype(BF16), b.astype(BF16))


_NT = (((1,), (1,)), ((), ()))
_TN = (((0,), (0,)), ((), ()))
_NN = (((1,), (0,)), ((), ()))


def _dg(a, b, dn):
    return lax.dot_general(a, b, dn, preferred_element_type=F32)


def _dot_nt(a, b):
    return _dg(a.astype(BF16), b.astype(BF16), _NT)


def _dot_tn(a, b):
    return _dg(a.astype(BF16), b.astype(BF16), _TN)


def _split3(x):
    hi = x.astype(BF16)
    r1 = x - hi.astype(F32)
    mid = r1.astype(BF16)
    lo = (r1 - mid.astype(F32)).astype(BF16)
    return hi, mid, lo


def _split2(x):
    hi = x.astype(BF16)
    lo = (x - hi.astype(F32)).astype(BF16)
    return hi, lo


def _pieces(x, n):
    return _split3(x) if n == 3 else _split2(x)


def _sel_dot(mask_bf, x, n=3):
    return functools.reduce(jnp.add, [_mm(mask_bf, p) for p in _pieces(x, n)])


def _dot_sel(x, mask_bf, n=3):
    return functools.reduce(jnp.add, [_mm(p, mask_bf) for p in _pieces(x, n)])


def _dot_sel_tn(x, mask_bf, n=3):
    return functools.reduce(jnp.add, [_dg(p, mask_bf, _TN) for p in _pieces(x, n)])


def _mmx(a, b, dn=_NN):
    a_hi, a_lo = _split2(a)
    b_hi, b_lo = _split2(b)
    return _dg(a_hi, b_hi, dn) + _dg(a_hi, b_lo, dn) + _dg(a_lo, b_hi, dn)


def _iota(shape, dim):
    return lax.broadcasted_iota(jnp.int32, shape, dim)


def _sigmoid(x):
    return jax.nn.sigmoid(x)


def _softplus(x):
    return jnp.maximum(x, 0.0) + jnp.log1p(jnp.exp(-jnp.abs(x)))


def _log_sigmoid(x):
    return jnp.minimum(x, 0.0) - jnp.log1p(jnp.exp(-jnp.abs(x)))


def _gelu_tanh(x):
    return x * (0.5 * (1.0 + jnp.tanh(0.7978845608028654 * (x + 0.044715 * (x * x * x)))))


def _rms(x):
    return x * lax.rsqrt(jnp.mean(x * x, axis=-1, keepdims=True) + EPS)


def _adaln(x, shift, scale):
    return _rms(x) * (1.0 + scale) + shift


def _log2(n):
    k = n.bit_length() - 1
    assert (1 << k) == n
    return k


def _mod_spec(rows, width, layer, col):
    return pl.BlockSpec((None, None, rows, width), lambda g, i, n: (layer, g, 0, col))


def _mod_body(c_ref, w_ref, b_ref, o_ref):
    c = c_ref[...]
    cs = (c * _sigmoid(c)).astype(BF16)
    o_ref[...] = _mm(cs, w_ref[...].astype(BF16)) + b_ref[...]


def _modulation(c_all, w_mod, b_mod):
    rows = c_all.shape[0]
    tn = 1024
    return pl.pallas_call(
        _mod_body,
        grid=(DEPTH, N_MOD * D // tn),
        in_specs=[pl.BlockSpec((rows, D), lambda l, n: (0, 0)),
                  pl.BlockSpec((None, D, tn), lambda l, n: (l, 0, n)),
                  pl.BlockSpec((None, 1, tn), lambda l, n: (l, 0, n))],
        out_specs=pl.BlockSpec((None, rows, tn), lambda l, n: (l, 0, n)),
        out_shape=jax.ShapeDtypeStruct((DEPTH, rows, N_MOD * D), F32),
        compiler_params=_cp(("arbitrary", "arbitrary")),
        name="modulation",
    )(c_all, w_mod, b_mod.reshape(DEPTH, 1, N_MOD * D))


def _lower_bounds_body(g_ref, o_ref):
    g = g_ref[...]
    e = jnp.exp(g - jnp.max(g, axis=0, keepdims=True))
    sm = e / jnp.sum(e, axis=0, keepdims=True)
    acc = jnp.zeros_like(sm[0:1])
    for j in range(N_EVEN):
        acc = acc + sm[j:j + 1]
        o_ref[j:j + 1, :] = acc - sm[0:1]


def _lower_bounds(gamma):
    return pl.pallas_call(
        _lower_bounds_body,
        out_shape=jax.ShapeDtypeStruct(gamma.shape, F32),
        name="hgrn2_lower_bounds",
    )(gamma)


def _ffn_step(x_ref, sh_ref, sc_ref, gt_ref, wg_ref, wu_ref, wd_ref, o_ref, hf_ref, acc_ref):
    f = pl.program_id(2)

    @pl.when(f == 0)
    def _():
        hf_ref[...] = _adaln(x_ref[...], sh_ref[...], sc_ref[...]).astype(BF16)
        acc_ref[...] = jnp.zeros_like(acc_ref)

    hf = hf_ref[...]
    g = _mm(hf, wg_ref[...])
    u = _mm(hf, wu_ref[...])
    act = (g * _sigmoid(g) * u).astype(BF16)
    acc_ref[...] += _mm(act, wd_ref[...])

    @pl.when(f == pl.num_programs(2) - 1)
    def _():
        o_ref[...] = x_ref[...] + 0.5 * (1.0 + gt_ref[...]) * acc_ref[...]


def _ffn_body(x_ref, sh_ref, sc_ref, gt_ref, wg_ref, wu_ref, wd_ref, o_ref, hf_ref, acc_ref):
    _ffn_step(x_ref, sh_ref, sc_ref, gt_ref, wg_ref, wu_ref, wd_ref, o_ref, hf_ref, acc_ref)


def _ffn_cast_body(x_ref, sh_ref, sc_ref, gt_ref, wg_ref, wu_ref, wd_ref, o_ref, wgo_ref, wuo_ref, wdo_ref,
                   hf_ref, acc_ref):
    wgo_ref[...] = wg_ref[...].astype(BF16)
    wuo_ref[...] = wu_ref[...].astype(BF16)
    wdo_ref[...] = wd_ref[...].astype(BF16)
    _ffn_step(x_ref, sh_ref, sc_ref, gt_ref, wgo_ref, wuo_ref, wdo_ref, o_ref, hf_ref, acc_ref)


FFN_TF = 512


def _ffn_specs(x, mod, layer, which, tm):
    R = 1 if mod.shape[2] == 1 else tm
    kb = 6 * which
    return [pl.BlockSpec((None, tm, D), lambda g, i, f: (g, i, 0)),
            _mod_spec(R, D, layer, kb), _mod_spec(R, D, layer, kb + 1), _mod_spec(R, D, layer, kb + 2)]


def _ffn(x, mod, w_bf16, layer, which, tm):
    G, Lg, _ = x.shape
    tf = FFN_TF
    return pl.pallas_call(
        _ffn_body,
        grid=(G, Lg // tm, D_FF // tf),
        in_specs=_ffn_specs(x, mod, layer, which, tm) + [
            pl.BlockSpec((D, tf), lambda g, i, f: (0, f)), pl.BlockSpec((D, tf), lambda g, i, f: (0, f)),
            pl.BlockSpec((tf, D), lambda g, i, f: (f, 0))],
        out_specs=pl.BlockSpec((None, tm, D), lambda g, i, f: (g, i, 0)),
        out_shape=jax.ShapeDtypeStruct(x.shape, F32),
        scratch_shapes=[pltpu.VMEM((tm, D), BF16), pltpu.VMEM((tm, D), F32)],
        compiler_params=_cp(("arbitrary", "arbitrary", "arbitrary")),
        name="adaln_swiglu_ffn",
    )(x, mod, mod, mod, *w_bf16)


def _ffn_cast(x, mod, wg, wu, wd, layer, which):
    G, tm, _ = x.shape
    assert G == 1
    tf = FFN_TF // 2
    bf = lambda s: jax.ShapeDtypeStruct(s, BF16)
    outs = pl.pallas_call(
        _ffn_cast_body,
        grid=(1, 1, D_FF // tf),
        in_specs=_ffn_specs(x, mod, layer, which, tm) + [
            pl.BlockSpec((None, None, D, tf), lambda g, i, f: (layer, which, 0, f)),
            pl.BlockSpec((None, None, D, tf), lambda g, i, f: (layer, which, 0, f)),
            pl.BlockSpec((None, None, tf, D), lambda g, i, f: (layer, which, f, 0))],
        out_specs=[pl.BlockSpec((None, tm, D), lambda g, i, f: (g, i, 0)),
                   pl.BlockSpec((D, tf), lambda g, i, f: (0, f)), pl.BlockSpec((D, tf), lambda g, i, f: (0, f)),
                   pl.BlockSpec((tf, D), lambda g, i, f: (f, 0))],
        out_shape=[jax.ShapeDtypeStruct(x.shape, F32), bf((D, D_FF)), bf((D, D_FF)), bf((D_FF, D))],
        scratch_shapes=[pltpu.VMEM((tm, D), BF16), pltpu.VMEM((tm, D), F32)],
        compiler_params=_cp(("arbitrary", "arbitrary", "arbitrary")),
        name="adaln_swiglu_ffn_weight_cast",
    )(x, mod, mod, mod, wg, wu, wd)
    return outs[0], tuple(outs[1:])


def _inproj_body(x_ref, sh_ref, sc_ref, w_ref, o_ref, hf_ref, *, w_is_out_by_in):
    @pl.when(pl.program_id(2) == 0)
    def _():
        hf_ref[...] = _adaln(x_ref[...], sh_ref[...], sc_ref[...]).astype(BF16)

    o_ref[...] = _dg(hf_ref[...], w_ref[...], _NT if w_is_out_by_in else _NN)


def _inproj(x, mod, w, layer, j, tm, width=None, tn=512, w_is_out_by_in=False):
    G, Lg, _ = x.shape
    R = 1 if mod.shape[2] == 1 else tm
    n_out = w.shape[1] if w_is_out_by_in else w.shape[2]
    P = n_out if width is None else width
    assert P % tn == 0 and P <= n_out
    w_spec = (pl.BlockSpec((None, tn, D), lambda g, i, n: (j, n, 0)) if w_is_out_by_in
              else pl.BlockSpec((None, D, tn), lambda g, i, n: (j, 0, n)))
    return pl.pallas_call(
        functools.partial(_inproj_body, w_is_out_by_in=w_is_out_by_in),
        grid=(G, Lg // tm, P // tn),
        in_specs=[pl.BlockSpec((None, tm, D), lambda g, i, n: (g, i, 0)),
                  _mod_spec(R, D, layer, 3), _mod_spec(R, D, layer, 4), w_spec],
        out_specs=pl.BlockSpec((None, tm, tn), lambda g, i, n: (g, i, n)),
        out_shape=jax.ShapeDtypeStruct((G, Lg, P), F32),
        scratch_shapes=[pltpu.VMEM((tm, D), BF16)],
        compiler_params=_cp(("arbitrary", "arbitrary", "arbitrary")),
        name="adaln_in_projection",
    )(x, mod, mod, w)


def _outproj_body(x_ref, gt_ref, y1_ref, y2_ref, w1_ref, w2_ref, o_ref):
    y = _dot(y1_ref[...], w1_ref[...]) + _dot(y2_ref[...], w2_ref[...])
    o_ref[...] = x_ref[...] + (1.0 + gt_ref[...]) * y


def _outproj(x, mod, y1, y2, w, layer, j, tm):
    G, Lg, _ = x.shape
    R = 1 if mod.shape[2] == 1 else tm
    tn = D
    nb = D // tn
    half = y1.shape[-1]
    return pl.pallas_call(
        _outproj_body,
        grid=(G, Lg // tm, nb),
        in_specs=[pl.BlockSpec((None, tm, tn), lambda g, i, n: (g, i, n)),
                  pl.BlockSpec((None, None, R, tn), lambda g, i, n: (layer, g, 0, 5 * nb + n)),
                  pl.BlockSpec((None, tm, half), lambda g, i, n: (g, i, 0)),
                  pl.BlockSpec((None, tm, half), lambda g, i, n: (g, i, 0)),
                  pl.BlockSpec((None, None, half, tn), lambda g, i, n: (j, 0, 0, n)),
                  pl.BlockSpec((None, None, half, tn), lambda g, i, n: (j, 1, 0, n))],
        out_specs=pl.BlockSpec((None, tm, tn), lambda g, i, n: (g, i, n)),
        out_shape=jax.ShapeDtypeStruct(x.shape, F32),
        compiler_params=_cp(("arbitrary", "arbitrary", "arbitrary")),
        name="out_projection_residual",
    )(x, mod, y1, y2, w, w)


def _final_body(x_ref, w_ref, o_ref):
    o_ref[...] = _rms(x_ref[...]) * w_ref[...]


def _final_norm(x, w, tm):
    G, Lg, _ = x.shape
    return pl.pallas_call(
        _final_body,
        grid=(G, Lg // tm),
        in_specs=[pl.BlockSpec((None, tm, D), lambda g, i: (g, i, 0)),
                  pl.BlockSpec((1, D), lambda g, i: (0, 0))],
        out_specs=pl.BlockSpec((None, tm, D), lambda g, i: (g, i, 0)),
        out_shape=jax.ShapeDtypeStruct(x.shape, F32),
        compiler_params=_cp(("arbitrary", "arbitrary")),
        name="final_rmsnorm",
    )(x, w.reshape(1, D))


def _rglru_gates(x0, x1, x2, x3, cw, cb, wr, br, wi, bi, lam):
    u = cb + x0 * cw[0:1] + x1 * cw[1:2] + x2 * cw[2:3] + x3 * cw[3:4]
    r_parts, i_parts = [], []
    for n in range(NB_A):
        un = u[:, n * BS_A:(n + 1) * BS_A].astype(BF16)
        r_parts.append(_mm(un, wr[n]))
        i_parts.append(_mm(un, wi[n]))
    r = _sigmoid(jnp.concatenate(r_parts, axis=1) + br)
    ig = _sigmoid(jnp.concatenate(i_parts, axis=1) + bi)
    log_a = -C_RGLRU * r * _softplus(-lam)
    a = jnp.exp(log_a)
    b = jnp.sqrt(-jnp.tanh(log_a) * (a * a + 1.0)) * (ig * u)
    return a, b


def _rglru_seq_body(x_ref, halo_ref, buf_ref, cw_ref, cb_ref, wr_ref, br_ref, wi_ref, bi_ref, lam_ref,
                    a_ref, b_ref, *, tt):
    i = pl.program_id(1)
    halo = jnp.where(i == 0, buf_ref[...], halo_ref[...])
    full = jnp.concatenate([halo, x_ref[...]], axis=0)
    taps = [pltpu.roll(full, CONV_W - 1 - k, axis=0)[8:8 + tt] for k in range(CONV_W - 1)]
    a, b = _rglru_gates(taps[0], taps[1], taps[2], x_ref[...], cw_ref[...], cb_ref[...], wr_ref[...], br_ref[...],
                        wi_ref[...], bi_ref[...], lam_ref[...])
    a_ref[...] = a
    b_ref[...] = b


def _rglru_param_specs(nidx):
    z2 = lambda *_: (0, 0)
    z3 = lambda *_: (0, 0, 0)
    del nidx
    return [pl.BlockSpec((CONV_W, W_A), z2), pl.BlockSpec((1, W_A), z2),
            pl.BlockSpec((NB_A, BS_A, BS_A), z3), pl.BlockSpec((1, W_A), z2),
            pl.BlockSpec((NB_A, BS_A, BS_A), z3), pl.BlockSpec((1, W_A), z2),
            pl.BlockSpec((1, W_A), z2)]


def _rglru_seq(z, buf8, prm, tt):
    B, L, _ = z.shape
    hb = tt // 8
    out = jax.ShapeDtypeStruct((B, L, W_A), F32)
    return pl.pallas_call(
        functools.partial(_rglru_seq_body, tt=tt),
        grid=(B, L // tt),
        in_specs=[pl.BlockSpec((None, tt, W_A), lambda b, i: (b, i, 0)),
                  pl.BlockSpec((None, 8, W_A), lambda b, i: (b, jnp.maximum(i * hb - 1, 0), 0)),
                  pl.BlockSpec((None, 8, W_A), lambda b, i: (b, 0, 0))] + _rglru_param_specs(2),
        out_specs=[pl.BlockSpec((None, tt, W_A), lambda b, i: (b, i, 0))] * 2,
        out_shape=[out, out],
        compiler_params=_cp(("arbitrary", "arbitrary")),
        name="rglru_conv_gates",
    )(z, z, buf8, *prm)


def _rglru_scan_body(a_ref, b_ref, ag_ref, h0_ref, y_ref, hl_ref, h_scr, *, tt):
    @pl.when(pl.program_id(1) == 0)
    def _():
        h_scr[...] = h0_ref[...]

    def step(t, h):
        h = a_ref[t] * h + b_ref[t]
        y_ref[t] = h
        return h

    h = lax.fori_loop(0, tt, step, h_scr[...], unroll=8)
    h_scr[...] = h
    hl_ref[...] = h
    y_ref[...] = y_ref[...] * _gelu_tanh(ag_ref[...])


def _rglru_scan(a, b, z, h0, tt):
    B, L, _ = a.shape
    a4 = a.reshape(B, L, 8, LANES)
    b4 = b.reshape(B, L, 8, LANES)
    z4 = z.reshape(B, L, z.shape[-1] // LANES, LANES)
    spec = pl.BlockSpec((None, tt, 8, LANES), lambda bb, i: (bb, i, 0, 0))
    y, hl = pl.pallas_call(
        functools.partial(_rglru_scan_body, tt=tt),
        grid=(B, L // tt),
        in_specs=[spec, spec,
                  pl.BlockSpec((None, tt, 8, LANES), lambda bb, i: (bb, i, 1, 0)),
                  pl.BlockSpec((None, 8, LANES), lambda bb, i: (bb, 0, 0))],
        out_specs=[spec, pl.BlockSpec((None, 8, LANES), lambda bb, i: (bb, 0, 0))],
        out_shape=[jax.ShapeDtypeStruct((B, L, 8, LANES), F32), jax.ShapeDtypeStruct((B, 8, LANES), F32)],
        scratch_shapes=[pltpu.VMEM((8, LANES), F32)],
        compiler_params=_cp(("arbitrary", "arbitrary")),
        name="rglru_scan_gelu_gate",
    )(a4, b4, z4, h0.reshape(B, 8, LANES))
    return y.reshape(B, L, W_A), hl.reshape(B, W_A)


def _rglru_step_body(x3_ref, ag_ref, x0_ref, x1_ref, x2_ref, h0_ref, cw_ref, cb_ref, wr_ref, br_ref, wi_ref, bi_ref,
                     lam_ref, y_ref, h_ref):
    a, b = _rglru_gates(x0_ref[...], x1_ref[...], x2_ref[...], x3_ref[...], cw_ref[...], cb_ref[...], wr_ref[...],
                        br_ref[...], wi_ref[...], bi_ref[...], lam_ref[...])
    h = a * h0_ref[...] + b
    h_ref[...] = h
    y_ref[...] = h * _gelu_tanh(ag_ref[...])


def _rglru_step(z, buf, h0, prm):
    n = z.shape[0]
    row = lambda c: pl.BlockSpec((n, W_A), lambda i: (0, c))
    out = jax.ShapeDtypeStruct((n, W_A), F32)
    return pl.pallas_call(
        _rglru_step_body,
        grid=(1,),
        in_specs=[row(0), row(1), row(0), row(0), row(0), row(0)] + _rglru_param_specs(1),
        out_specs=[row(0), row(0)],
        out_shape=[out, out],
        compiler_params=_cp(("arbitrary",)),
        name="rglru_decode_step",
    )(z, z, buf[:, 0], buf[:, 1], buf[:, 2], h0, *prm)


def _live_rows(C, first, valid):
    row = _iota((C, 1), 0)
    return (row >= first) & (row < first + valid)


def _hgrn2_chunk(q, fp, v, gate, lb, nw, S, *, C, valid, first=0):
    SB = min(8, C)
    k = (1.0 - lb) * _sigmoid(-fp)
    lf = jnp.log1p(-k)
    if valid < C:
        live = _live_rows(C, first, valid)
        k = jnp.where(live, k, 0.0)
        lf = jnp.where(live, lf, 0.0)
    g_col = _dot_sel_tn(lf, jnp.ones((C, DV_B), BF16))
    if valid == 1:
        kv = _dot_tn(k, v)
        yield
        S_new = jnp.exp(g_col) * S + kv
        o = _dot(q, S_new)
        yield
        return _rms(o) * nw * (gate * _sigmoid(gate)), S_new
    tril = _iota((C, C), 0) >= _iota((C, C), 1)
    b = _sel_dot(jnp.where(tril, 1.0, 0.0).astype(BF16), lf)
    yield
    g_row = b[C - 1:C, :]
    o = _dot(q * jnp.exp(b), S)
    khat = k * jnp.exp(g_row - b)
    S_new = jnp.exp(g_col) * S + _dot_tn(khat, v)

    tril_sb = _iota((SB, SB), 0) >= _iota((SB, SB), 1)
    atts = []
    for i in range(C // SB):
        lo = i * SB
        qi, bi, ki = q[lo:lo + SB], b[lo:lo + SB], k[lo:lo + SB]
        ci = bi - jnp.log(ki)
        att = jnp.where(tril_sb, jnp.sum(qi[:, None, :] * jnp.exp(bi[:, None, :] - ci[None, :, :]), axis=-1), 0.0)
        off = None
        if i > 0:
            ref = b[lo - 1:lo, :]
            qt = qi * jnp.exp(bi - ref)
            kt = k[0:lo] * jnp.exp(ref - b[0:lo])
            off = _dot_nt(qt, kt)
        atts.append((att, off))
    yield
    parts = []
    for i, (att, off) in enumerate(atts):
        lo = i * SB
        oi = _dot(att, v[lo:lo + SB])
        parts.append(oi if off is None else oi + _dot(off, v[0:lo]))
    yield
    o = o + (parts[0] if len(parts) == 1 else jnp.concatenate(parts, axis=0))
    return _rms(o) * nw * (gate * _sigmoid(gate)), S_new


def _merge_rows(outs, C):
    row = _iota((C, 1), 0)
    y = jnp.where(row == 0, outs[0], 0.0)
    for s in range(1, len(outs)):
        y = jnp.where(row == s, outs[s], y)
    return y


def _chained(prev, out_index, operands, specs):
    if prev is None:
        return {}
    operands.append(prev)
    specs.append(pl.BlockSpec(memory_space=pl.ANY))
    return {len(operands) - 1: out_index}


def _hgrn2_body(q_ref, f_ref, v_ref, g_ref, lb_ref, nw_ref, s0_ref, *rest, C, valid, nb, hp, decode, nchunk):
    y_ref, s1_ref = rest[-2:]

    @pl.when(pl.program_id(2) == 0)
    def _():
        s1_ref[...] = s0_ref[...]

    nw = nw_ref[...]
    probs = [(bi, hh) for bi in range(nb) for hh in range(hp)]

    def chunk(c, carry):
        rows = pl.ds(pl.multiple_of(c * C, C), C)
        gens = []
        for bi, hh in probs:
            lanes = slice(hh * LANES, (hh + 1) * LANES)
            src = 0 if decode else bi
            gens.append(_hgrn2_chunk(q_ref[src, rows, lanes], f_ref[src, rows, lanes], v_ref[src, rows, lanes],
                                     g_ref[src, rows, lanes], lb_ref[:, lanes], nw, s1_ref[bi, hh],
                                     C=C, valid=valid, first=bi if decode else 0))
        outs = _interleave(gens)
        for (bi, hh), (y, s_new) in zip(probs, outs):
            s1_ref[bi, hh] = s_new
            if not decode:
                y_ref[bi, rows, hh * LANES:(hh + 1) * LANES] = y
        if decode:
            for hh in range(hp):
                y_ref[0, rows, hh * LANES:(hh + 1) * LANES] = _merge_rows([outs[bi * hp + hh][0] for bi in range(nb)], C)
        return carry

    lax.fori_loop(0, nchunk, chunk, 0)


def _hgrn2(z, lb, nw, s_all, s_prev, j, *, C, valid, nb, hp, tt, decode):
    B, L, _ = z.shape
    zb = 1 if decode else nb
    w = hp * LANES
    col = lambda base: pl.BlockSpec((zb, tt, w), lambda b, h, i: (b, i, base // hp + h))
    st = pl.BlockSpec((None, nb, hp, DK_B, DV_B), lambda b, h, i: (j, b, h, 0, 0))
    operands = [z, z, z, z, lb, nw, s_all]
    specs = [col(16), col(24), col(32), col(40), pl.BlockSpec((1, w), lambda b, h, i: (0, h)),
             pl.BlockSpec((1, LANES), lambda b, h, i: (0, 0)), st]
    aliases = _chained(s_prev, 1, operands, specs)
    return pl.pallas_call(
        functools.partial(_hgrn2_body, C=C, valid=valid, nb=nb, hp=hp, decode=decode, nchunk=tt // C),
        grid=(B // zb, H_B // hp, L // tt),
        in_specs=specs,
        out_specs=[col(0), st],
        out_shape=[jax.ShapeDtypeStruct((B, L, H_B * DV_B), F32), jax.ShapeDtypeStruct(s_all.shape, F32)],
        input_output_aliases=aliases,
        compiler_params=_cp(("arbitrary", "arbitrary", "arbitrary")),
        name="hgrn2_chunked",
    )(*operands)


def _mlstm_step(q, k, v, op, gt, h, nw, Cst, n, m, *, C, first):
    lane = _iota((C, LANES), 1)
    live = _live_rows(C, first, 1)
    i0 = jnp.sum(jnp.where(live & (lane == h), gt, 0.0), keepdims=True)
    lf0 = _log_sigmoid(jnp.sum(jnp.where(live & (lane == h + H_C), gt, 0.0), keepdims=True))
    inter = lf0 + m
    m_t = jnp.maximum(inter, i0)
    w_inter = jnp.exp(inter - m_t)
    wk = jnp.exp(i0 - m_t)
    kh = k * (DK_C ** -0.5)
    s = jnp.sum(q * kh, axis=1, keepdims=True) * wk
    qc = _dot(q, Cst)
    kw = jnp.where(live, kh * wk, 0.0)
    kv = _dot_tn(kw, v)
    yield
    num = w_inter * qc + s * v
    den = w_inter * jnp.sum(q * n, axis=1, keepdims=True) + s
    hh = num / jnp.maximum(jnp.abs(den), jnp.exp(-m_t))
    C_new = w_inter * Cst + kv
    n_new = w_inter * n + jnp.sum(kw, axis=0, keepdims=True)
    y = _rms(hh) * nw * _sigmoid(op)
    return y, C_new, n_new, m_t


def _mlstm_chunk(q, k, v, op, gt, h, nw, Cst, n, m, *, C, valid, first=0):
    if valid == 1:
        return (yield from _mlstm_step(q, k, v, op, gt, h, nw, Cst, n, m, C=C, first=first))
    assert first == 0
    gtT = gt.T
    lane = _iota((C, LANES), 1)
    sub = _iota((LANES, C), 0)
    i_col = jnp.sum(jnp.where(lane == h, gt, 0.0), axis=1, keepdims=True)
    f_col = jnp.sum(jnp.where(lane == h + H_C, gt, 0.0), axis=1, keepdims=True)
    i_row = jnp.sum(jnp.where(sub == h, gtT, 0.0), axis=0, keepdims=True)
    f_row = jnp.sum(jnp.where(sub == h + H_C, gtT, 0.0), axis=0, keepdims=True)
    lf_col = _log_sigmoid(f_col)
    lf_row = _log_sigmoid(f_row)
    if valid < C:
        live_c = _iota((C, 1), 0) < valid
        live_r = _iota((1, C), 1) < valid
        i_col = jnp.where(live_c, i_col, NEG_BIG)
        i_row = jnp.where(live_r, i_row, NEG_BIG)
        lf_col = jnp.where(live_c, lf_col, 0.0)
        lf_row = jnp.where(live_r, lf_row, 0.0)
    tril = _iota((C, C), 0) >= _iota((C, C), 1)
    b_col = jnp.sum(jnp.where(tril, lf_row, 0.0), axis=1, keepdims=True)
    b_row = jnp.sum(jnp.where(_iota((C, C), 0) <= _iota((C, C), 1), lf_col, 0.0), axis=0, keepdims=True)
    dm = jnp.where(tril, b_col - b_row + i_row, NEG_BIG)
    inter = b_col + m
    m_t = jnp.maximum(inter, jnp.max(dm, axis=1, keepdims=True))
    w_inter = jnp.exp(inter - m_t)
    kh = k * (DK_C ** -0.5)
    qk = _dot_nt(q, kh)
    qc = _dot(q, Cst)
    g = b_col[C - 1:C, :]
    m_new = m_t[C - 1:C, :]
    wk = jnp.exp(g - b_col + i_col - m_new)
    f_state = jnp.exp(g + m - m_new)
    kw = kh * wk
    kv = _dot_tn(kw, v)
    yield
    s = qk * jnp.exp(dm - m_t)
    sv = _dot(s, v)
    yield
    num = w_inter * qc + sv
    den = w_inter * jnp.sum(q * n, axis=1, keepdims=True) + jnp.sum(s, axis=1, keepdims=True)
    hh = num / jnp.maximum(jnp.abs(den), jnp.exp(-m_t))
    C_new = f_state * Cst + kv
    n_new = f_state * n + jnp.sum(kw, axis=0, keepdims=True)
    y = _rms(hh) * nw * _sigmoid(op)
    return y, C_new, n_new, m_new


def _mlstm_body(q_ref, k_ref, v_ref, o_ref, g_ref, gb_ref, nw_ref, c0_ref, n0_ref, m0_ref, *rest,
                C, valid, nb, hp, decode, nchunk):
    y_ref, c1_ref, n1_ref, m1_ref = rest[-4:]

    @pl.when(pl.program_id(2) == 0)
    def _():
        c1_ref[...] = c0_ref[...]
        n1_ref[...] = n0_ref[...]
        m1_ref[...] = m0_ref[...]

    gb = gb_ref[...]
    nw = nw_ref[...]
    probs = [(bi, hh) for bi in range(nb) for hh in range(hp)]

    def chunk(c, carry):
        rows = pl.ds(pl.multiple_of(c * C, C), C)
        gens = []
        for bi, hh in probs:
            src = 0 if decode else bi
            kl = slice(hh * DK_C, (hh + 1) * DK_C)
            vl = slice(hh * DV_C, (hh + 1) * DV_C)
            gens.append(_mlstm_chunk(
                q_ref[src, rows, kl], k_ref[src, rows, kl], v_ref[src, rows, vl], o_ref[src, rows, vl],
                g_ref[src, rows, :] + gb, pl.program_id(1) * hp + hh, nw, c1_ref[bi, hh], n1_ref[bi, hh],
                m1_ref[bi, hh][:, 0:1], C=C, valid=valid, first=bi if decode else 0))
        outs = _interleave(gens)
        for (bi, hh), (y, c_new, n_new, m_new) in zip(probs, outs):
            if not decode:
                y_ref[bi, rows, hh * DV_C:(hh + 1) * DV_C] = y
            c1_ref[bi, hh] = c_new
            n1_ref[bi, hh] = n_new
            m1_ref[bi, hh] = jnp.broadcast_to(m_new, (1, LANES))
        if decode:
            for hh in range(hp):
                y_ref[0, rows, hh * DV_C:(hh + 1) * DV_C] = _merge_rows([outs[bi * hp + hh][0] for bi in range(nb)], C)
        return carry

    lax.fori_loop(0, nchunk, chunk, 0)


def _mlstm(z, gb, nw, states, prev, j, *, C, valid, nb, hp, tt, decode):
    B, L, _ = z.shape
    zb = 1 if decode else nb
    c128 = lambda base: pl.BlockSpec((zb, tt, hp * DK_C), lambda b, h, i: (b, i, base // hp + h))
    c256 = lambda base: pl.BlockSpec((zb, tt, hp * DV_C), lambda b, h, i: (b, i, base // hp + h))
    cst = pl.BlockSpec((None, nb, hp, DK_C, DV_C), lambda b, h, i: (j, b, h, 0, 0))
    vec = pl.BlockSpec((None, nb, hp, 1, LANES), lambda b, h, i: (j, b, h, 0, 0))
    operands = [z, z, z, z, z, gb, nw, *states]
    specs = [c128(0), c128(4), c256(4), c256(8), pl.BlockSpec((zb, tt, LANES), lambda b, h, i: (b, i, 24)),
             pl.BlockSpec((1, LANES), lambda b, h, i: (0, 0)), pl.BlockSpec((1, DV_C), lambda b, h, i: (0, 0)),
             cst, vec, vec]
    aliases = {}
    for k, p in enumerate(prev or ()):
        aliases.update(_chained(p, 1 + k, operands, specs))
    outs = pl.pallas_call(
        functools.partial(_mlstm_body, C=C, valid=valid, nb=nb, hp=hp, decode=decode, nchunk=tt // C),
        grid=(B // zb, H_C // hp, L // tt),
        in_specs=specs,
        out_specs=[c256(0), cst, vec, vec],
        out_shape=[jax.ShapeDtypeStruct((B, L, H_C * DV_C), F32)] + [jax.ShapeDtypeStruct(s.shape, F32) for s in states],
        input_output_aliases=aliases,
        compiler_params=_cp(("arbitrary", "arbitrary", "arbitrary")),
        name="mlstm_chunked",
    )(*operands)
    return outs[0], tuple(outs[1:])


def _head_sum(x, bd):
    return jnp.concatenate([_dot_sel(x[:, n * LANES:(n + 1) * LANES], bd) for n in range(W_D // LANES)], axis=1)


def _rwkv_pre(zd, sh, mu, w0, a0, kkw, ka, rk, w2p, a2p, g2, bd):
    zs = zd + (sh - zd) * mu
    r, k, v = zs[:, 0:W_D], zs[:, W_D:2 * W_D], zs[:, 2 * W_D:3 * W_D]
    wa = zs[:, 3 * W_D:3 * W_D + LANES]
    gl = zs[:, 3 * W_D + LANES:]
    w = -_softplus(-(w0 + _dot(jnp.tanh(wa), w2p))) - 0.5
    logd = -jnp.exp(w)
    a = _sigmoid(a0 + _dot(wa, a2p))
    gate = _dot(_sigmoid(gl), g2)
    kk = k * kkw
    kk = kk / jnp.maximum(jnp.sqrt(_head_sum(kk * kk, bd)), 1e-12)
    kmod = k * (1.0 + (a - 1.0) * ka)
    bonus = _head_sum(r * kmod * rk, bd) * v
    return r, logd, kmod, v, kk, kk * a, gate, bonus


def _head_block_ones():
    return jnp.where((_iota((LANES, LANES), 0) >> 6) == (_iota((LANES, LANES), 1) >> 6), 1.0, 0.0).astype(BF16)


def _rwkv_pre_seq_body(z_ref, halo_ref, prev_ref, mu_ref, w0_ref, a0_ref, kk_ref, ka_ref, rk_ref, w2_ref, a2_ref,
                       g2_ref, *out_refs, tt):
    i = pl.program_id(1)
    halo = jnp.where(i == 0, prev_ref[...], halo_ref[...])
    zd = z_ref[...]
    sh = pltpu.roll(jnp.concatenate([halo, zd], axis=0), 1, axis=0)[8:8 + tt]
    outs = _rwkv_pre(zd, sh, mu_ref[...], w0_ref[...], a0_ref[...], kk_ref[...], ka_ref[...], rk_ref[...],
                     w2_ref[...], a2_ref[...], g2_ref[...], _head_block_ones())
    for ref, val in zip(out_refs, outs):
        ref[...] = val


def _rwkv_pre_step_body(z_ref, sh_ref, mu_ref, w0_ref, a0_ref, kk_ref, ka_ref, rk_ref, w2_ref, a2_ref, g2_ref,
                        *out_refs):
    outs = _rwkv_pre(z_ref[...], sh_ref[...], mu_ref[...], w0_ref[...], a0_ref[...], kk_ref[...], ka_ref[...],
                     rk_ref[...], w2_ref[...], a2_ref[...], g2_ref[...], _head_block_ones())
    for ref, val in zip(out_refs[:6], outs[:6]):
        ref[...] = val.T
    for ref, val in zip(out_refs[6:], outs[6:]):
        ref[...] = val


def _rwkv_pre_param_specs():
    z2 = lambda *_: (0, 0)
    return ([pl.BlockSpec((1, P_D), z2)] + [pl.BlockSpec((1, W_D), z2)] * 5
            + [pl.BlockSpec((LANES, W_D), z2)] * 3)


def _rwkv_pre_seq(z, prev8, prm, tt):
    B, L, _ = z.shape
    hb = tt // 8
    out = jax.ShapeDtypeStruct((B, L, W_D), F32)
    return pl.pallas_call(
        functools.partial(_rwkv_pre_seq_body, tt=tt),
        grid=(B, L // tt),
        in_specs=[pl.BlockSpec((None, tt, P_D), lambda b, i: (b, i, 0)),
                  pl.BlockSpec((None, 8, P_D), lambda b, i: (b, jnp.maximum(i * hb - 1, 0), 0)),
                  pl.BlockSpec((None, 8, P_D), lambda b, i: (b, 0, 0))] + _rwkv_pre_param_specs(),
        out_specs=[pl.BlockSpec((None, tt, W_D), lambda b, i: (b, i, 0))] * 8,
        out_shape=[out] * 8,
        compiler_params=_cp(("arbitrary", "arbitrary")),
        name="rwkv7_token_shift_features",
    )(z, z, prev8, *prm)


def _rwkv_pre_step(z, shifted, prm):
    n = z.shape[0]
    fm, tm = jax.ShapeDtypeStruct((W_D, n), F32), jax.ShapeDtypeStruct((n, W_D), F32)
    whole = lambda s: pl.BlockSpec(s.shape, lambda i: (0, 0))
    outs = pl.pallas_call(
        _rwkv_pre_step_body,
        grid=(1,),
        in_specs=[pl.BlockSpec((n, P_D), lambda i: (0, 0)), pl.BlockSpec((n, P_D), lambda i: (0, 0))]
        + _rwkv_pre_param_specs(),
        out_specs=[whole(fm)] * 6 + [whole(tm)] * 2,
        out_shape=[fm] * 6 + [tm] * 2,
        compiler_params=_cp(("arbitrary",)),
        name="rwkv7_decode_features",
    )(z, shifted, *prm)
    return outs[:6], outs[6], outs[7]


def _unit_lower_inverse(N, C):
    n = N.shape[0]
    ri, ci = _iota((n, n), 0), _iota((n, n), 1)
    base = min(8, C)
    kb = _log2(base)
    X = jnp.where((ri >> kb) == (ci >> kb), N, 0.0)
    T = jnp.where(ri == ci, 1.0, 0.0) + X
    for _ in range(kb - 1):
        X = _dot(X, X)
        yield
        T = T + _dot(T, X)
        yield
    size = base
    while size < C:
        ks = _log2(size)
        sel = ((ri >> (ks + 1)) == (ci >> (ks + 1))) & (((ri >> ks) & 1) == 1) & (((ci >> ks) & 1) == 0)
        TN_ = _dot(T, jnp.where(sel, N, 0.0))
        yield
        T = T + _dot(TN_, T)
        yield
        size *= 2
    return T


def _interleave(gens):
    results = [None] * len(gens)
    live = list(range(len(gens)))
    while live:
        for i in list(live):
            try:
                next(gens[i])
            except StopIteration as stop:
                results[i] = stop.value
                live.remove(i)
    return results


def _rwkv_groupnorm_gate(y, gate, bonus, lnw, lnb):
    bd = _head_block_ones()
    mean = _dot_sel(y, bd, 2) * (1.0 / N_D)
    yield
    dy = y - mean
    var = _dot(dy * dy, bd) * (1.0 / N_D)
    yield
    return (dy * lax.rsqrt(var + GN_EPS_D) * lnw + lnb + bonus) * gate


def _rwkv_chunk(r, ld, k, v, kk, be, gate, bonus, lnw, lnb, P, *, C, valid, first=0):
    if valid < C:
        live = _live_rows(C, first, valid)
        ld = jnp.where(live, ld, 0.0)
        be = jnp.where(live, be, 0.0)
        k = jnp.where(live, k, 0.0)
        v = jnp.where(live, v, 0.0)
    tril = _iota((C, C), 0) >= _iota((C, C), 1)
    c = _sel_dot(jnp.where(tril, 1.0, 0.0).astype(BF16), ld, 2)
    g_col = _dot_sel_tn(ld, jnp.ones((C, LANES), BF16), 2)
    yield
    ec, enc = jnp.exp(c), jnp.exp(-c)
    lane0 = _iota((C, LANES), 1) < N_D

    def stack(x):
        return jnp.concatenate([jnp.where(lane0, x, 0.0), jnp.where(lane0, 0.0, x)], axis=0)

    n2 = 2 * C
    AR = jnp.concatenate([stack(-kk * jnp.exp(c - ld)), stack(r * ec)], axis=0).astype(BF16)
    BK = jnp.concatenate([stack(be * enc), stack(k * enc)], axis=0).astype(BF16)
    V2 = stack(v).astype(BF16)
    kc = _log2(C)
    ri, ci = _iota((2 * n2, 2 * n2), 0), _iota((2 * n2, 2 * n2), 1)
    tpos, spos = ri & (C - 1), ci & (C - 1)
    keep = (((ri >> kc) & 1) == ((ci >> kc) & 1)) & ((spos < tpos) | ((ri >= n2) & (spos == tpos)))
    G = jnp.where(keep, _dg(AR, BK, _NT), 0.0)
    LP = _mm(AR, P.astype(BF16))
    yield
    GV = _mm(G[:, n2:].astype(BF16), V2)
    T = yield from _unit_lower_inverse(G[0:n2, 0:n2], C)
    U = _dot(T, LP[0:n2] + GV[0:n2])
    yield
    Y2 = LP[n2:] + GV[n2:] + _dot(G[n2:, 0:n2], U)
    P_new = jnp.exp(g_col) * (P + _dg(BK, jnp.concatenate([U.astype(BF16), V2], axis=0), _TN))
    yield
    y = Y2[0:C] + Y2[C:n2]
    out = yield from _rwkv_groupnorm_gate(y, gate, bonus, lnw, lnb)
    return out, P_new


def _rwkv_body(r_ref, ld_ref, k_ref, v_ref, kk_ref, be_ref, gt_ref, bo_ref, lnw_ref, lnb_ref, p0_ref,
               y_ref, p1_ref, *, C, valid, nb, hp, nchunk):
    @pl.when(pl.program_id(2) == 0)
    def _():
        p1_ref[...] = p0_ref[...]

    probs = [(bi, pp) for bi in range(nb) for pp in range(hp)]
    in_refs = (r_ref, ld_ref, k_ref, v_ref, kk_ref, be_ref, gt_ref, bo_ref)

    def chunk(c, carry):
        rows = pl.ds(pl.multiple_of(c * C, C), C)
        gens = []
        for bi, pp in probs:
            lanes = slice(pp * LANES, (pp + 1) * LANES)
            args = [ref[bi, rows, lanes] for ref in in_refs] + [lnw_ref[:, lanes], lnb_ref[:, lanes], p1_ref[bi, pp]]
            gens.append(_rwkv_chunk(*args, C=C, valid=valid))
        for (bi, pp), (y, p_new) in zip(probs, _interleave(gens)):
            p1_ref[bi, pp] = p_new
            y_ref[bi, rows, pp * LANES:(pp + 1) * LANES] = y
        return carry

    lax.fori_loop(0, nchunk, chunk, 0)


def _rwkv_decode_body(r_ref, ld_ref, k_ref, v_ref, kk_ref, be_ref, s0_ref, *rest):
    y_ref, s1_ref = rest[-2:]
    S = s0_ref[...]
    sa = -jnp.sum(S * kk_ref[...][None], axis=1)
    S = S * jnp.exp(ld_ref[...])[None] + sa[:, None, :] * be_ref[...][None] + v_ref[...][:, None, :] * k_ref[...][None]
    s1_ref[...] = S
    y_ref[...] = jnp.sum(S * r_ref[...][None], axis=1)


def _rwkv_decode(feats_fm, s_all, s_prev, j):
    n = feats_fm[0].shape[1]
    row = pl.BlockSpec((N_D, n), lambda h: (h, 0))
    st = pl.BlockSpec((None, None, N_D, N_D, n), lambda h: (j, h, 0, 0, 0))
    operands = [*feats_fm, s_all]
    specs = [row] * 6 + [st]
    aliases = _chained(s_prev, 1, operands, specs)
    return pl.pallas_call(
        _rwkv_decode_body,
        grid=(H_D,),
        in_specs=specs,
        out_specs=[row, st],
        out_shape=[jax.ShapeDtypeStruct((W_D, n), F32), jax.ShapeDtypeStruct(s_all.shape, F32)],
        input_output_aliases=aliases,
        compiler_params=_cp(("arbitrary",)),
        name="rwkv7_decode_step",
    )(*operands)


def _rwkv_decode_out_body(y_ref, gt_ref, bo_ref, lnw_ref, lnb_ref, o_ref):
    y = y_ref[...].T
    bd = _head_block_ones()
    dy = y - _head_sum(y, bd) * (1.0 / N_D)
    var = _head_sum(dy * dy, bd) * (1.0 / N_D)
    o_ref[...] = (dy * lax.rsqrt(var + GN_EPS_D) * lnw_ref[...] + lnb_ref[...] + bo_ref[...]) * gt_ref[...]


def _rwkv_decode_out(y_fm, gate, bonus, lnw, lnb):
    n = y_fm.shape[1]
    return pl.pallas_call(
        _rwkv_decode_out_body,
        out_shape=jax.ShapeDtypeStruct((n, W_D), F32),
        name="rwkv7_decode_groupnorm_gate",
    )(y_fm, gate, bonus, lnw, lnb)


def _rwkv(feats, lnw, lnb, p0, *, C, valid, nb, hp, tt):
    B, L, _ = feats[0].shape
    w = hp * LANES
    col = pl.BlockSpec((nb, tt, w), lambda b, p, i: (b, i, p))
    vec = pl.BlockSpec((1, w), lambda b, p, i: (0, p))
    st = pl.BlockSpec((nb, hp, LANES, LANES), lambda b, p, i: (b, p, 0, 0))
    return pl.pallas_call(
        functools.partial(_rwkv_body, C=C, valid=valid, nb=nb, hp=hp, nchunk=tt // C),
        grid=(B // nb, H_D // 2 // hp, L // tt),
        in_specs=[col] * 8 + [vec, vec, st],
        out_specs=[col, st],
        out_shape=[jax.ShapeDtypeStruct((B, L, W_D), F32), jax.ShapeDtypeStruct(p0.shape, F32)],
        compiler_params=_cp(("arbitrary", "arbitrary", "arbitrary")),
        name="rwkv7_chunked",
    )(*feats, lnw, lnb, p0)


def _pair_blockdiag(s):
    B = s.shape[0]
    st = s.reshape(B, H_D // 2, 2, N_D, N_D)
    zero = jnp.zeros_like(st[:, :, 0])
    top = jnp.concatenate([st[:, :, 0], zero], axis=-1)
    bot = jnp.concatenate([zero, st[:, :, 1]], axis=-1)
    return jnp.concatenate([top, bot], axis=-2)


def _pair_unblockdiag(p):
    B = p.shape[0]
    st = jnp.stack([p[:, :, :N_D, :N_D], p[:, :, N_D:, N_D:]], axis=2)
    return st.reshape(B, H_D, N_D, N_D)


def _trunk(x, mod, st, prm, *, decode):
    G, Lg, _ = x.shape
    tm = min(128 if decode else 512, Lg)
    tm_in = tm if decode or Lg % 1024 else 1024
    assert decode or Lg % 512 == 0
    nseq = Lg if decode else G
    new = {k: [] for k in ("a_conv", "a_h", "d_shift", "d_s")}
    b_s = None
    c_in = (st["c_c"], st["c_n"][:, :, :, None, :],
            jnp.broadcast_to(st["c_m"][..., None, None], st["c_m"].shape + (1, LANES)))
    c_out = d_s = None
    d_s_in = jnp.transpose(st["d_s"], (0, 2, 3, 4, 1)) if decode else None
    blocks = lambda t: t.reshape(nseq // DEC_NB, DEC_NB, -1)
    ffn_bf16 = {} if decode else prm["ffn_bf16"]

    def ffn(x, l, which):
        if decode:
            assert Lg == tm
            x, ffn_bf16[l, which] = _ffn_cast(x, mod, prm["ffn_w_gate"], prm["ffn_w_up"], prm["ffn_w_down"], l, which)
            return x
        return _ffn(x, mod, ffn_bf16[l, which], l, which, tm)

    for l in range(DEPTH):
        j = l // 2
        x = ffn(x, l, 0)
        if l % 2 == 0:
            z = _inproj(x, mod, prm["w_in_even"], l, j, tm_in, tn=1024)
            a_prm = prm["rglru"][j]
            conv0, h0 = st["a_conv"][j], st["a_h"][j]
            if decode:
                z2 = z[0]
                y1, h1 = _rglru_step(z2, conv0, h0, a_prm)
                conv1 = jnp.concatenate([conv0[:, 1:], z2[:, None, :W_A]], axis=1)
                yb, b_s = _hgrn2(blocks(z2), prm["lb"][j], prm["b_norm_w"][j], st["b_s"], b_s, j,
                                 C=DEC_NB, valid=1, nb=DEC_NB, hp=2, tt=DEC_NB, decode=True)
                y1, y2 = y1[None], yb.reshape(1, nseq, -1)
            else:
                buf8 = jnp.concatenate([jnp.zeros((nseq, 8 - (CONV_W - 1), W_A), F32), conv0], axis=1)
                a, b = _rglru_seq(z, buf8, a_prm, 512)
                y1, h1 = _rglru_scan(a, b, z, h0, 512)
                conv1 = z[:, Lg - (CONV_W - 1):, :W_A]
                y2, b_s = _hgrn2(z, prm["lb"][j], prm["b_norm_w"][j], st["b_s"], b_s, j,
                                 C=CHUNK, valid=CHUNK, nb=nseq, hp=2, tt=512, decode=False)
            new["a_conv"].append(conv1)
            new["a_h"].append(h1)
            x = _outproj(x, mod, y1, y2, prm["w_out_even"], l, j, tm)
        else:
            z = _inproj(x, mod, prm["w_in_odd"], l, j, tm_in, width=MLSTM_COLS, tn=MLSTM_COLS // 4, w_is_out_by_in=True)
            zd = _inproj(x, mod, prm["w_in_rwkv"], l, j, tm, tn=P_D // 2, w_is_out_by_in=True)
            d_prm = prm["rwkv"][j]
            if decode:
                yc, c_out = _mlstm(blocks(z[0]), prm["c_gate_b"][j], prm["c_norm_w"][j], c_in, c_out, j,
                                   C=DEC_NB, valid=1, nb=DEC_NB, hp=2, tt=DEC_NB, decode=True)
                feats_fm, gate, bonus = _rwkv_pre_step(zd[0], st["d_shift"][j], d_prm)
                y_fm, d_s = _rwkv_decode(feats_fm, d_s_in, d_s, j)
                yd = _rwkv_decode_out(y_fm, gate, bonus, prm["d_ln_w"][j], prm["d_ln_b"][j])
                y1, y2 = yc.reshape(1, nseq, -1), yd[None]
                shift1 = zd[0]
            else:
                y1, c_out = _mlstm(z, prm["c_gate_b"][j], prm["c_norm_w"][j], c_in, c_out, j,
                                   C=CHUNK, valid=CHUNK, nb=nseq, hp=2, tt=256, decode=False)
                prev8 = jnp.concatenate([jnp.zeros((nseq, 7, P_D), F32), st["d_shift"][j][:, None]], axis=1)
                feats = _rwkv_pre_seq(zd, prev8, d_prm, 128)
                p0 = _pair_blockdiag(jnp.swapaxes(st["d_s"][j], -1, -2))
                y2, p1 = _rwkv(feats, prm["d_ln_w"][j], prm["d_ln_b"][j], p0, C=CHUNK, valid=CHUNK, nb=nseq, hp=2,
                               tt=256)
                shift1 = zd[:, -1]
                new["d_s"].append(jnp.swapaxes(_pair_unblockdiag(p1), -1, -2))
            new["d_shift"].append(shift1)
            x = _outproj(x, mod, y1, y2, prm["w_out_odd"], l, j, tm)
        x = ffn(x, l, 1)
    y = _final_norm(x, prm["final_norm_w"], tm)
    d_s = jnp.transpose(d_s, (0, 4, 1, 2, 3)) if decode else jnp.stack(new["d_s"])
    outs = (y, jnp.stack(new["a_conv"]), jnp.stack(new["a_h"]), b_s, c_out[0], c_out[1][:, :, :, 0, :],
            c_out[2][:, :, :, 0, 0], jnp.stack(new["d_shift"]), d_s)
    return outs, ffn_bf16


def _prepare(w_mod, b_mod, ffn_w_gate, ffn_w_up, ffn_w_down, w_in_even, w_out_even, a_conv_w, a_conv_b, a_gate_r_w,
             a_gate_r_b, a_gate_i_w, a_gate_i_b, a_lambda, b_lb_gamma, b_norm_w, w_in_odd, w_out_odd, c_igate_b,
             c_fgate_b, c_norm_w, d_mu, d_w0, d_w2, d_a0, d_a2, d_g2, d_k_k, d_k_a, d_r_k, d_ln_w, d_ln_b,
             final_norm_w):
    n_gate = 2 * H_C
    w_odd_t = jnp.swapaxes(w_in_odd, 1, 2).astype(BF16)
    row = lambda t: t[:, None, :]
    zpad = jnp.zeros((N_ODD, LANES - n_gate), F32)
    half = jnp.zeros((N_ODD, R_W, W_D), F32)
    rglru = [(a_conv_w[j], a_conv_b[j][None], a_gate_r_w[j].astype(BF16), a_gate_r_b[j][None],
              a_gate_i_w[j].astype(BF16), a_gate_i_b[j][None], a_lambda[j][None]) for j in range(N_EVEN)]
    w2p = jnp.concatenate([d_w2, half], axis=1).astype(BF16)
    a2p = jnp.concatenate([half, d_a2], axis=1).astype(BF16)
    g2 = d_g2.astype(BF16)
    rwkv = [(d_mu[j][None], d_w0[j][None], d_a0[j][None], d_k_k[j][None], d_k_a[j][None],
             d_r_k[j].reshape(1, W_D), w2p[j], a2p[j], g2[j]) for j in range(N_ODD)]
    return dict(
        ffn_w_gate=ffn_w_gate, ffn_w_up=ffn_w_up, ffn_w_down=ffn_w_down,
        w_in_even=w_in_even.astype(BF16), w_in_odd=w_odd_t, w_in_rwkv=w_odd_t[:, P_ODD - P_D:],
        w_out_even=w_out_even.astype(BF16).reshape(N_EVEN, 2, D // 2, D),
        w_out_odd=w_out_odd.astype(BF16).reshape(N_ODD, 2, D // 2, D),
        rglru=rglru, rwkv=rwkv,
        lb=row(_lower_bounds(b_lb_gamma)), b_norm_w=row(b_norm_w),
        c_gate_b=row(jnp.concatenate([c_igate_b, c_fgate_b, zpad], axis=-1)), c_norm_w=row(c_norm_w),
        d_ln_w=row(d_ln_w), d_ln_b=row(d_ln_b), final_norm_w=final_norm_w)


def kernel(x_prompt, x_sample, c_prompt, c_sample, state_a_conv, state_a_h, state_b_s, state_c_c, state_c_n, state_c_m, state_d_shift, state_d_s, w_mod, b_mod, ffn_w_gate, ffn_w_up, ffn_w_down, w_in_even, w_out_even, a_conv_w, a_conv_b, a_gate_r_w, a_gate_r_b, a_gate_i_w, a_gate_i_b, a_lambda, b_lb_gamma, b_norm_w, w_in_odd, w_out_odd, c_igate_b, c_fgate_b, c_norm_w, d_mu, d_w0, d_w2, d_a0, d_a2, d_g2, d_k_k, d_k_a, d_r_k, d_ln_w, d_ln_b, final_norm_w):
    prm = _prepare(w_mod, b_mod, ffn_w_gate, ffn_w_up, ffn_w_down, w_in_even, w_out_even, a_conv_w, a_conv_b,
                   a_gate_r_w, a_gate_r_b, a_gate_i_w, a_gate_i_b, a_lambda, b_lb_gamma, b_norm_w, w_in_odd,
                   w_out_odd, c_igate_b, c_fgate_b, c_norm_w, d_mu, d_w0, d_w2, d_a0, d_a2, d_g2, d_k_k, d_k_a,
                   d_r_k, d_ln_w, d_ln_b, final_norm_w)
    bp, lp, _ = x_prompt.shape
    bs = x_sample.shape[0]
    n_rows = -(-(bs + bp) // 8) * 8
    c_all = jnp.concatenate([c_sample, c_prompt, jnp.zeros((n_rows - bs - bp, D), F32)], axis=0)
    mod_all = _modulation(c_all, w_mod, b_mod)
    mod_s = mod_all[:, None]
    mod_p = mod_all[:, bs:bs + bp, None]

    zeros = lambda *s: jnp.zeros(s, F32)
    st_p = dict(a_conv=zeros(N_EVEN, bp, CONV_W - 1, W_A), a_h=zeros(N_EVEN, bp, W_A),
                b_s=zeros(N_EVEN, bp, H_B, DK_B, DV_B), c_c=zeros(N_ODD, bp, H_C, DK_C, DV_C),
                c_n=zeros(N_ODD, bp, H_C, DK_C), c_m=zeros(N_ODD, bp, H_C), d_shift=zeros(N_ODD, bp, P_D),
                d_s=zeros(N_ODD, bp, H_D, N_D, N_D))
    st_s = dict(a_conv=state_a_conv, a_h=state_a_h, b_s=state_b_s, c_c=state_c_c, c_n=state_c_n, c_m=state_c_m,
                d_shift=state_d_shift, d_s=state_d_s)
    out_s, ffn_bf16 = _trunk(x_sample.reshape(1, bs, D), mod_s, st_s, prm, decode=True)
    out_p, _ = _trunk(x_prompt, mod_p, st_p, dict(prm, ffn_bf16=ffn_bf16), decode=False)
    y_s = out_s[0].reshape(bs, 1, D)
    return (out_p[0], y_s) + out_p[1:] + out_s[1:]
```

```python
import functools

import jax
import jax.numpy as jnp
from jax import lax
from jax.experimental import pallas as pl
from jax.experimental.pallas import tpu as pltpu

F32 = jnp.float32
BF16 = jnp.bfloat16

D = 2048
DEPTH = 4
N_EVEN = 2
N_ODD = 2
D_FF = 5632
N_MOD = 9
EPS = 1e-6
CHUNK = 64
NEG_BIG = -1e30

W_A = 1024
NB_A = 8
BS_A = 128
CONV_W = 4
C_RGLRU = 8.0
H_B = 8
DK_B = 128
DV_B = 128
H_C = 4
DK_C = 128
DV_C = 256
N_D = 64
H_D = 16
W_D = 1024
R_W = 64
GN_EPS_D = 64e-5
P_EVEN = 6144
P_D = 3328
P_ODD = 6408
MLSTM_COLS = 3584
DEC_NB = 8
LANES = 128
VMEM_LIMIT = 48 * 2**20


def _cp(sem, vmem=VMEM_LIMIT):
    return pltpu.CompilerParams(dimension_semantics=sem, vmem_limit_bytes=vmem)


def _mm(a, b):
    return jnp.dot(a, b, preferred_element_type=F32)


def _dot(a, b):
    return _mm(a.astype(BF16), b.astype(BF16))


_NT = (((1,), (1,)), ((), ()))
_TN = (((0,), (0,)), ((), ()))
_NN = (((1,), (0,)), ((), ()))


def _dg(a, b, dn):
    return lax.dot_general(a, b, dn, preferred_element_type=F32)


def _dot_nt(a, b):
    return _dg(a.astype(BF16), b.astype(BF16), _NT)


def _dot_tn(a, b):
    return _dg(a.astype(BF16), b.astype(BF16), _TN)


def _split3(x):
    hi = x.astype(BF16)
    r1 = x - hi.astype(F32)
    mid = r1.astype(BF16)
    lo = (r1 - mid.astype(F32)).astype(BF16)
    return hi, mid, lo


def _split2(x):
    hi = x.astype(BF16)
    lo = (x - hi.astype(F32)).astype(BF16)
    return hi, lo


def _pieces(x, n):
    return _split3(x) if n == 3 else _split2(x)


def _sel_dot(mask_bf, x, n=3):
    return functools.reduce(jnp.add, [_mm(mask_bf, p) for p in _pieces(x, n)])


def _dot_sel(x, mask_bf, n=3):
    return functools.reduce(jnp.add, [_mm(p, mask_bf) for p in _pieces(x, n)])


def _dot_sel_tn(x, mask_bf, n=3):
    return functools.reduce(jnp.add, [_dg(p, mask_bf, _TN) for p in _pieces(x, n)])


def _iota(shape, dim):
    return lax.broadcasted_iota(jnp.int32, shape, dim)


def _sigmoid(x):
    return jax.nn.sigmoid(x)


def _softplus(x):
    return jnp.maximum(x, 0.0) + jnp.log1p(jnp.exp(-jnp.abs(x)))


def _log_sigmoid(x):
    return jnp.minimum(x, 0.0) - jnp.log1p(jnp.exp(-jnp.abs(x)))


def _gelu_tanh(x):
    return x * (0.5 * (1.0 + jnp.tanh(0.7978845608028654 * (x + 0.044715 * (x * x * x)))))


def _rms(x):
    return x * lax.rsqrt(jnp.mean(x * x, axis=-1, keepdims=True) + EPS)


def _adaln(x, shift, scale):
    return _rms(x) * (1.0 + scale) + shift


def _log2(n):
    k = n.bit_length() - 1
    assert (1 << k) == n
    return k


def _mod_spec(rows, width, layer, col):
    return pl.BlockSpec((None, None, rows, width), lambda g, i, n: (layer, g, 0, col))


def _mod_body(c_ref, w_ref, b_ref, o_ref):
    c = c_ref[...]
    cs = (c * _sigmoid(c)).astype(BF16)
    o_ref[...] = _mm(cs, w_ref[...].astype(BF16)) + b_ref[...]


def _modulation(c_all, w_mod, b_mod):
    rows = c_all.shape[0]
    tn = 1024
    return pl.pallas_call(
        _mod_body,
        grid=(DEPTH, N_MOD * D // tn),
        in_specs=[pl.BlockSpec((rows, D), lambda l, n: (0, 0)),
                  pl.BlockSpec((None, D, tn), lambda l, n: (l, 0, n)),
                  pl.BlockSpec((None, 1, tn), lambda l, n: (l, 0, n))],
        out_specs=pl.BlockSpec((None, rows, tn), lambda l, n: (l, 0, n)),
        out_shape=jax.ShapeDtypeStruct((DEPTH, rows, N_MOD * D), F32),
        compiler_params=_cp(("arbitrary", "arbitrary")),
        name="modulation",
    )(c_all, w_mod, b_mod.reshape(DEPTH, 1, N_MOD * D))


def _lower_bounds_body(g_ref, o_ref):
    g = g_ref[...]
    e = jnp.exp(g - jnp.max(g, axis=0, keepdims=True))
    sm = e / jnp.sum(e, axis=0, keepdims=True)
    acc = jnp.zeros_like(sm[0:1])
    for j in range(N_EVEN):
        acc = acc + sm[j:j + 1]
        o_ref[j:j + 1, :] = acc - sm[0:1]


def _lower_bounds(gamma):
    return pl.pallas_call(
        _lower_bounds_body,
        out_shape=jax.ShapeDtypeStruct(gamma.shape, F32),
        name="hgrn2_lower_bounds",
    )(gamma)


def _ffn_step(x_ref, sh_ref, sc_ref, gt_ref, wg_ref, wu_ref, wd_ref, o_ref, hf_ref, acc_ref):
    f = pl.program_id(2)

    @pl.when(f == 0)
    def _():
        hf_ref[...] = _adaln(x_ref[...], sh_ref[...], sc_ref[...]).astype(BF16)
        acc_ref[...] = jnp.zeros_like(acc_ref)

    hf = hf_ref[...]
    g = _mm(hf, wg_ref[...])
    u = _mm(hf, wu_ref[...])
    act = (g * _sigmoid(g) * u).astype(BF16)
    acc_ref[...] += _mm(act, wd_ref[...])

    @pl.when(f == pl.num_programs(2) - 1)
    def _():
        o_ref[...] = x_ref[...] + 0.5 * (1.0 + gt_ref[...]) * acc_ref[...]


def _ffn_body(x_ref, sh_ref, sc_ref, gt_ref, wg_ref, wu_ref, wd_ref, o_ref, hf_ref, acc_ref):
    _ffn_step(x_ref, sh_ref, sc_ref, gt_ref, wg_ref, wu_ref, wd_ref, o_ref, hf_ref, acc_ref)


def _ffn_cast_body(x_ref, sh_ref, sc_ref, gt_ref, wg_ref, wu_ref, wd_ref, o_ref, wgo_ref, wuo_ref, wdo_ref,
                   hf_ref, acc_ref):
    wgo_ref[...] = wg_ref[...].astype(BF16)
    wuo_ref[...] = wu_ref[...].astype(BF16)
    wdo_ref[...] = wd_ref[...].astype(BF16)
    _ffn_step(x_ref, sh_ref, sc_ref, gt_ref, wgo_ref, wuo_ref, wdo_ref, o_ref, hf_ref, acc_ref)


FFN_TF = 512


def _ffn_specs(x, mod, layer, which, tm):
    R = 1 if mod.shape[2] == 1 else tm
    kb = 6 * which
    return [pl.BlockSpec((None, tm, D), lambda g, i, f: (g, i, 0)),
            _mod_spec(R, D, layer, kb), _mod_spec(R, D, layer, kb + 1), _mod_spec(R, D, layer, kb + 2)]


def _ffn(x, mod, w_bf16, layer, which, tm):
    G, Lg, _ = x.shape
    tf = FFN_TF
    return pl.pallas_call(
        _ffn_body,
        grid=(G, Lg // tm, D_FF // tf),
        in_specs=_ffn_specs(x, mod, layer, which, tm) + [
            pl.BlockSpec((D, tf), lambda g, i, f: (0, f)), pl.BlockSpec((D, tf), lambda g, i, f: (0, f)),
            pl.BlockSpec((tf, D), lambda g, i, f: (f, 0))],
        out_specs=pl.BlockSpec((None, tm, D), lambda g, i, f: (g, i, 0)),
        out_shape=jax.ShapeDtypeStruct(x.shape, F32),
        scratch_shapes=[pltpu.VMEM((tm, D), BF16), pltpu.VMEM((tm, D), F32)],
        compiler_params=_cp(("arbitrary", "arbitrary", "arbitrary")),
        name="adaln_swiglu_ffn",
    )(x, mod, mod, mod, *w_bf16)


def _ffn_cast(x, mod, wg, wu, wd, layer, which):
    G, tm, _ = x.shape
    assert G == 1
    tf = FFN_TF // 2
    bf = lambda s: jax.ShapeDtypeStruct(s, BF16)
    outs = pl.pallas_call(
        _ffn_cast_body,
        grid=(1, 1, D_FF // tf),
        in_specs=_ffn_specs(x, mod, layer, which, tm) + [
            pl.BlockSpec((None, None, D, tf), lambda g, i, f: (layer, which, 0, f)),
            pl.BlockSpec((None, None, D, tf), lambda g, i, f: (layer, which, 0, f)),
            pl.BlockSpec((None, None, tf, D), lambda g, i, f: (layer, which, f, 0))],
        out_specs=[pl.BlockSpec((None, tm, D), lambda g, i, f: (g, i, 0)),
                   pl.BlockSpec((D, tf), lambda g, i, f: (0, f)), pl.BlockSpec((D, tf), lambda g, i, f: (0, f)),
                   pl.BlockSpec((tf, D), lambda g, i, f: (f, 0))],
        out_shape=[jax.ShapeDtypeStruct(x.shape, F32), bf((D, D_FF)), bf((D, D_FF)), bf((D_FF, D))],
        scratch_shapes=[pltpu.VMEM((tm, D), BF16), pltpu.VMEM((tm, D), F32)],
        compiler_params=_cp(("arbitrary", "arbitrary", "arbitrary")),
        name="adaln_swiglu_ffn_weight_cast",
    )(x, mod, mod, mod, wg, wu, wd)
    return outs[0], tuple(outs[1:])


def _inproj_body(x_ref, sh_ref, sc_ref, w_ref, o_ref, hf_ref, *, w_is_out_by_in):
    @pl.when(pl.program_id(2) == 0)
    def _():
        hf_ref[...] = _adaln(x_ref[...], sh_ref[...], sc_ref[...]).astype(BF16)

    o_ref[...] = _dg(hf_ref[...], w_ref[...], _NT if w_is_out_by_in else _NN)


def _inproj(x, mod, w, layer, j, tm, width=None, tn=512, w_is_out_by_in=False):
    G, Lg, _ = x.shape
    R = 1 if mod.shape[2] == 1 else tm
    n_out = w.shape[1] if w_is_out_by_in else w.shape[2]
    P = n_out if width is None else width
    assert P % tn == 0 and P <= n_out
    w_spec = (pl.BlockSpec((None, tn, D), lambda g, i, n: (j, n, 0)) if w_is_out_by_in
              else pl.BlockSpec((None, D, tn), lambda g, i, n: (j, 0, n)))
    return pl.pallas_call(
        functools.partial(_inproj_body, w_is_out_by_in=w_is_out_by_in),
        grid=(G, Lg // tm, P // tn),
        in_specs=[pl.BlockSpec((None, tm, D), lambda g, i, n: (g, i, 0)),
                  _mod_spec(R, D, layer, 3), _mod_spec(R, D, layer, 4), w_spec],
        out_specs=pl.BlockSpec((None, tm, tn), lambda g, i, n: (g, i, n)),
        out_shape=jax.ShapeDtypeStruct((G, Lg, P), F32),
        scratch_shapes=[pltpu.VMEM((tm, D), BF16)],
        compiler_params=_cp(("arbitrary", "arbitrary", "arbitrary")),
        name="adaln_in_projection",
    )(x, mod, mod, w)


def _outproj_body(x_ref, gt_ref, y1_ref, y2_ref, w1_ref, w2_ref, o_ref):
    y = _dot(y1_ref[...], w1_ref[...]) + _dot(y2_ref[...], w2_ref[...])
    o_ref[...] = x_ref[...] + (1.0 + gt_ref[...]) * y


def _outproj(x, mod, y1, y2, w, layer, j, tm):
    G, Lg, _ = x.shape
    R = 1 if mod.shape[2] == 1 else tm
    tn = D
    nb = D // tn
    half = y1.shape[-1]
    return pl.pallas_call(
        _outproj_body,
        grid=(G, Lg // tm, nb),
        in_specs=[pl.BlockSpec((None, tm, tn), lambda g, i, n: (g, i, n)),
                  pl.BlockSpec((None, None, R, tn), lambda g, i, n: (layer, g, 0, 5 * nb + n)),
                  pl.BlockSpec((None, tm, half), lambda g, i, n: (g, i, 0)),
                  pl.BlockSpec((None, tm, half), lambda g, i, n: (g, i, 0)),
                  pl.BlockSpec((None, None, half, tn), lambda g, i, n: (j, 0, 0, n)),
                  pl.BlockSpec((None, None, half, tn), lambda g, i, n: (j, 1, 0, n))],
        out_specs=pl.BlockSpec((None, tm, tn), lambda g, i, n: (g, i, n)),
        out_shape=jax.ShapeDtypeStruct(x.shape, F32),
        compiler_params=_cp(("arbitrary", "arbitrary", "arbitrary")),
        name="out_projection_residual",
    )(x, mod, y1, y2, w, w)


def _final_body(x_ref, w_ref, o_ref):
    o_ref[...] = _rms(x_ref[...]) * w_ref[...]


def _final_norm(x, w, tm):
    G, Lg, _ = x.shape
    return pl.pallas_call(
        _final_body,
        grid=(G, Lg // tm),
        in_specs=[pl.BlockSpec((None, tm, D), lambda g, i: (g, i, 0)),
                  pl.BlockSpec((1, D), lambda g, i: (0, 0))],
        out_specs=pl.BlockSpec((None, tm, D), lambda g, i: (g, i, 0)),
        out_shape=jax.ShapeDtypeStruct(x.shape, F32),
        compiler_params=_cp(("arbitrary", "arbitrary")),
        name="final_rmsnorm",
    )(x, w.reshape(1, D))


def _rglru_gates(x0, x1, x2, x3, cw, cb, wr, br, wi, bi, lam):
    u = cb + x0 * cw[0:1] + x1 * cw[1:2] + x2 * cw[2:3] + x3 * cw[3:4]
    r_parts, i_parts = [], []
    for n in range(NB_A):
        un = u[:, n * BS_A:(n + 1) * BS_A].astype(BF16)
        r_parts.append(_mm(un, wr[n]))
        i_parts.append(_mm(un, wi[n]))
    r = _sigmoid(jnp.concatenate(r_parts, axis=1) + br)
    ig = _sigmoid(jnp.concatenate(i_parts, axis=1) + bi)
    log_a = -C_RGLRU * r * _softplus(-lam)
    a = jnp.exp(log_a)
    b = jnp.sqrt(-jnp.tanh(log_a) * (a * a + 1.0)) * (ig * u)
    return a, b


def _rglru_seq_body(x_ref, halo_ref, buf_ref, cw_ref, cb_ref, wr_ref, br_ref, wi_ref, bi_ref, lam_ref,
                    a_ref, b_ref, *, tt):
    i = pl.program_id(1)
    halo = jnp.where(i == 0, buf_ref[...], halo_ref[...])
    full = jnp.concatenate([halo, x_ref[...]], axis=0)
    taps = [pltpu.roll(full, CONV_W - 1 - k, axis=0)[8:8 + tt] for k in range(CONV_W - 1)]
    a, b = _rglru_gates(taps[0], taps[1], taps[2], x_ref[...], cw_ref[...], cb_ref[...], wr_ref[...], br_ref[...],
                        wi_ref[...], bi_ref[...], lam_ref[...])
    a_ref[...] = a
    b_ref[...] = b


def _rglru_param_specs():
    z2 = lambda *_: (0, 0)
    z3 = lambda *_: (0, 0, 0)
    return [pl.BlockSpec((CONV_W, W_A), z2), pl.BlockSpec((1, W_A), z2),
            pl.BlockSpec((NB_A, BS_A, BS_A), z3), pl.BlockSpec((1, W_A), z2),
            pl.BlockSpec((NB_A, BS_A, BS_A), z3), pl.BlockSpec((1, W_A), z2),
            pl.BlockSpec((1, W_A), z2)]


def _rglru_seq(z, buf8, prm, tt):
    B, L, _ = z.shape
    hb = tt // 8
    out = jax.ShapeDtypeStruct((B, L, W_A), F32)
    return pl.pallas_call(
        functools.partial(_rglru_seq_body, tt=tt),
        grid=(B, L // tt),
        in_specs=[pl.BlockSpec((None, tt, W_A), lambda b, i: (b, i, 0)),
                  pl.BlockSpec((None, 8, W_A), lambda b, i: (b, jnp.maximum(i * hb - 1, 0), 0)),
                  pl.BlockSpec((None, 8, W_A), lambda b, i: (b, 0, 0))] + _rglru_param_specs(),
        out_specs=[pl.BlockSpec((None, tt, W_A), lambda b, i: (b, i, 0))] * 2,
        out_shape=[out, out],
        compiler_params=_cp(("arbitrary", "arbitrary")),
        name="rglru_conv_gates",
    )(z, z, buf8, *prm)


def _rglru_scan_body(a_ref, b_ref, ag_ref, h0_ref, y_ref, hl_ref, h_scr, *, tt):
    @pl.when(pl.program_id(1) == 0)
    def _():
        h_scr[...] = h0_ref[...]

    def step(t, h):
        h = a_ref[t] * h + b_ref[t]
        y_ref[t] = h
        return h

    h = lax.fori_loop(0, tt, step, h_scr[...], unroll=8)
    h_scr[...] = h
    hl_ref[...] = h
    y_ref[...] = y_ref[...] * _gelu_tanh(ag_ref[...])


def _rglru_scan(a, b, z, h0, tt):
    B, L, _ = a.shape
    a4 = a.reshape(B, L, 8, LANES)
    b4 = b.reshape(B, L, 8, LANES)
    z4 = z.reshape(B, L, z.shape[-1] // LANES, LANES)
    spec = pl.BlockSpec((None, tt, 8, LANES), lambda bb, i: (bb, i, 0, 0))
    y, hl = pl.pallas_call(
        functools.partial(_rglru_scan_body, tt=tt),
        grid=(B, L // tt),
        in_specs=[spec, spec,
                  pl.BlockSpec((None, tt, 8, LANES), lambda bb, i: (bb, i, 1, 0)),
                  pl.BlockSpec((None, 8, LANES), lambda bb, i: (bb, 0, 0))],
        out_specs=[spec, pl.BlockSpec((None, 8, LANES), lambda bb, i: (bb, 0, 0))],
        out_shape=[jax.ShapeDtypeStruct((B, L, 8, LANES), F32), jax.ShapeDtypeStruct((B, 8, LANES), F32)],
        scratch_shapes=[pltpu.VMEM((8, LANES), F32)],
        compiler_params=_cp(("arbitrary", "arbitrary")),
        name="rglru_scan_gelu_gate",
    )(a4, b4, z4, h0.reshape(B, 8, LANES))
    return y.reshape(B, L, W_A), hl.reshape(B, W_A)


def _rglru_step_body(x3_ref, ag_ref, x0_ref, x1_ref, x2_ref, h0_ref, cw_ref, cb_ref, wr_ref, br_ref, wi_ref, bi_ref,
                     lam_ref, y_ref, h_ref):
    a, b = _rglru_gates(x0_ref[...], x1_ref[...], x2_ref[...], x3_ref[...], cw_ref[...], cb_ref[...], wr_ref[...],
                        br_ref[...], wi_ref[...], bi_ref[...], lam_ref[...])
    h = a * h0_ref[...] + b
    h_ref[...] = h
    y_ref[...] = h * _gelu_tanh(ag_ref[...])


def _rglru_step(z, buf, h0, prm):
    n = z.shape[0]
    row = lambda c: pl.BlockSpec((n, W_A), lambda i: (0, c))
    out = jax.ShapeDtypeStruct((n, W_A), F32)
    return pl.pallas_call(
        _rglru_step_body,
        grid=(1,),
        in_specs=[row(0), row(1), row(0), row(0), row(0), row(0)] + _rglru_param_specs(),
        out_specs=[row(0), row(0)],
        out_shape=[out, out],
        compiler_params=_cp(("arbitrary",)),
        name="rglru_decode_step",
    )(z, z, buf[:, 0], buf[:, 1], buf[:, 2], h0, *prm)


def _live_rows(C, first, valid):
    row = _iota((C, 1), 0)
    return (row >= first) & (row < first + valid)


def _hgrn2_chunk(q, fp, v, gate, lb, nw, S, *, C, valid, first=0):
    SB = min(8, C)
    k = (1.0 - lb) * _sigmoid(-fp)
    lf = jnp.log1p(-k)
    if valid < C:
        live = _live_rows(C, first, valid)
        k = jnp.where(live, k, 0.0)
        lf = jnp.where(live, lf, 0.0)
    g_col = _dot_sel_tn(lf, jnp.ones((C, DV_B), BF16))
    if valid == 1:
        kv = _dot_tn(k, v)
        yield
        S_new = jnp.exp(g_col) * S + kv
        o = _dot(q, S_new)
        yield
        return _rms(o) * nw * (gate * _sigmoid(gate)), S_new
    tril = _iota((C, C), 0) >= _iota((C, C), 1)
    b = _sel_dot(jnp.where(tril, 1.0, 0.0).astype(BF16), lf)
    yield
    g_row = b[C - 1:C, :]
    o = _dot(q * jnp.exp(b), S)
    khat = k * jnp.exp(g_row - b)
    S_new = jnp.exp(g_col) * S + _dot_tn(khat, v)

    tril_sb = _iota((SB, SB), 0) >= _iota((SB, SB), 1)
    atts = []
    for i in range(C // SB):
        lo = i * SB
        qi, bi, ki = q[lo:lo + SB], b[lo:lo + SB], k[lo:lo + SB]
        ci = bi - jnp.log(ki)
        att = jnp.where(tril_sb, jnp.sum(qi[:, None, :] * jnp.exp(bi[:, None, :] - ci[None, :, :]), axis=-1), 0.0)
        off = None
        if i > 0:
            ref = b[lo - 1:lo, :]
            qt = qi * jnp.exp(bi - ref)
            kt = k[0:lo] * jnp.exp(ref - b[0:lo])
            off = _dot_nt(qt, kt)
        atts.append((att, off))
    yield
    parts = []
    for i, (att, off) in enumerate(atts):
        lo = i * SB
        oi = _dot(att, v[lo:lo + SB])
        parts.append(oi if off is None else oi + _dot(off, v[0:lo]))
    yield
    o = o + (parts[0] if len(parts) == 1 else jnp.concatenate(parts, axis=0))
    return _rms(o) * nw * (gate * _sigmoid(gate)), S_new


def _merge_rows(outs, C):
    row = _iota((C, 1), 0)
    y = jnp.where(row == 0, outs[0], 0.0)
    for s in range(1, len(outs)):
        y = jnp.where(row == s, outs[s], y)
    return y


def _chained(prev, out_index, operands, specs):
    if prev is None:
        return {}
    operands.append(prev)
    specs.append(pl.BlockSpec(memory_space=pl.ANY))
    return {len(operands) - 1: out_index}


def _hgrn2_body(q_ref, f_ref, v_ref, g_ref, lb_ref, nw_ref, s0_ref, *rest, C, valid, nb, hp, decode, nchunk):
    y_ref, s1_ref = rest[-2:]

    @pl.when(pl.program_id(2) == 0)
    def _():
        s1_ref[...] = s0_ref[...]

    nw = nw_ref[...]
    probs = [(bi, hh) for bi in range(nb) for hh in range(hp)]

    def chunk(c, carry):
        rows = pl.ds(pl.multiple_of(c * C, C), C)
        gens = []
        for bi, hh in probs:
            lanes = slice(hh * LANES, (hh + 1) * LANES)
            src = 0 if decode else bi
            gens.append(_hgrn2_chunk(q_ref[src, rows, lanes], f_ref[src, rows, lanes], v_ref[src, rows, lanes],
                                     g_ref[src, rows, lanes], lb_ref[:, lanes], nw, s1_ref[bi, hh],
                                     C=C, valid=valid, first=bi if decode else 0))
        outs = _interleave(gens)
        for (bi, hh), (y, s_new) in zip(probs, outs):
            s1_ref[bi, hh] = s_new
            if not decode:
                y_ref[bi, rows, hh * LANES:(hh + 1) * LANES] = y
        if decode:
            for hh in range(hp):
                y_ref[0, rows, hh * LANES:(hh + 1) * LANES] = _merge_rows([outs[bi * hp + hh][0] for bi in range(nb)], C)
        return carry

    lax.fori_loop(0, nchunk, chunk, 0)


def _hgrn2(z, lb, nw, s_all, s_prev, j, *, C, valid, nb, hp, tt, decode):
    B, L, _ = z.shape
    zb = 1 if decode else nb
    w = hp * LANES
    col = lambda base: pl.BlockSpec((zb, tt, w), lambda b, h, i: (b, i, base // hp + h))
    st = pl.BlockSpec((None, nb, hp, DK_B, DV_B), lambda b, h, i: (j, b, h, 0, 0))
    operands = [z, z, z, z, lb, nw, s_all]
    specs = [col(16), col(24), col(32), col(40), pl.BlockSpec((1, w), lambda b, h, i: (0, h)),
             pl.BlockSpec((1, LANES), lambda b, h, i: (0, 0)), st]
    aliases = _chained(s_prev, 1, operands, specs)
    return pl.pallas_call(
        functools.partial(_hgrn2_body, C=C, valid=valid, nb=nb, hp=hp, decode=decode, nchunk=tt // C),
        grid=(B // zb, H_B // hp, L // tt),
        in_specs=specs,
        out_specs=[col(0), st],
        out_shape=[jax.ShapeDtypeStruct((B, L, H_B * DV_B), F32), jax.ShapeDtypeStruct(s_all.shape, F32)],
        input_output_aliases=aliases,
        compiler_params=_cp(("arbitrary", "arbitrary", "arbitrary")),
        name="hgrn2_chunked",
    )(*operands)


def _mlstm_step(q, k, v, op, gt, h, nw, Cst, n, m, *, C, first):
    lane = _iota((C, LANES), 1)
    live = _live_rows(C, first, 1)
    i0 = jnp.sum(jnp.where(live & (lane == h), gt, 0.0), keepdims=True)
    lf0 = _log_sigmoid(jnp.sum(jnp.where(live & (lane == h + H_C), gt, 0.0), keepdims=True))
    inter = lf0 + m
    m_t = jnp.maximum(inter, i0)
    w_inter = jnp.exp(inter - m_t)
    wk = jnp.exp(i0 - m_t)
    kh = k * (DK_C ** -0.5)
    s = jnp.sum(q * kh, axis=1, keepdims=True) * wk
    qc = _dot(q, Cst)
    kw = jnp.where(live, kh * wk, 0.0)
    kv = _dot_tn(kw, v)
    yield
    num = w_inter * qc + s * v
    den = w_inter * jnp.sum(q * n, axis=1, keepdims=True) + s
    hh = num / jnp.maximum(jnp.abs(den), jnp.exp(-m_t))
    C_new = w_inter * Cst + kv
    n_new = w_inter * n + jnp.sum(kw, axis=0, keepdims=True)
    y = _rms(hh) * nw * _sigmoid(op)
    return y, C_new, n_new, m_t


def _mlstm_chunk(q, k, v, op, gt, h, nw, Cst, n, m, *, C, valid, first=0):
    if valid == 1:
        return (yield from _mlstm_step(q, k, v, op, gt, h, nw, Cst, n, m, C=C, first=first))
    assert first == 0
    gtT = gt.T
    lane = _iota((C, LANES), 1)
    sub = _iota((LANES, C), 0)
    i_col = jnp.sum(jnp.where(lane == h, gt, 0.0), axis=1, keepdims=True)
    f_col = jnp.sum(jnp.where(lane == h + H_C, gt, 0.0), axis=1, keepdims=True)
    i_row = jnp.sum(jnp.where(sub == h, gtT, 0.0), axis=0, keepdims=True)
    f_row = jnp.sum(jnp.where(sub == h + H_C, gtT, 0.0), axis=0, keepdims=True)
    lf_col = _log_sigmoid(f_col)
    lf_row = _log_sigmoid(f_row)
    if valid < C:
        live_c = _iota((C, 1), 0) < valid
        live_r = _iota((1, C), 1) < valid
        i_col = jnp.where(live_c, i_col, NEG_BIG)
        i_row = jnp.where(live_r, i_row, NEG_BIG)
        lf_col = jnp.where(live_c, lf_col, 0.0)
        lf_row = jnp.where(live_r, lf_row, 0.0)
    tril = _iota((C, C), 0) >= _iota((C, C), 1)
    b_col = jnp.sum(jnp.where(tril, lf_row, 0.0), axis=1, keepdims=True)
    b_row = jnp.sum(jnp.where(_iota((C, C), 0) <= _iota((C, C), 1), lf_col, 0.0), axis=0, keepdims=True)
    dm = jnp.where(tril, b_col - b_row + i_row, NEG_BIG)
    inter = b_col + m
    m_t = jnp.maximum(inter, jnp.max(dm, axis=1, keepdims=True))
    w_inter = jnp.exp(inter - m_t)
    kh = k * (DK_C ** -0.5)
    qk = _dot_nt(q, kh)
    qc = _dot(q, Cst)
    g = b_col[C - 1:C, :]
    m_new = m_t[C - 1:C, :]
    wk = jnp.exp(g - b_col + i_col - m_new)
    f_state = jnp.exp(g + m - m_new)
    kw = kh * wk
    kv = _dot_tn(kw, v)
    yield
    s = qk * jnp.exp(dm - m_t)
    sv = _dot(s, v)
    yield
    num = w_inter * qc + sv
    den = w_inter * jnp.sum(q * n, axis=1, keepdims=True) + jnp.sum(s, axis=1, keepdims=True)
    hh = num / jnp.maximum(jnp.abs(den), jnp.exp(-m_t))
    C_new = f_state * Cst + kv
    n_new = f_state * n + jnp.sum(kw, axis=0, keepdims=True)
    y = _rms(hh) * nw * _sigmoid(op)
    return y, C_new, n_new, m_new


def _mlstm_body(q_ref, k_ref, v_ref, o_ref, g_ref, gb_ref, nw_ref, c0_ref, n0_ref, m0_ref, *rest,
                C, valid, nb, hp, decode, nchunk):
    y_ref, c1_ref, n1_ref, m1_ref = rest[-4:]

    @pl.when(pl.program_id(2) == 0)
    def _():
        c1_ref[...] = c0_ref[...]
        n1_ref[...] = n0_ref[...]
        m1_ref[...] = m0_ref[...]

    gb = gb_ref[...]
    nw = nw_ref[...]
    probs = [(bi, hh) for bi in range(nb) for hh in range(hp)]

    def chunk(c, carry):
        rows = pl.ds(pl.multiple_of(c * C, C), C)
        gens = []
        for bi, hh in probs:
            src = 0 if decode else bi
            kl = slice(hh * DK_C, (hh + 1) * DK_C)
            vl = slice(hh * DV_C, (hh + 1) * DV_C)
            gens.append(_mlstm_chunk(
                q_ref[src, rows, kl], k_ref[src, rows, kl], v_ref[src, rows, vl], o_ref[src, rows, vl],
                g_ref[src, rows, :] + gb, pl.program_id(1) * hp + hh, nw, c1_ref[bi, hh], n1_ref[bi, hh],
                m1_ref[bi, hh][:, 0:1], C=C, valid=valid, first=bi if decode else 0))
        outs = _interleave(gens)
        for (bi, hh), (y, c_new, n_new, m_new) in zip(probs, outs):
            if not decode:
                y_ref[bi, rows, hh * DV_C:(hh + 1) * DV_C] = y
            c1_ref[bi, hh] = c_new
            n1_ref[bi, hh] = n_new
            m1_ref[bi, hh] = jnp.broadcast_to(m_new, (1, LANES))
        if decode:
            for hh in range(hp):
                y_ref[0, rows, hh * DV_C:(hh + 1) * DV_C] = _merge_rows([outs[bi * hp + hh][0] for bi in range(nb)], C)
        return carry

    lax.fori_loop(0, nchunk, chunk, 0)


def _mlstm(z, gb, nw, states, prev, j, *, C, valid, nb, hp, tt, decode):
    B, L, _ = z.shape
    zb = 1 if decode else nb
    c128 = lambda base: pl.BlockSpec((zb, tt, hp * DK_C), lambda b, h, i: (b, i, base // hp + h))
    c256 = lambda base: pl.BlockSpec((zb, tt, hp * DV_C), lambda b, h, i: (b, i, base // hp + h))
    cst = pl.BlockSpec((None, nb, hp, DK_C, DV_C), lambda b, h, i: (j, b, h, 0, 0))
    vec = pl.BlockSpec((None, nb, hp, 1, LANES), lambda b, h, i: (j, b, h, 0, 0))
    operands = [z, z, z, z, z, gb, nw, *states]
    specs = [c128(0), c128(4), c256(4), c256(8), pl.BlockSpec((zb, tt, LANES), lambda b, h, i: (b, i, 24)),
             pl.BlockSpec((1, LANES), lambda b, h, i: (0, 0)), pl.BlockSpec((1, DV_C), lambda b, h, i: (0, 0)),
             cst, vec, vec]
    aliases = {}
    for k, p in enumerate(prev or ()):
        aliases.update(_chained(p, 1 + k, operands, specs))
    outs = pl.pallas_call(
        functools.partial(_mlstm_body, C=C, valid=valid, nb=nb, hp=hp, decode=decode, nchunk=tt // C),
        grid=(B // zb, H_C // hp, L // tt),
        in_specs=specs,
        out_specs=[c256(0), cst, vec, vec],
        out_shape=[jax.ShapeDtypeStruct((B, L, H_C * DV_C), F32)] + [jax.ShapeDtypeStruct(s.shape, F32) for s in states],
        input_output_aliases=aliases,
        compiler_params=_cp(("arbitrary", "arbitrary", "arbitrary")),
        name="mlstm_chunked",
    )(*operands)
    return outs[0], tuple(outs[1:])


def _head_sum(x, bd):
    return jnp.concatenate([_dot_sel(x[:, n * LANES:(n + 1) * LANES], bd) for n in range(W_D // LANES)], axis=1)


def _rwkv_pre(zd, sh, mu, w0, a0, kkw, ka, rk, w2p, a2p, g2, bd):
    zs = zd + (sh - zd) * mu
    r, k, v = zs[:, 0:W_D], zs[:, W_D:2 * W_D], zs[:, 2 * W_D:3 * W_D]
    wa = zs[:, 3 * W_D:3 * W_D + LANES]
    gl = zs[:, 3 * W_D + LANES:]
    w = -_softplus(-(w0 + _dot(jnp.tanh(wa), w2p))) - 0.5
    logd = -jnp.exp(w)
    a = _sigmoid(a0 + _dot(wa, a2p))
    gate = _dot(_sigmoid(gl), g2)
    kk = k * kkw
    kk = kk / jnp.maximum(jnp.sqrt(_head_sum(kk * kk, bd)), 1e-12)
    kmod = k * (1.0 + (a - 1.0) * ka)
    bonus = _head_sum(r * kmod * rk, bd) * v
    return r, logd, kmod, v, kk, kk * a, gate, bonus


def _head_block_ones():
    return jnp.where((_iota((LANES, LANES), 0) >> 6) == (_iota((LANES, LANES), 1) >> 6), 1.0, 0.0).astype(BF16)


def _rwkv_pre_seq_body(z_ref, halo_ref, prev_ref, mu_ref, w0_ref, a0_ref, kk_ref, ka_ref, rk_ref, w2_ref, a2_ref,
                       g2_ref, *out_refs, tt):
    i = pl.program_id(1)
    halo = jnp.where(i == 0, prev_ref[...], halo_ref[...])
    zd = z_ref[...]
    sh = pltpu.roll(jnp.concatenate([halo, zd], axis=0), 1, axis=0)[8:8 + tt]
    outs = _rwkv_pre(zd, sh, mu_ref[...], w0_ref[...], a0_ref[...], kk_ref[...], ka_ref[...], rk_ref[...],
                     w2_ref[...], a2_ref[...], g2_ref[...], _head_block_ones())
    for ref, val in zip(out_refs, outs):
        ref[...] = val


def _rwkv_pre_step_body(z_ref, sh_ref, mu_ref, w0_ref, a0_ref, kk_ref, ka_ref, rk_ref, w2_ref, a2_ref, g2_ref,
                        *out_refs):
    outs = _rwkv_pre(z_ref[...], sh_ref[...], mu_ref[...], w0_ref[...], a0_ref[...], kk_ref[...], ka_ref[...],
                     rk_ref[...], w2_ref[...], a2_ref[...], g2_ref[...], _head_block_ones())
    for ref, val in zip(out_refs[:6], outs[:6]):
        ref[...] = val.T
    for ref, val in zip(out_refs[6:], outs[6:]):
        ref[...] = val


def _rwkv_pre_param_specs():
    z2 = lambda *_: (0, 0)
    return ([pl.BlockSpec((1, P_D), z2)] + [pl.BlockSpec((1, W_D), z2)] * 5
            + [pl.BlockSpec((LANES, W_D), z2)] * 3)


def _rwkv_pre_seq(z, prev8, prm, tt):
    B, L, _ = z.shape
    hb = tt // 8
    out = jax.ShapeDtypeStruct((B, L, W_D), F32)
    return pl.pallas_call(
        functools.partial(_rwkv_pre_seq_body, tt=tt),
        grid=(B, L // tt),
        in_specs=[pl.BlockSpec((None, tt, P_D), lambda b, i: (b, i, 0)),
                  pl.BlockSpec((None, 8, P_D), lambda b, i: (b, jnp.maximum(i * hb - 1, 0), 0)),
                  pl.BlockSpec((None, 8, P_D), lambda b, i: (b, 0, 0))] + _rwkv_pre_param_specs(),
        out_specs=[pl.BlockSpec((None, tt, W_D), lambda b, i: (b, i, 0))] * 8,
        out_shape=[out] * 8,
        compiler_params=_cp(("arbitrary", "arbitrary")),
        name="rwkv7_token_shift_features",
    )(z, z, prev8, *prm)


def _rwkv_pre_step(z, shifted, prm):
    n = z.shape[0]
    fm, tm = jax.ShapeDtypeStruct((W_D, n), F32), jax.ShapeDtypeStruct((n, W_D), F32)
    whole = lambda s: pl.BlockSpec(s.shape, lambda i: (0, 0))
    outs = pl.pallas_call(
        _rwkv_pre_step_body,
        grid=(1,),
        in_specs=[pl.BlockSpec((n, P_D), lambda i: (0, 0)), pl.BlockSpec((n, P_D), lambda i: (0, 0))]
        + _rwkv_pre_param_specs(),
        out_specs=[whole(fm)] * 6 + [whole(tm)] * 2,
        out_shape=[fm] * 6 + [tm] * 2,
        compiler_params=_cp(("arbitrary",)),
        name="rwkv7_decode_features",
    )(z, shifted, *prm)
    return outs[:6], outs[6], outs[7]


def _unit_lower_inverse(N, C):
    n = N.shape[0]
    ri, ci = _iota((n, n), 0), _iota((n, n), 1)
    base = min(8, C)
    kb = _log2(base)
    X = jnp.where((ri >> kb) == (ci >> kb), N, 0.0)
    T = jnp.where(ri == ci, 1.0, 0.0) + X
    for _ in range(kb - 1):
        X = _dot(X, X)
        yield
        T = T + _dot(T, X)
        yield
    size = base
    while size < C:
        ks = _log2(size)
        sel = ((ri >> (ks + 1)) == (ci >> (ks + 1))) & (((ri >> ks) & 1) == 1) & (((ci >> ks) & 1) == 0)
        TN_ = _dot(T, jnp.where(sel, N, 0.0))
        yield
        T = T + _dot(TN_, T)
        yield
        size *= 2
    return T


def _interleave(gens):
    results = [None] * len(gens)
    live = list(range(len(gens)))
    while live:
        for i in list(live):
            try:
                next(gens[i])
            except StopIteration as stop:
                results[i] = stop.value
                live.remove(i)
    return results


def _rwkv_groupnorm_gate(y, gate, bonus, lnw, lnb):
    bd = _head_block_ones()
    mean = _dot_sel(y, bd, 2) * (1.0 / N_D)
    yield
    dy = y - mean
    var = _dot(dy * dy, bd) * (1.0 / N_D)
    yield
    return (dy * lax.rsqrt(var + GN_EPS_D) * lnw + lnb + bonus) * gate


def _rwkv_chunk(r, ld, k, v, kk, be, gate, bonus, lnw, lnb, P, *, C, valid, first=0):
    if valid < C:
        live = _live_rows(C, first, valid)
        ld = jnp.where(live, ld, 0.0)
        be = jnp.where(live, be, 0.0)
        k = jnp.where(live, k, 0.0)
        v = jnp.where(live, v, 0.0)
    tril = _iota((C, C), 0) >= _iota((C, C), 1)
    c = _sel_dot(jnp.where(tril, 1.0, 0.0).astype(BF16), ld, 2)
    g_col = _dot_sel_tn(ld, jnp.ones((C, LANES), BF16), 2)
    yield
    ec, enc = jnp.exp(c), jnp.exp(-c)
    lane0 = _iota((C, LANES), 1) < N_D

    def stack(x):
        return jnp.concatenate([jnp.where(lane0, x, 0.0), jnp.where(lane0, 0.0, x)], axis=0)

    n2 = 2 * C
    AR = jnp.concatenate([stack(-kk * jnp.exp(c - ld)), stack(r * ec)], axis=0).astype(BF16)
    BK = jnp.concatenate([stack(be * enc), stack(k * enc)], axis=0).astype(BF16)
    V2 = stack(v).astype(BF16)
    kc = _log2(C)
    ri, ci = _iota((2 * n2, 2 * n2), 0), _iota((2 * n2, 2 * n2), 1)
    tpos, spos = ri & (C - 1), ci & (C - 1)
    keep = (((ri >> kc) & 1) == ((ci >> kc) & 1)) & ((spos < tpos) | ((ri >= n2) & (spos == tpos)))
    G = jnp.where(keep, _dg(AR, BK, _NT), 0.0)
    LP = _mm(AR, P.astype(BF16))
    yield
    GV = _mm(G[:, n2:].astype(BF16), V2)
    T = yield from _unit_lower_inverse(G[0:n2, 0:n2], C)
    U = _dot(T, LP[0:n2] + GV[0:n2])
    yield
    Y2 = LP[n2:] + GV[n2:] + _dot(G[n2:, 0:n2], U)
    P_new = jnp.exp(g_col) * (P + _dg(BK, jnp.concatenate([U.astype(BF16), V2], axis=0), _TN))
    yield
    y = Y2[0:C] + Y2[C:n2]
    out = yield from _rwkv_groupnorm_gate(y, gate, bonus, lnw, lnb)
    return out, P_new


def _rwkv_body(r_ref, ld_ref, k_ref, v_ref, kk_ref, be_ref, gt_ref, bo_ref, lnw_ref, lnb_ref, p0_ref,
               y_ref, p1_ref, *, C, valid, nb, hp, nchunk):
    @pl.when(pl.program_id(2) == 0)
    def _():
        p1_ref[...] = p0_ref[...]

    probs = [(bi, pp) for bi in range(nb) for pp in range(hp)]
    in_refs = (r_ref, ld_ref, k_ref, v_ref, kk_ref, be_ref, gt_ref, bo_ref)

    def chunk(c, carry):
        rows = pl.ds(pl.multiple_of(c * C, C), C)
        gens = []
        for bi, pp in probs:
            lanes = slice(pp * LANES, (pp + 1) * LANES)
            args = [ref[bi, rows, lanes] for ref in in_refs] + [lnw_ref[:, lanes], lnb_ref[:, lanes], p1_ref[bi, pp]]
            gens.append(_rwkv_chunk(*args, C=C, valid=valid))
        for (bi, pp), (y, p_new) in zip(probs, _interleave(gens)):
            p1_ref[bi, pp] = p_new
            y_ref[bi, rows, pp * LANES:(pp + 1) * LANES] = y
        return carry

    lax.fori_loop(0, nchunk, chunk, 0)


def _rwkv_decode_body(r_ref, ld_ref, k_ref, v_ref, kk_ref, be_ref, s0_ref, *rest):
    y_ref, s1_ref = rest[-2:]
    S = s0_ref[...]
    sa = -jnp.sum(S * kk_ref[...][None], axis=1)
    S = S * jnp.exp(ld_ref[...])[None] + sa[:, None, :] * be_ref[...][None] + v_ref[...][:, None, :] * k_ref[...][None]
    s1_ref[...] = S
    y_ref[...] = jnp.sum(S * r_ref[...][None], axis=1)


def _rwkv_decode(feats_fm, s_all, s_prev, j):
    n = feats_fm[0].shape[1]
    row = pl.BlockSpec((N_D, n), lambda h: (h, 0))
    st = pl.BlockSpec((None, None, N_D, N_D, n), lambda h: (j, h, 0, 0, 0))
    operands = [*feats_fm, s_all]
    specs = [row] * 6 + [st]
    aliases = _chained(s_prev, 1, operands, specs)
    return pl.pallas_call(
        _rwkv_decode_body,
        grid=(H_D,),
        in_specs=specs,
        out_specs=[row, st],
        out_shape=[jax.ShapeDtypeStruct((W_D, n), F32), jax.ShapeDtypeStruct(s_all.shape, F32)],
        input_output_aliases=aliases,
        compiler_params=_cp(("arbitrary",)),
        name="rwkv7_decode_step",
    )(*operands)


def _rwkv_decode_out_body(y_ref, gt_ref, bo_ref, lnw_ref, lnb_ref, o_ref):
    y = y_ref[...].T
    bd = _head_block_ones()
    dy = y - _head_sum(y, bd) * (1.0 / N_D)
    var = _head_sum(dy * dy, bd) * (1.0 / N_D)
    o_ref[...] = (dy * lax.rsqrt(var + GN_EPS_D) * lnw_ref[...] + lnb_ref[...] + bo_ref[...]) * gt_ref[...]


def _rwkv_decode_out(y_fm, gate, bonus, lnw, lnb):
    n = y_fm.shape[1]
    return pl.pallas_call(
        _rwkv_decode_out_body,
        out_shape=jax.ShapeDtypeStruct((n, W_D), F32),
        name="rwkv7_decode_groupnorm_gate",
    )(y_fm, gate, bonus, lnw, lnb)


def _rwkv(feats, lnw, lnb, p0, *, C, valid, nb, hp, tt):
    B, L, _ = feats[0].shape
    w = hp * LANES
    col = pl.BlockSpec((nb, tt, w), lambda b, p, i: (b, i, p))
    vec = pl.BlockSpec((1, w), lambda b, p, i: (0, p))
    st = pl.BlockSpec((nb, hp, LANES, LANES), lambda b, p, i: (b, p, 0, 0))
    return pl.pallas_call(
        functools.partial(_rwkv_body, C=C, valid=valid, nb=nb, hp=hp, nchunk=tt // C),
        grid=(B // nb, H_D // 2 // hp, L // tt),
        in_specs=[col] * 8 + [vec, vec, st],
        out_specs=[col, st],
        out_shape=[jax.ShapeDtypeStruct((B, L, W_D), F32), jax.ShapeDtypeStruct(p0.shape, F32)],
        compiler_params=_cp(("arbitrary", "arbitrary", "arbitrary")),
        name="rwkv7_chunked",
    )(*feats, lnw, lnb, p0)


def _pair_blockdiag(s):
    B = s.shape[0]
    st = s.reshape(B, H_D // 2, 2, N_D, N_D)
    zero = jnp.zeros_like(st[:, :, 0])
    top = jnp.concatenate([st[:, :, 0], zero], axis=-1)
    bot = jnp.concatenate([zero, st[:, :, 1]], axis=-1)
    return jnp.concatenate([top, bot], axis=-2)


def _pair_unblockdiag(p):
    B = p.shape[0]
    st = jnp.stack([p[:, :, :N_D, :N_D], p[:, :, N_D:, N_D:]], axis=2)
    return st.reshape(B, H_D, N_D, N_D)


def _trunk(x, mod, st, prm, *, decode):
    G, Lg, _ = x.shape
    tm = min(128 if decode else 512, Lg)
    tm_in = tm if decode or Lg % 1024 else 1024
    assert decode or Lg % 512 == 0
    nseq = Lg if decode else G
    new = {k: [] for k in ("a_conv", "a_h", "d_shift", "d_s")}
    b_s = None
    c_in = (st["c_c"], st["c_n"][:, :, :, None, :],
            jnp.broadcast_to(st["c_m"][..., None, None], st["c_m"].shape + (1, LANES)))
    c_out = d_s = None
    d_s_in = jnp.transpose(st["d_s"], (0, 2, 3, 4, 1)) if decode else None
    blocks = lambda t: t.reshape(nseq // DEC_NB, DEC_NB, -1)
    ffn_bf16 = {} if decode else prm["ffn_bf16"]

    def ffn(x, l, which):
        if decode:
            assert Lg == tm
            x, ffn_bf16[l, which] = _ffn_cast(x, mod, prm["ffn_w_gate"], prm["ffn_w_up"], prm["ffn_w_down"], l, which)
            return x
        return _ffn(x, mod, ffn_bf16[l, which], l, which, tm)

    for l in range(DEPTH):
        j = l // 2
        x = ffn(x, l, 0)
        if l % 2 == 0:
            z = _inproj(x, mod, prm["w_in_even"], l, j, tm_in, tn=1024)
            a_prm = prm["rglru"][j]
            conv0, h0 = st["a_conv"][j], st["a_h"][j]
            if decode:
                z2 = z[0]
                y1, h1 = _rglru_step(z2, conv0, h0, a_prm)
                conv1 = jnp.concatenate([conv0[:, 1:], z2[:, None, :W_A]], axis=1)
                yb, b_s = _hgrn2(blocks(z2), prm["lb"][j], prm["b_norm_w"][j], st["b_s"], b_s, j,
                                 C=DEC_NB, valid=1, nb=DEC_NB, hp=4, tt=DEC_NB, decode=True)
                y1, y2 = y1[None], yb.reshape(1, nseq, -1)
            else:
                buf8 = jnp.concatenate([jnp.zeros((nseq, 8 - (CONV_W - 1), W_A), F32), conv0], axis=1)
                a, b = _rglru_seq(z, buf8, a_prm, 512)
                y1, h1 = _rglru_scan(a, b, z, h0, 512)
                conv1 = z[:, Lg - (CONV_W - 1):, :W_A]
                y2, b_s = _hgrn2(z, prm["lb"][j], prm["b_norm_w"][j], st["b_s"], b_s, j,
                                 C=CHUNK, valid=CHUNK, nb=nseq, hp=2, tt=512, decode=False)
            new["a_conv"].append(conv1)
            new["a_h"].append(h1)
            x = _outproj(x, mod, y1, y2, prm["w_out_even"], l, j, tm)
        else:
            z = _inproj(x, mod, prm["w_in_odd"], l, j, tm_in, width=MLSTM_COLS, tn=MLSTM_COLS // 4, w_is_out_by_in=True)
            zd = _inproj(x, mod, prm["w_in_rwkv"], l, j, tm, tn=P_D // 2, w_is_out_by_in=True)
            d_prm = prm["rwkv"][j]
            if decode:
                yc, c_out = _mlstm(blocks(z[0]), prm["c_gate_b"][j], prm["c_norm_w"][j], c_in, c_out, j,
                                   C=DEC_NB, valid=1, nb=DEC_NB, hp=4, tt=DEC_NB, decode=True)
                feats_fm, gate, bonus = _rwkv_pre_step(zd[0], st["d_shift"][j], d_prm)
                y_fm, d_s = _rwkv_decode(feats_fm, d_s_in, d_s, j)
                yd = _rwkv_decode_out(y_fm, gate, bonus, prm["d_ln_w"][j], prm["d_ln_b"][j])
                y1, y2 = yc.reshape(1, nseq, -1), yd[None]
                shift1 = zd[0]
            else:
                y1, c_out = _mlstm(z, prm["c_gate_b"][j], prm["c_norm_w"][j], c_in, c_out, j,
                                   C=CHUNK, valid=CHUNK, nb=nseq, hp=2, tt=256, decode=False)
                prev8 = jnp.concatenate([jnp.zeros((nseq, 7, P_D), F32), st["d_shift"][j][:, None]], axis=1)
                feats = _rwkv_pre_seq(zd, prev8, d_prm, 128)
                p0 = _pair_blockdiag(jnp.swapaxes(st["d_s"][j], -1, -2))
                y2, p1 = _rwkv(feats, prm["d_ln_w"][j], prm["d_ln_b"][j], p0, C=CHUNK, valid=CHUNK, nb=nseq, hp=2,
                               tt=256)
                shift1 = zd[:, -1]
                new["d_s"].append(jnp.swapaxes(_pair_unblockdiag(p1), -1, -2))
            new["d_shift"].append(shift1)
            x = _outproj(x, mod, y1, y2, prm["w_out_odd"], l, j, tm)
        x = ffn(x, l, 1)
    y = _final_norm(x, prm["final_norm_w"], tm)
    d_s = jnp.transpose(d_s, (0, 4, 1, 2, 3)) if decode else jnp.stack(new["d_s"])
    outs = (y, jnp.stack(new["a_conv"]), jnp.stack(new["a_h"]), b_s, c_out[0], c_out[1][:, :, :, 0, :],
            c_out[2][:, :, :, 0, 0], jnp.stack(new["d_shift"]), d_s)
    return outs, ffn_bf16


def _prepare(w_mod, b_mod, ffn_w_gate, ffn_w_up, ffn_w_down, w_in_even, w_out_even, a_conv_w, a_conv_b, a_gate_r_w,
             a_gate_r_b, a_gate_i_w, a_gate_i_b, a_lambda, b_lb_gamma, b_norm_w, w_in_odd, w_out_odd, c_igate_b,
             c_fgate_b, c_norm_w, d_mu, d_w0, d_w2, d_a0, d_a2, d_g2, d_k_k, d_k_a, d_r_k, d_ln_w, d_ln_b,
             final_norm_w):
    n_gate = 2 * H_C
    w_odd_t = jnp.swapaxes(w_in_odd, 1, 2).astype(BF16)
    row = lambda t: t[:, None, :]
    zpad = jnp.zeros((N_ODD, LANES - n_gate), F32)
    half = jnp.zeros((N_ODD, R_W, W_D), F32)
    rglru = [(a_conv_w[j], a_conv_b[j][None], a_gate_r_w[j].astype(BF16), a_gate_r_b[j][None],
              a_gate_i_w[j].astype(BF16), a_gate_i_b[j][None], a_lambda[j][None]) for j in range(N_EVEN)]
    w2p = jnp.concatenate([d_w2, half], axis=1).astype(BF16)
    a2p = jnp.concatenate([half, d_a2], axis=1).astype(BF16)
    g2 = d_g2.astype(BF16)
    rwkv = [(d_mu[j][None], d_w0[j][None], d_a0[j][None], d_k_k[j][None], d_k_a[j][None],
             d_r_k[j].reshape(1, W_D), w2p[j], a2p[j], g2[j]) for j in range(N_ODD)]
    return dict(
        ffn_w_gate=ffn_w_gate, ffn_w_up=ffn_w_up, ffn_w_down=ffn_w_down,
        w_in_even=w_in_even.astype(BF16), w_in_odd=w_odd_t, w_in_rwkv=w_odd_t[:, P_ODD - P_D:],
        w_out_even=w_out_even.astype(BF16).reshape(N_EVEN, 2, D // 2, D),
        w_out_odd=w_out_odd.astype(BF16).reshape(N_ODD, 2, D // 2, D),
        rglru=rglru, rwkv=rwkv,
        lb=row(_lower_bounds(b_lb_gamma)), b_norm_w=row(b_norm_w),
        c_gate_b=row(jnp.concatenate([c_igate_b, c_fgate_b, zpad], axis=-1)), c_norm_w=row(c_norm_w),
        d_ln_w=row(d_ln_w), d_ln_b=row(d_ln_b), final_norm_w=final_norm_w)


def kernel(x_prompt, x_sample, c_prompt, c_sample, state_a_conv, state_a_h, state_b_s, state_c_c, state_c_n, state_c_m, state_d_shift, state_d_s, w_mod, b_mod, ffn_w_gate, ffn_w_up, ffn_w_down, w_in_even, w_out_even, a_conv_w, a_conv_b, a_gate_r_w, a_gate_r_b, a_gate_i_w, a_gate_i_b, a_lambda, b_lb_gamma, b_norm_w, w_in_odd, w_out_odd, c_igate_b, c_fgate_b, c_norm_w, d_mu, d_w0, d_w2, d_a0, d_a2, d_g2, d_k_k, d_k_a, d_r_k, d_ln_w, d_ln_b, final_norm_w):
    prm = _prepare(w_mod, b_mod, ffn_w_gate, ffn_w_up, ffn_w_down, w_in_even, w_out_even, a_conv_w, a_conv_b,
                   a_gate_r_w, a_gate_r_b, a_gate_i_w, a_gate_i_b, a_lambda, b_lb_gamma, b_norm_w, w_in_odd,
                   w_out_odd, c_igate_b, c_fgate_b, c_norm_w, d_mu, d_w0, d_w2, d_a0, d_a2, d_g2, d_k_k, d_k_a,
                   d_r_k, d_ln_w, d_ln_b, final_norm_w)
    bp, lp, _ = x_prompt.shape
    bs = x_sample.shape[0]
    n_rows = -(-(bs + bp) // 8) * 8
    c_all = jnp.concatenate([c_sample, c_prompt, jnp.zeros((n_rows - bs - bp, D), F32)], axis=0)
    mod_all = _modulation(c_all, w_mod, b_mod)
    mod_s = mod_all[:, None]
    mod_p = mod_all[:, bs:bs + bp, None]

    zeros = lambda *s: jnp.zeros(s, F32)
    st_p = dict(a_conv=zeros(N_EVEN, bp, CONV_W - 1, W_A), a_h=zeros(N_EVEN, bp, W_A),
                b_s=zeros(N_EVEN, bp, H_B, DK_B, DV_B), c_c=zeros(N_ODD, bp, H_C, DK_C, DV_C),
                c_n=zeros(N_ODD, bp, H_C, DK_C), c_m=zeros(N_ODD, bp, H_C), d_shift=zeros(N_ODD, bp, P_D),
                d_s=zeros(N_ODD, bp, H_D, N_D, N_D))
    st_s = dict(a_conv=state_a_conv, a_h=state_a_h, b_s=state_b_s, c_c=state_c_c, c_n=state_c_n, c_m=state_c_m,
                d_shift=state_d_shift, d_s=state_d_s)
    out_s, ffn_bf16 = _trunk(x_sample.reshape(1, bs, D), mod_s, st_s, prm, decode=True)
    out_p, _ = _trunk(x_prompt, mod_p, st_p, dict(prm, ffn_bf16=ffn_bf16), decode=False)
    y_s = out_s[0].reshape(bs, 1, D)
    return (out_p[0], y_s) + out_p[1:] + out_s[1:]
```
